```python
import math
import jax
import jax.numpy as jnp
from jax import lax
import numpy as np

D_MODEL = 1024
BATCH = 32
SEQ = 2048
DEPTH = 4
DEC_BATCH = 2
DEC_SEQ = 16384
PAST_LEN = 128

GRID_W = 64
D_MIX = D_MODEL
HEAD_DIM = 64
NA_WIDTH = D_MIX // 4
NA_HEADS = NA_WIDTH // HEAD_DIM
NA_KH_MAX = 8
NA_KW = 16
HG_WIDTH = D_MIX // 4
HG_HEADS = HG_WIDTH // HEAD_DIM
HG_CHUNK = 64
HY_WIDTH = D_MIX // 4
HY_ORDER = 2
HY_EMB = 33
HY_BANDS = (HY_EMB - 1) // 2
HY_HIDDEN = 64
HY_DECAY_TARGET = 1e-2
HY_FAST_PCT = 0.3
HY_SLOW_PCT = 1.5
MLA_WIDTH = D_MIX - NA_WIDTH - HG_WIDTH - HY_WIDTH
MLA_HEADS = 4
MLA_NOPE = 64
MLA_ROPE = 32
MLA_V = MLA_WIDTH // MLA_HEADS
MLA_Q_LORA = 256
MLA_KV_LORA = 128
ROPE_THETA = 10000.0
Q_BLOCK = 128
D_FF = 2816
EPS = 1e-6

NA_COLS = 3 * NA_WIDTH
HG_COLS = 5 * HG_WIDTH
HY_COLS = 3 * HY_WIDTH
MLA_COLS = MLA_Q_LORA + MLA_KV_LORA + MLA_ROPE
D_IN = NA_COLS + HG_COLS + HY_COLS + MLA_COLS
SPLIT_POINTS = (NA_COLS, NA_COLS + HG_COLS, NA_COLS + HG_COLS + HY_COLS)

kernel_name = 'hybrid_bidir_encoder_trunk'


def rmsnorm(x, g):
    xf = x.astype(jnp.float32)
    y = xf * lax.rsqrt(jnp.mean(xf * xf, axis=-1, keepdims=True) + EPS)
    return (y * g.astype(jnp.float32)).astype(x.dtype)


def dwconv3(x, w):
    xp = jnp.pad(x, ((0, 0), (1, 1), (0, 0)))
    return w[0] * xp[:, :-2] + w[1] * xp[:, 1:-1] + w[2] * xp[:, 2:]


def rope(x):
    L = x.shape[1]
    half = x.shape[-1] // 2
    inv = ROPE_THETA ** (-jnp.arange(half, dtype=jnp.float32) / half)
    ang = jnp.arange(L, dtype=jnp.float32)[:, None] * inv[None, :]
    cos = jnp.cos(ang)[None, :, None, :]
    sin = jnp.sin(ang)[None, :, None, :]
    xf = x.astype(jnp.float32)
    x1, x2 = xf[..., :half], xf[..., half:]
    return jnp.concatenate([x1 * cos - x2 * sin, x2 * cos + x1 * sin], axis=-1).astype(x.dtype)


def neighbourhood_attention(q, k, v, rpb):
    B, L, H, Dh = q.shape
    rows = L // GRID_W
    kh = min(NA_KH_MAX, rows)
    r = jnp.arange(rows)
    row_idx = jnp.clip(r - kh // 2, 0, rows - kh)[:, None] + jnp.arange(kh)[None, :]
    c = jnp.arange(GRID_W)
    col_start = jnp.clip(c - NA_KW // 2, 0, GRID_W - NA_KW)
    col_ok = (c[None, :] >= col_start[:, None]) & (c[None, :] < col_start[:, None] + NA_KW)
    qg = q.reshape(B, rows, GRID_W, H, Dh)
    kg = k.reshape(B, rows, GRID_W, H, Dh)[:, row_idx]
    vg = v.reshape(B, rows, GRID_W, H, Dh)[:, row_idx]
    s = jnp.einsum('brqhd,brikhd->bhrqik', qg, kg).astype(jnp.float32) * (Dh ** -0.5)
    dr = row_idx - r[:, None] + (NA_KH_MAX - 1)
    dc = jnp.clip(c[None, :] - c[:, None] + (NA_KW - 1), 0, 2 * NA_KW - 2)
    bias = rpb.astype(jnp.float32)[:, dr][..., dc]
    s = s + bias.transpose(0, 1, 3, 2, 4)[None]
    s = jnp.where(col_ok[:, None, :], s, -jnp.inf)
    p = jax.nn.softmax(s.reshape(B, H, rows, GRID_W, kh * GRID_W), axis=-1).reshape(s.shape)
    o = jnp.einsum('bhrqik,brikhd->brqhd', p.astype(v.dtype), vg)
    return o.reshape(B, L, H * Dh)


def hgrn2_bidirectional(pb, lb, norm_g):
    B, L, _ = pb.shape
    n, C, H = L // HG_CHUNK, HG_CHUNK, HG_HEADS
    d = HG_WIDTH // H
    q, f_fwd, f_bwd, i, g = jnp.split(pb.astype(jnp.float32), 5, axis=-1)
    lbb = lb.astype(jnp.float32)[:, None, None, :]
    forget = lbb + (1.0 - lbb) * jax.nn.sigmoid(jnp.stack([f_fwd, f_bwd[:, ::-1]]))
    q = jax.nn.silu(q)
    qq = jnp.stack([q, q[:, ::-1]])
    vv = jnp.stack([i, i[:, ::-1]])

    def chunks(a):
        return a.reshape(2, B, n, C, H, d).transpose(2, 0, 1, 4, 3, 5)

    qc, kc, vc, gc = chunks(qq), chunks(1.0 - forget), chunks(vv), chunks(jnp.log(forget))
    tri = jnp.tril(jnp.ones((C, C), dtype=bool))[:, :, None]

    def step(S, inp):
        qt, kt, vt, lg = inp
        b = jnp.cumsum(lg, axis=-2)
        b_end = b[..., -1:, :]
        diff = b[..., :, None, :] - b[..., None, :, :]
        decay = jnp.where(tri, jnp.exp(jnp.where(tri, diff, 0.0)), 0.0)
        scores = jnp.einsum('zbhtk,zbhsk,zbhtsk->zbhts', qt, kt, decay)
        o = jnp.einsum('zbhts,zbhsv->zbhtv', scores, vt) + jnp.einsum('zbhtk,zbhkv->zbhtv', qt * jnp.exp(b), S)
        S = jnp.exp(b_end)[..., 0, :, None] * S + jnp.einsum('zbhsk,zbhsv->zbhkv', kt * jnp.exp(b_end - b), vt)
        return S, o

    S0 = jnp.zeros((2, B, H, d, d), jnp.float32)
    _, o = lax.scan(step, S0, (qc, kc, vc, gc))
    o = o.transpose(1, 2, 0, 4, 3, 5).reshape(2, B, L, H, d)
    o = o[0] + o[1][:, ::-1]
    o = rmsnorm(o, norm_g.reshape(H, d)).reshape(B, L, HG_WIDTH)
    return (o * jax.nn.silu(g)).astype(pb.dtype)


def hyena_filters(L, w1, b1, freq, w2, b2, w3):
    f32 = jnp.float32
    t = jnp.linspace(0.0, 1.0, L, dtype=f32)[:, None]
    w = (2.0 * math.pi / L) * jnp.arange(L, dtype=f32)[:, None]
    bands = jnp.linspace(1e-4, HY_BANDS - 1, HY_BANDS, dtype=f32)[None, :]
    z = jnp.concatenate([t, jnp.cos(bands * w), -jnp.sin(bands * w)], axis=-1)
    freq = freq.astype(f32)
    a = jnp.sin(freq * (z @ w1.astype(f32) + b1.astype(f32)))
    a = jnp.sin(freq * (a @ w2.astype(f32) + b2.astype(f32)))
    h = (a @ w3.astype(f32)).reshape(L, 2, HY_ORDER, HY_WIDTH)
    deltas = jnp.abs(jnp.linspace(math.log(HY_DECAY_TARGET) / HY_SLOW_PCT,
                                  math.log(HY_DECAY_TARGET) / HY_FAST_PCT, HY_WIDTH, dtype=f32))
    h = h * jnp.exp(-t * deltas)[:, None, None, :]
    h = h / jnp.sum(jnp.abs(h), axis=(0, 1), keepdims=True)
    h_fwd, h_bwd = h[:, 0], h[:, 1]
    return jnp.concatenate([h_fwd.at[0].add(h_bwd[0]), jnp.zeros_like(h_fwd[:1]), h_bwd[:0:-1]], axis=0)


def hyena_bidirectional(pc, short_w, w1, b1, freq, w2, b2, w3, skip):
    B, L, _ = pc.shape
    v, x1, x2 = jnp.split(dwconv3(pc, short_w).astype(jnp.float32), 3, axis=-1)
    hf = jnp.fft.rfft(hyena_filters(L, w1, b1, freq, w2, b2, w3), axis=0)
    skip = skip.astype(jnp.float32)
    z = v
    for n, gate in enumerate((x1, x2)):
        zc = jnp.fft.irfft(jnp.fft.rfft(z, n=2 * L, axis=1) * hf[None, :, n], n=2 * L, axis=1)[:, :L]
        z = gate * (zc + skip[n] * z)
    return z.astype(pc.dtype)


def mla_attention(pd, q_norm, kv_norm, w_uq, w_ukv):
    B, L, _ = pd.shape
    H = MLA_HEADS
    c_q, c_kv, k_rope = jnp.split(pd, (MLA_Q_LORA, MLA_Q_LORA + MLA_KV_LORA), axis=-1)
    q = (rmsnorm(c_q, q_norm) @ w_uq).reshape(B, L, H, MLA_NOPE + MLA_ROPE)
    kv = (rmsnorm(c_kv, kv_norm) @ w_ukv).reshape(B, L, H, MLA_NOPE + MLA_V)
    q_nope, q_rope = q[..., :MLA_NOPE], rope(q[..., MLA_NOPE:])
    k_nope, v = kv[..., :MLA_NOPE], kv[..., MLA_NOPE:]
    k_rope = rope(k_rope[:, :, None, :])[:, :, 0]
    scale = (MLA_NOPE + MLA_ROPE) ** -0.5
    nb = L // Q_BLOCK

    def block(qs):
        qn, qr = qs
        s = jnp.einsum('bqhd,bkhd->bhqk', qn, k_nope) + jnp.einsum('bqhr,bkr->bhqk', qr, k_rope)
        p = jax.nn.softmax(s.astype(jnp.float32) * scale, axis=-1)
        return jnp.einsum('bhqk,bkhd->bqhd', p.astype(v.dtype), v)

    qn_b = q_nope.reshape(B, nb, Q_BLOCK, H, MLA_NOPE).swapaxes(0, 1)
    qr_b = q_rope.reshape(B, nb, Q_BLOCK, H, MLA_ROPE).swapaxes(0, 1)
    o = lax.map(block, (qn_b, qr_b))
    return o.swapaxes(0, 1).reshape(B, L, MLA_WIDTH)


def conv_ffn(h, w_up, conv_w, w_down):
    u = dwconv3(h @ w_up, conv_w)
    a, b = jnp.split(u, 2, axis=-1)
    return (jax.nn.gelu(a, approximate=True) * b) @ w_down


def trunk_layer(x, cond, lb, p):
    B, L, _ = x.shape
    mod = jax.nn.silu(cond) @ p['ada_w'] + p['ada_b']
    sh1, sc1, g1, sh2, sc2, g2 = jnp.split(mod[:, None, :], 6, axis=-1)
    h = rmsnorm(x, p['norm_g'][0]) * (1.0 + sc1) + sh1
    pa, pb, pc, pd = jnp.split(h @ p['w_in'], SPLIT_POINTS, axis=-1)
    qa, ka, va = (t.reshape(B, L, NA_HEADS, HEAD_DIM) for t in jnp.split(pa, 3, axis=-1))
    o_na = neighbourhood_attention(qa, ka, va, p['na_rpb'])
    o_hg = hgrn2_bidirectional(pb, lb, p['hg_norm'])
    o_hy = hyena_bidirectional(pc, p['hy_short'], p['hy_w1'], p['hy_b1'], p['hy_freq'],
                               p['hy_w2'], p['hy_b2'], p['hy_w3'], p['hy_skip'])
    o_mla = mla_attention(pd, p['mla_q_norm'], p['mla_kv_norm'], p['mla_w_uq'], p['mla_w_ukv'])
    mix = jnp.concatenate([o_na, o_hg, o_hy, o_mla], axis=-1).astype(x.dtype) @ p['w_out']
    x = x + g1 * rmsnorm(mix, p['norm_g'][1])
    h = rmsnorm(x, p['norm_g'][2]) * (1.0 + sc2) + sh2
    f = conv_ffn(h, p['ffn_w_up'], p['ffn_conv'], p['ffn_w_down'])
    return x + g2 * rmsnorm(f, p['norm_g'][3])


def setup_inputs(seed: int = 0) -> dict:
    key = jax.random.key(seed)
    keys = iter(jax.random.split(key, 32))

    def nrm(shape, scale):
        return scale * jax.random.normal(next(keys), shape, jnp.float32)

    def gain(shape, noise=0.05):
        return 1.0 + nrm(shape, noise)

    return {
        'x_prompt': nrm((BATCH, SEQ, D_MODEL), 1.0),
        'x_sample': nrm((DEC_BATCH, DEC_SEQ, D_MODEL), 1.0),
        'c_prompt': nrm((BATCH, D_MODEL), 1.0),
        'c_sample': nrm((DEC_BATCH, D_MODEL), 1.0),
        'ada_w': nrm((DEPTH, D_MODEL, 6 * D_MODEL), D_MODEL ** -0.5),
        'ada_b': nrm((DEPTH, 6 * D_MODEL), 0.01),
        'norm_g': gain((DEPTH, 4, D_MODEL)),
        'w_in': nrm((DEPTH, D_MODEL, D_IN), D_MODEL ** -0.5),
        'w_out': nrm((DEPTH, D_MIX, D_MODEL), D_MIX ** -0.5),
        'na_rpb': nrm((DEPTH, NA_HEADS, 2 * NA_KH_MAX - 1, 2 * NA_KW - 1), 0.1),
        'hg_lb': nrm((DEPTH, 2, HG_WIDTH), 1.0),
        'hg_norm': gain((DEPTH, HG_WIDTH)),
        'hy_short': nrm((DEPTH, 3, HY_COLS), 3 ** -0.5),
        'hy_w1': nrm((DEPTH, HY_EMB, HY_HIDDEN), HY_EMB ** -0.5),
        'hy_b1': nrm((DEPTH, HY_HIDDEN), 0.1),
        'hy_freq': gain((DEPTH, HY_HIDDEN), 0.1),
        'hy_w2': nrm((DEPTH, HY_HIDDEN, HY_HIDDEN), HY_HIDDEN ** -0.5),
        'hy_b2': nrm((DEPTH, HY_HIDDEN), 0.1),
        'hy_w3': nrm((DEPTH, HY_HIDDEN, 2 * HY_ORDER * HY_WIDTH), HY_HIDDEN ** -0.5),
        'hy_skip': nrm((DEPTH, HY_ORDER, HY_WIDTH), 0.5),
        'mla_q_norm': gain((DEPTH, MLA_Q_LORA)),
        'mla_kv_norm': gain((DEPTH, MLA_KV_LORA)),
        'mla_w_uq': nrm((DEPTH, MLA_Q_LORA, MLA_HEADS * (MLA_NOPE + MLA_ROPE)), MLA_Q_LORA ** -0.5),
        'mla_w_ukv': nrm((DEPTH, MLA_KV_LORA, MLA_HEADS * (MLA_NOPE + MLA_V)), MLA_KV_LORA ** -0.5),
        'ffn_w_up': nrm((DEPTH, D_MODEL, 2 * D_FF), D_MODEL ** -0.5),
        'ffn_conv': nrm((DEPTH, 3, 2 * D_FF), 3 ** -0.5),
        'ffn_w_down': nrm((DEPTH, D_FF, D_MODEL), D_FF ** -0.5),
    }


def reference(x_prompt, x_sample, c_prompt, c_sample, ada_w, ada_b, norm_g, w_in, w_out,
              na_rpb, hg_lb, hg_norm, hy_short, hy_w1, hy_b1, hy_freq, hy_w2, hy_b2, hy_w3,
              hy_skip, mla_q_norm, mla_kv_norm, mla_w_uq, mla_w_ukv, ffn_w_up, ffn_conv, ffn_w_down):
    lb_soft = jax.nn.softmax(hg_lb.astype(jnp.float32), axis=0)
    lower_bounds = jnp.cumsum(lb_soft, axis=0) - lb_soft[:1]
    y_prompt = x_prompt
    y_sample = x_sample
    for l in range(DEPTH):
        p = {
            'ada_w': ada_w[l], 'ada_b': ada_b[l], 'norm_g': norm_g[l],
            'w_in': w_in[l], 'w_out': w_out[l], 'na_rpb': na_rpb[l], 'hg_norm': hg_norm[l],
            'hy_short': hy_short[l], 'hy_w1': hy_w1[l], 'hy_b1': hy_b1[l], 'hy_freq': hy_freq[l],
            'hy_w2': hy_w2[l], 'hy_b2': hy_b2[l], 'hy_w3': hy_w3[l], 'hy_skip': hy_skip[l],
            'mla_q_norm': mla_q_norm[l], 'mla_kv_norm': mla_kv_norm[l],
            'mla_w_uq': mla_w_uq[l], 'mla_w_ukv': mla_w_ukv[l],
            'ffn_w_up': ffn_w_up[l], 'ffn_conv': ffn_conv[l], 'ffn_w_down': ffn_w_down[l],
        }
        y_prompt = trunk_layer(y_prompt, c_prompt, lower_bounds[l], p)
        y_sample = trunk_layer(y_sample, c_sample, lower_bounds[l], p)
    return (y_prompt, y_sample)
```

```python
import functools
import math

import jax
import jax.numpy as jnp
from jax import lax
from jax.experimental import pallas as pl
from jax.experimental.pallas import tpu as pltpu

F32 = jnp.float32
BF16 = jnp.bfloat16

D_MODEL = 1024
DEPTH = 4
GRID_W = 64
HEAD_DIM = 64
NA_WIDTH = 256
NA_HEADS = 4
NA_KH = 8
NA_KW = 16
HG_WIDTH = 256
HG_HEADS = 4
HG_CHUNK = 64
HY_WIDTH = 256
HY_ORDER = 2
HY_EMB = 33
HY_BANDS = 16
HY_HIDDEN = 64
HY_DECAY_TARGET = 1e-2
HY_FAST_PCT = 0.3
HY_SLOW_PCT = 1.5
MLA_WIDTH = 256
MLA_HEADS = 4
MLA_NOPE = 64
MLA_ROPE = 32
MLA_V = 64
MLA_Q_LORA = 256
MLA_KV_LORA = 128
MLA_HEAD_PAD = 128
ROPE_THETA = 10000.0
D_FF = 2816
EPS = 1e-6
NA_COLS = 3 * NA_WIDTH
HG_COLS = 5 * HG_WIDTH
HY_COLS = 3 * HY_WIDTH
MLA_COLS = MLA_Q_LORA + MLA_KV_LORA + MLA_ROPE
MLA_COLS_PAD = 512
NEG_BIG = -1e30


def _dot(a, b):
    return jnp.dot(a, b, preferred_element_type=F32)


def _dot_nt(a, b):
    return lax.dot_general(a, b, (((1,), (1,)), ((), ())), preferred_element_type=F32)


def _rms(x, g):
    return x * lax.rsqrt(jnp.mean(x * x, axis=-1, keepdims=True) + EPS) * g


def _ada_kernel(c_ref, w_ref, b_ref, o_ref):
    c = c_ref[...]
    s = c * jax.nn.sigmoid(c)
    o_ref[0] = _dot(s.astype(BF16), w_ref[0].astype(BF16)) + b_ref[0]


def _ada_modulation(cond, ada_w, ada_b):
    R = cond.shape[0]
    tn = 1536
    return pl.pallas_call(
        _ada_kernel,
        grid=(DEPTH, 6 * D_MODEL // tn),
        in_specs=[
            pl.BlockSpec((R, D_MODEL), lambda l, n: (0, 0)),
            pl.BlockSpec((1, D_MODEL, tn), lambda l, n: (l, 0, n)),
            pl.BlockSpec((1, 1, tn), lambda l, n: (l, 0, n)),
        ],
        out_specs=pl.BlockSpec((1, R, tn), lambda l, n: (l, 0, n)),
        out_shape=jax.ShapeDtypeStruct((DEPTH, R, 6 * D_MODEL), F32),
        name="ada_modulation",
    )(cond, ada_w, ada_b.reshape(DEPTH, 1, 6 * D_MODEL))


def _in_proj_kernel(x_ref, mod_ref, g_ref, wna, whg, why, wml, ona, ohg, ohy, oml):
    x = x_ref[0]
    h = _rms(x, g_ref[0:1, :]) * (1.0 + mod_ref[0, 1:2, :]) + mod_ref[0, 0:1, :]
    hb = h.astype(BF16)
    ona[0] = _dot(hb, wna[...])
    ohg[0] = _dot(hb, whg[...])
    ohy[0] = _dot(hb, why[...])
    oml[0] = _dot(hb, wml[...])


def _in_proj(x, mod, g, w_na, w_hg, w_hy, w_ml, tm=512):
    B, L, D = x.shape
    tm = min(tm, L)
    full = lambda a: pl.BlockSpec(a.shape, lambda b, i: (0, 0))
    outs = [NA_COLS, HG_COLS, HY_COLS, MLA_COLS_PAD]
    return pl.pallas_call(
        _in_proj_kernel,
        grid=(B, L // tm),
        in_specs=[
            pl.BlockSpec((1, tm, D), lambda b, i: (b, i, 0)),
            pl.BlockSpec((1, 8, D), lambda b, i: (b, 0, 0)),
            full(g), full(w_na), full(w_hg), full(w_hy), full(w_ml),
        ],
        out_specs=[pl.BlockSpec((1, tm, n), lambda b, i: (b, i, 0)) for n in outs],
        out_shape=[jax.ShapeDtypeStruct((B, L, n), F32) for n in outs],
        compiler_params=pltpu.CompilerParams(dimension_semantics=("parallel", "parallel")),
        name="in_proj",
    )(x, mod, g, w_na, w_hg, w_hy, w_ml)


def _out_proj_kernel(x_ref, mod_ref, g_ref, na, hg, hy, ml, w_ref, x1_ref, h2_ref):
    W = NA_WIDTH
    mix = _dot(na[0].astype(BF16), w_ref[0 * W:1 * W, :])
    mix += _dot(hg[0].astype(BF16), w_ref[1 * W:2 * W, :])
    mix += _dot(hy[0].astype(BF16), w_ref[2 * W:3 * W, :])
    mix += _dot(ml[0].astype(BF16), w_ref[3 * W:4 * W, :])
    x1 = x_ref[0] + mod_ref[0, 2:3, :] * _rms(mix, g_ref[1:2, :])
    x1_ref[0] = x1
    h2 = _rms(x1, g_ref[2:3, :]) * (1.0 + mod_ref[0, 4:5, :]) + mod_ref[0, 3:4, :]
    h2_ref[0] = h2.astype(BF16)


def _out_proj(x, mod, g, o_na, o_hg, o_hy, o_ml, w_out, tm=512):
    B, L, D = x.shape
    tm = min(tm, L)
    tok = lambda n: pl.BlockSpec((1, tm, n), lambda b, i: (b, i, 0))
    return pl.pallas_call(
        _out_proj_kernel,
        grid=(B, L // tm),
        in_specs=[
            tok(D),
            pl.BlockSpec((1, 8, D), lambda b, i: (b, 0, 0)),
            pl.BlockSpec(g.shape, lambda b, i: (0, 0)),
            tok(NA_WIDTH), tok(HG_WIDTH), tok(HY_WIDTH), tok(MLA_WIDTH),
            pl.BlockSpec(w_out.shape, lambda b, i: (0, 0)),
        ],
        out_specs=[tok(D), tok(D)],
        out_shape=[jax.ShapeDtypeStruct((B, L, D), F32), jax.ShapeDtypeStruct((B, L, D), BF16)],
        compiler_params=pltpu.CompilerParams(dimension_semantics=("parallel", "parallel")),
        name="out_proj",
    )(x, mod, g, o_na, o_hg, o_hy, o_ml, w_out)


FFN_HALO = 16


def _ffn_kernel(hm_ref, hp_ref, hn_ref, x_ref, mod_ref, g_ref, wa_ref, wb_ref, ca_ref, cb_ref,
                wd_ref, o_ref, acc_ref):
    i = pl.program_id(1)
    f = pl.program_id(2)
    tm = hm_ref.shape[1]
    not_first = (i > 0).astype(F32)
    not_last = (i < pl.num_programs(1) - 1).astype(F32)
    row = lax.broadcasted_iota(jnp.int32, (tm, 1), 0)
    hm = hm_ref[0]
    hp = hp_ref[0]
    hn = hn_ref[0]

    def conv_branch(w_ref, c_ref):
        u = _dot(hm, w_ref[...])
        u_prev = _dot(hp, w_ref[...])[FFN_HALO - 1:FFN_HALO, :] * not_first
        u_next = _dot(hn, w_ref[...])[0:1, :] * not_last
        um1 = jnp.where(row == 0, u_prev, pltpu.roll(u, 1, 0))
        up1 = jnp.where(row == tm - 1, u_next, pltpu.roll(u, tm - 1, 0))
        return c_ref[0:1, :] * um1 + c_ref[1:2, :] * u + c_ref[2:3, :] * up1

    a = conv_branch(wa_ref, ca_ref)
    b = conv_branch(wb_ref, cb_ref)
    act = (jax.nn.gelu(a, approximate=True) * b).astype(BF16)
    part = _dot(act, wd_ref[...])

    @pl.when(f == 0)
    def _():
        acc_ref[...] = part

    @pl.when(f > 0)
    def _():
        acc_ref[...] += part

    @pl.when(f == pl.num_programs(2) - 1)
    def _():
        o_ref[0] = x_ref[0] + mod_ref[0, 5:6, :] * _rms(acc_ref[...], g_ref[3:4, :])


def _conv_ffn(h2, x1, mod, g, w_up, conv_w, w_down, tm=1024, tf=256):
    B, L, D = x1.shape
    tm = min(tm, L)
    nf = D_FF // tf
    nh = tm // FFN_HALO
    last_halo = L // FFN_HALO - 1
    return pl.pallas_call(
        _ffn_kernel,
        grid=(B, L // tm, nf),
        in_specs=[
            pl.BlockSpec((1, tm, D), lambda b, i, f: (b, i, 0)),
            pl.BlockSpec((1, FFN_HALO, D), lambda b, i, f: (b, jnp.maximum(i * nh - 1, 0), 0)),
            pl.BlockSpec((1, FFN_HALO, D), lambda b, i, f: (b, jnp.minimum((i + 1) * nh, last_halo), 0)),
            pl.BlockSpec((1, tm, D), lambda b, i, f: (b, i, 0)),
            pl.BlockSpec((1, 8, D), lambda b, i, f: (b, 0, 0)),
            pl.BlockSpec(g.shape, lambda b, i, f: (0, 0)),
            pl.BlockSpec((D, tf), lambda b, i, f: (0, f)),
            pl.BlockSpec((D, tf), lambda b, i, f: (0, nf + f)),
            pl.BlockSpec((3, tf), lambda b, i, f: (0, f)),
            pl.BlockSpec((3, tf), lambda b, i, f: (0, nf + f)),
            pl.BlockSpec((tf, D), lambda b, i, f: (f, 0)),
        ],
        out_specs=pl.BlockSpec((1, tm, D), lambda b, i, f: (b, i, 0)),
        out_shape=jax.ShapeDtypeStruct((B, L, D), F32),
        scratch_shapes=[pltpu.VMEM((tm, D), F32)],
        compiler_params=pltpu.CompilerParams(
            dimension_semantics=("parallel", "parallel", "arbitrary")),
        name="conv_ffn",
    )(h2, h2, h2, x1, mod, g, w_up, w_up, conv_w, conv_w, w_down)


NA_RB = 8


def _na_kernel(q_ref, kp_ref, kc_ref, kn_ref, vp_ref, vc_ref, vn_ref, bias_ref, o_ref,
               kbuf, vbuf, *, rows):
    i = pl.program_id(1)
    blk = NA_RB * GRID_W
    win = NA_KH * GRID_W
    kbuf[0 * blk:1 * blk, :] = kp_ref[0].astype(BF16)
    kbuf[1 * blk:2 * blk, :] = kc_ref[0].astype(BF16)
    kbuf[2 * blk:3 * blk, :] = kn_ref[0].astype(BF16)
    vbuf[0 * blk:1 * blk, :] = vp_ref[0].astype(BF16)
    vbuf[1 * blk:2 * blk, :] = vc_ref[0].astype(BF16)
    vbuf[2 * blk:3 * blk, :] = vn_ref[0].astype(BF16)
    scale = HEAD_DIM ** -0.5
    for j in range(NA_RB):
        r = i * NA_RB + j
        start = jnp.clip(r - NA_KH // 2, 0, rows - NA_KH)
        loc = start - (i - 1) * NA_RB
        dr0 = start - r + (NA_KH - 1)
        off = pl.multiple_of(loc * GRID_W, GRID_W)
        kw = kbuf[pl.ds(off, win), :]
        vw = vbuf[pl.ds(off, win), :]
        qj = q_ref[0, j * GRID_W:(j + 1) * GRID_W, :].astype(BF16)
        for h in range(NA_HEADS):
            hs = slice(h * HEAD_DIM, (h + 1) * HEAD_DIM)
            s = _dot_nt(qj[:, hs], kw[:, hs]) * scale + bias_ref[h, dr0]
            m = jnp.max(s, axis=-1, keepdims=True)
            p = jnp.exp(s - m)
            l = jnp.sum(p, axis=-1, keepdims=True)
            o = _dot(p.astype(BF16), vw[:, hs]) / l
            o_ref[0, j * GRID_W:(j + 1) * GRID_W, hs] = o


def _na_bias_table(rpb):
    c = jnp.arange(GRID_W)
    dc = jnp.clip(c[None, :] - c[:, None] + (NA_KW - 1), 0, 2 * NA_KW - 2)
    col_start = jnp.clip(c - NA_KW // 2, 0, GRID_W - NA_KW)
    ok = (c[None, :] >= col_start[:, None]) & (c[None, :] < col_start[:, None] + NA_KW)
    e = jnp.where(ok[None, None], rpb.astype(F32)[:, :, dc], NEG_BIG)
    t = jnp.stack([e[:, d0:d0 + NA_KH] for d0 in range(NA_KH)], axis=1)
    return t.transpose(0, 1, 3, 2, 4).reshape(NA_HEADS, NA_KH, GRID_W, NA_KH * GRID_W)


def _neighbourhood_attention(pa, bias):
    B, L, _ = pa.shape
    rows = L // GRID_W
    blk = NA_RB * GRID_W
    nblk = rows // NA_RB
    spec = lambda col, shift: pl.BlockSpec(
        (1, blk, NA_WIDTH), lambda b, i: (b, jnp.clip(i + shift, 0, nblk - 1), col))
    return pl.pallas_call(
        functools.partial(_na_kernel, rows=rows),
        grid=(B, nblk),
        in_specs=[spec(0, 0), spec(1, -1), spec(1, 0), spec(1, 1), spec(2, -1), spec(2, 0), spec(2, 1),
                  pl.BlockSpec(bias.shape, lambda b, i: (0, 0, 0, 0))],
        out_specs=pl.BlockSpec((1, blk, NA_WIDTH), lambda b, i: (b, i, 0)),
        out_shape=jax.ShapeDtypeStruct((B, L, NA_WIDTH), F32),
        scratch_shapes=[pltpu.VMEM((3 * blk, NA_WIDTH), BF16), pltpu.VMEM((3 * blk, NA_WIDTH), BF16)],
        compiler_params=pltpu.CompilerParams(dimension_semantics=("parallel", "parallel")),
        name="neighbourhood_attention",
    )(pa, pa, pa, pa, pa, pa, pa, bias)


def _rope_table_kernel(inv_ref, cos_ref, sin_ref):
    t = cos_ref.shape[0]
    pos = (pl.program_id(0) * t + lax.broadcasted_iota(jnp.int32, (t, MLA_HEAD_PAD), 0)).astype(F32)
    ang = pos * inv_ref[...]
    cos_ref[...] = jnp.cos(ang)
    sin_ref[...] = jnp.sin(ang)


def _rope_tables(L):
    half = MLA_ROPE // 2
    inv = ROPE_THETA ** (-jnp.arange(half, dtype=F32) / half)
    inv_row = jnp.zeros((1, MLA_HEAD_PAD), F32).at[0, MLA_NOPE:MLA_NOPE + MLA_ROPE].set(jnp.tile(inv, 2))
    t = min(L, 1024)
    return pl.pallas_call(
        _rope_table_kernel,
        grid=(L // t,),
        in_specs=[pl.BlockSpec((1, MLA_HEAD_PAD), lambda i: (0, 0))],
        out_specs=[pl.BlockSpec((t, MLA_HEAD_PAD), lambda i: (i, 0))] * 2,
        out_shape=[jax.ShapeDtypeStruct((L, MLA_HEAD_PAD), F32)] * 2,
        name="rope_tables",
    )(inv_row)


def _mla_prep_kernel(pd_ref, cos_ref, sin_ref, qn_ref, kvn_ref, wq, wqr, wk, wv, we, wer,
                     q_out, k_out, v_out):
    pd = pd_ref[0]
    cos = cos_ref[...]
    sin = sin_ref[...]
    nq = _rms(pd[:, :MLA_Q_LORA], qn_ref[...]).astype(BF16)
    nkv = _rms(pd[:, MLA_Q_LORA:MLA_Q_LORA + MLA_KV_LORA], kvn_ref[...]).astype(BF16)
    kr = pd[:, MLA_Q_LORA + MLA_KV_LORA:].astype(BF16)
    q = _dot(nq, wq[...])
    q_rot = _dot(nq, wqr[...])
    k = _dot(nkv, wk[...]) + _dot(kr, we[...])
    k_rot = _dot(kr, wer[...])
    scale = (MLA_NOPE + MLA_ROPE) ** -0.5
    for h in range(MLA_HEADS):
        hs = slice(h * MLA_HEAD_PAD, (h + 1) * MLA_HEAD_PAD)
        q_out[0, :, hs] = ((q[:, hs] * cos + q_rot[:, hs] * sin) * scale).astype(BF16)
        k_out[0, :, hs] = (k[:, hs] * cos + k_rot[:, hs] * sin).astype(BF16)
    v_out[0] = _dot(nkv, wv[...]).astype(BF16)


def _mla_weights(w_uq, w_ukv):
    half = MLA_ROPE // 2
    P = jnp.zeros((MLA_ROPE, MLA_ROPE), F32)
    P = P.at[jnp.arange(half) + half, jnp.arange(half)].set(-1.0)
    P = P.at[jnp.arange(half), jnp.arange(half) + half].set(1.0)
    HP = MLA_HEAD_PAD
    wq = jnp.zeros((MLA_Q_LORA, MLA_HEADS * HP), F32)
    wqr = jnp.zeros_like(wq)
    wk = jnp.zeros((MLA_KV_LORA, MLA_HEADS * HP), F32)
    wv = jnp.zeros((MLA_KV_LORA, MLA_HEADS * MLA_V), F32)
    we = jnp.zeros((MLA_COLS_PAD - MLA_Q_LORA - MLA_KV_LORA, MLA_HEADS * HP), F32)
    wer = jnp.zeros_like(we)
    eye = jnp.eye(MLA_ROPE, dtype=F32)
    for h in range(MLA_HEADS):
        qh = w_uq[:, h * (MLA_NOPE + MLA_ROPE):(h + 1) * (MLA_NOPE + MLA_ROPE)]
        wq = wq.at[:, h * HP:h * HP + MLA_NOPE + MLA_ROPE].set(qh)
        wqr = wqr.at[:, h * HP + MLA_NOPE:h * HP + MLA_NOPE + MLA_ROPE].set(qh[:, MLA_NOPE:] @ P)
        kvh = w_ukv[:, h * (MLA_NOPE + MLA_V):(h + 1) * (MLA_NOPE + MLA_V)]
        wk = wk.at[:, h * HP:h * HP + MLA_NOPE].set(kvh[:, :MLA_NOPE])
        wv = wv.at[:, h * MLA_V:(h + 1) * MLA_V].set(kvh[:, MLA_NOPE:])
        we = we.at[:MLA_ROPE, h * HP + MLA_NOPE:h * HP + MLA_NOPE + MLA_ROPE].set(eye)
        wer = wer.at[:MLA_ROPE, h * HP + MLA_NOPE:h * HP + MLA_NOPE + MLA_ROPE].set(P)
    return tuple(w.astype(BF16) for w in (wq, wqr, wk, wv, we, wer))


def _mla_prep(pd, cos, sin, q_norm, kv_norm, weights, tm=512):
    B, L, _ = pd.shape
    tm = min(tm, L)
    full = lambda a: pl.BlockSpec(a.shape, lambda b, i: (0, 0))
    HP = MLA_HEADS * MLA_HEAD_PAD
    return pl.pallas_call(
        _mla_prep_kernel,
        grid=(B, L // tm),
        in_specs=[
            pl.BlockSpec((1, tm, MLA_COLS_PAD), lambda b, i: (b, i, 0)),
            pl.BlockSpec((tm, MLA_HEAD_PAD), lambda b, i: (i, 0)),
            pl.BlockSpec((tm, MLA_HEAD_PAD), lambda b, i: (i, 0)),
            full(q_norm), full(kv_norm),
        ] + [full(w) for w in weights],
        out_specs=[pl.BlockSpec((1, tm, HP), lambda b, i: (b, i, 0)),
                   pl.BlockSpec((1, tm, HP), lambda b, i: (b, i, 0)),
                   pl.BlockSpec((1, tm, MLA_WIDTH), lambda b, i: (b, i, 0))],
        out_shape=[jax.ShapeDtypeStruct((B, L, HP), BF16), jax.ShapeDtypeStruct((B, L, HP), BF16),
                   jax.ShapeDtypeStruct((B, L, MLA_WIDTH), BF16)],
        compiler_params=pltpu.CompilerParams(dimension_semantics=("parallel", "parallel")),
        name="mla_prep",
    )(pd, cos, sin, q_norm, kv_norm, *weights)


def _flash_kernel(q_ref, k_ref, v_ref, o_ref, m_sc, l_sc, acc_sc):
    kv = pl.program_id(3)

    @pl.when(kv == 0)
    def _():
        m_sc[...] = jnp.full_like(m_sc, -jnp.inf)
        l_sc[...] = jnp.zeros_like(l_sc)
        acc_sc[...] = jnp.zeros_like(acc_sc)

    for h in range(2):
        q = q_ref[0, :, h * MLA_HEAD_PAD:(h + 1) * MLA_HEAD_PAD]
        k = k_ref[0, :, h * MLA_HEAD_PAD:(h + 1) * MLA_HEAD_PAD]
        v = v_ref[0, :, h * MLA_V:(h + 1) * MLA_V]
        s = _dot_nt(q, k)
        m_prev = m_sc[h]
        m_new = jnp.maximum(m_prev, jnp.max(s, axis=-1, keepdims=True))
        alpha = jnp.exp(m_prev - m_new)
        p = jnp.exp(s - m_new)
        l_sc[h] = alpha * l_sc[h] + jnp.sum(p, axis=-1, keepdims=True)
        acc_sc[h] = alpha * acc_sc[h] + _dot(p.astype(BF16), v)
        m_sc[h] = m_new

    @pl.when(kv == pl.num_programs(3) - 1)
    def _():
        for h in range(2):
            o_ref[0, :, h * MLA_V:(h + 1) * MLA_V] = acc_sc[h] / l_sc[h]


def _mla_flash(q, k, v, tq=512, tk=1024):
    B, L, _ = q.shape
    tq = min(tq, L)
    tk = min(tk, L)
    return pl.pallas_call(
        _flash_kernel,
        grid=(B, MLA_HEADS // 2, L // tq, L // tk),
        in_specs=[
            pl.BlockSpec((1, tq, 2 * MLA_HEAD_PAD), lambda b, h, i, j: (b, i, h)),
            pl.BlockSpec((1, tk, 2 * MLA_HEAD_PAD), lambda b, h, i, j: (b, j, h)),
            pl.BlockSpec((1, tk, 2 * MLA_V), lambda b, h, i, j: (b, j, h)),
        ],
        out_specs=pl.BlockSpec((1, tq, 2 * MLA_V), lambda b, h, i, j: (b, i, h)),
        out_shape=jax.ShapeDtypeStruct((B, L, MLA_WIDTH), F32),
        scratch_shapes=[pltpu.VMEM((2, tq, 1), F32), pltpu.VMEM((2, tq, 1), F32),
                        pltpu.VMEM((2, tq, MLA_V), F32)],
        compiler_params=pltpu.CompilerParams(
            dimension_semantics=("parallel", "parallel", "parallel", "arbitrary")),
        name="mla_flash",
    )(q, k, v)


def _hgrn2_jax(pb, lb, norm_g):
    B, L, _ = pb.shape
    n, C, H = L // HG_CHUNK, HG_CHUNK, HG_HEADS
    d = HG_WIDTH // H
    q, f_fwd, f_bwd, i, g = jnp.split(pb.astype(F32), 5, axis=-1)
    lbb = lb.astype(F32)[:, None, None, :]
    forget = lbb + (1.0 - lbb) * jax.nn.sigmoid(jnp.stack([f_fwd, f_bwd[:, ::-1]]))
    q = jax.nn.silu(q)
    qq = jnp.stack([q, q[:, ::-1]])
    vv = jnp.stack([i, i[:, ::-1]])

    def chunks(a):
        return a.reshape(2, B, n, C, H, d).transpose(2, 0, 1, 4, 3, 5)

    qc, kc, vc, gc = chunks(qq), chunks(1.0 - forget), chunks(vv), chunks(jnp.log(forget))
    tri = jnp.tril(jnp.ones((C, C), dtype=bool))[:, :, None]

    def step(S, inp):
        qt, kt, vt, lg = inp
        b = jnp.cumsum(lg, axis=-2)
        b_end = b[..., -1:, :]
        diff = b[..., :, None, :] - b[..., None, :, :]
        decay = jnp.where(tri, jnp.exp(jnp.where(tri, diff, 0.0)), 0.0)
        scores = jnp.einsum('zbhtk,zbhsk,zbhtsk->zbhts', qt, kt, decay)
        o = jnp.einsum('zbhts,zbhsv->zbhtv', scores, vt) + jnp.einsum('zbhtk,zbhkv->zbhtv', qt * jnp.exp(b), S)
        S = jnp.exp(b_end)[..., 0, :, None] * S + jnp.einsum('zbhsk,zbhsv->zbhkv', kt * jnp.exp(b_end - b), vt)
        return S, o

    S0 = jnp.zeros((2, B, H, d, d), F32)
    _, o = lax.scan(step, S0, (qc, kc, vc, gc))
    o = o.transpose(1, 2, 0, 4, 3, 5).reshape(2, B, L, H, d)
    o = o[0] + o[1][:, ::-1]
    gh = norm_g.reshape(H, d).astype(F32)
    o = (o * lax.rsqrt(jnp.mean(o * o, axis=-1, keepdims=True) + EPS) * gh).reshape(B, L, HG_WIDTH)
    return o * jax.nn.silu(g)


def _hyena_filters_jax(L, w1, b1, freq, w2, b2, w3):
    t = jnp.linspace(0.0, 1.0, L, dtype=F32)[:, None]
    w = (2.0 * math.pi / L) * jnp.arange(L, dtype=F32)[:, None]
    bands = jnp.linspace(1e-4, HY_BANDS - 1, HY_BANDS, dtype=F32)[None, :]
    z = jnp.concatenate([t, jnp.cos(bands * w), -jnp.sin(bands * w)], axis=-1)
    a = jnp.sin(freq * (z @ w1 + b1))
    a = jnp.sin(freq * (a @ w2 + b2))
    h = (a @ w3).reshape(L, 2, HY_ORDER, HY_WIDTH)
    deltas = jnp.abs(jnp.linspace(math.log(HY_DECAY_TARGET) / HY_SLOW_PCT,
                                  math.log(HY_DECAY_TARGET) / HY_FAST_PCT, HY_WIDTH, dtype=F32))
    h = h * jnp.exp(-t * deltas)[:, None, None, :]
    h = h / jnp.sum(jnp.abs(h), axis=(0, 1), keepdims=True)
    h_fwd, h_bwd = h[:, 0], h[:, 1]
    return jnp.concatenate([h_fwd.at[0].add(h_bwd[0]), jnp.zeros_like(h_fwd[:1]), h_bwd[:0:-1]], axis=0)


def _hyena_jax(pc, short_w, w1, b1, freq, w2, b2, w3, skip):
    B, L, _ = pc.shape
    xp = jnp.pad(pc, ((0, 0), (1, 1), (0, 0)))
    u = short_w[0] * xp[:, :-2] + short_w[1] * xp[:, 1:-1] + short_w[2] * xp[:, 2:]
    v, x1, x2 = jnp.split(u, 3, axis=-1)
    hf = jnp.fft.rfft(_hyena_filters_jax(L, w1, b1, freq, w2, b2, w3), axis=0)
    z = v
    for n, gate in enumerate((x1, x2)):
        zc = jnp.fft.irfft(jnp.fft.rfft(z, n=2 * L, axis=1) * hf[None, :, n], n=2 * L, axis=1)[:, :L]
        z = gate * (zc + skip[n] * z)
    return z


def _prep_layer_weights(l, w_in, w_out, na_rpb, mla_w_uq, mla_w_ukv, ffn_w_up, ffn_w_down):
    wi = w_in[l]
    s0, s1, s2 = NA_COLS, NA_COLS + HG_COLS, NA_COLS + HG_COLS + HY_COLS
    w_ml = jnp.pad(wi[:, s2:], ((0, 0), (0, MLA_COLS_PAD - MLA_COLS)))
    return dict(
        w_na=wi[:, :s0].astype(BF16), w_hg=wi[:, s0:s1].astype(BF16), w_hy=wi[:, s1:s2].astype(BF16),
        w_ml=w_ml.astype(BF16), w_out=w_out[l].astype(BF16), na_bias=_na_bias_table(na_rpb[l]),
        mla=_mla_weights(mla_w_uq[l], mla_w_ukv[l]),
        w_up=ffn_w_up[l].astype(BF16), w_down=ffn_w_down[l].astype(BF16),
    )


def _trunk_layer(x, mod, lb, g, lw, p, cos, sin):
    pa, pb, pc, pd = _in_proj(x, mod, g, lw['w_na'], lw['w_hg'], lw['w_hy'], lw['w_ml'])
    o_na = _neighbourhood_attention(pa, lw['na_bias'])
    o_hg = _hgrn2_jax(pb, lb, p['hg_norm'])
    o_hy = _hyena_jax(pc, p['hy_short'], p['hy_w1'], p['hy_b1'], p['hy_freq'], p['hy_w2'], p['hy_b2'],
                      p['hy_w3'], p['hy_skip'])
    L = x.shape[1]
    q, k, v = _mla_prep(pd, cos[:L], sin[:L], p['mla_q_norm'], p['mla_kv_norm'], lw['mla'])
    o_ml = _mla_flash(q, k, v)
    x1, h2 = _out_proj(x, mod, g, o_na, o_hg, o_hy, o_ml, lw['w_out'])
    return _conv_ffn(h2, x1, mod, g, lw['w_up'], p['ffn_conv'], lw['w_down'])


def kernel(x_prompt, x_sample, c_prompt, c_sample, ada_w, ada_b, norm_g, w_in, w_out, na_rpb, hg_lb,
           hg_norm, hy_short, hy_w1, hy_b1, hy_freq, hy_w2, hy_b2, hy_w3, hy_skip, mla_q_norm,
           mla_kv_norm, mla_w_uq, mla_w_ukv, ffn_w_up, ffn_conv, ffn_w_down):
    Bp, Bs = x_prompt.shape[0], x_sample.shape[0]
    lb_soft = jax.nn.softmax(hg_lb.astype(F32), axis=0)
    lower_bounds = jnp.cumsum(lb_soft, axis=0) - lb_soft[:1]

    R = -(-(Bp + Bs) // 8) * 8
    cond = jnp.zeros((R, D_MODEL), F32).at[:Bp].set(c_prompt).at[Bp:Bp + Bs].set(c_sample)
    mod = _ada_modulation(cond, ada_w, ada_b).reshape(DEPTH, R, 6, D_MODEL)
    mod = jnp.pad(mod, ((0, 0), (0, 0), (0, 2), (0, 0)))

    cos, sin = _rope_tables(max(x_prompt.shape[1], x_sample.shape[1]))
    y_prompt, y_sample = x_prompt, x_sample
    for l in range(DEPTH):
        lw = _prep_layer_weights(l, w_in, w_out, na_rpb, mla_w_uq, mla_w_ukv, ffn_w_up, ffn_w_down)
        p = dict(hg_norm=hg_norm[l], hy_short=hy_short[l], hy_w1=hy_w1[l], hy_b1=hy_b1[l],
                 hy_freq=hy_freq[l], hy_w2=hy_w2[l], hy_b2=hy_b2[l], hy_w3=hy_w3[l], hy_skip=hy_skip[l],
                 mla_q_norm=mla_q_norm[l].reshape(1, -1), mla_kv_norm=mla_kv_norm[l].reshape(1, -1),
                 ffn_conv=ffn_conv[l])
        y_prompt = _trunk_layer(y_prompt, mod[l, :Bp], lower_bounds[l], norm_g[l], lw, p, cos, sin)
        y_sample = _trunk_layer(y_sample, mod[l, Bp:Bp + Bs], lower_bounds[l], norm_g[l], lw, p, cos, sin)
    return (y_prompt, y_sample)
```

```python
import functools
import math

import jax
import jax.numpy as jnp
from jax import lax
from jax.experimental import pallas as pl
from jax.experimental.pallas import tpu as pltpu

F32 = jnp.float32
BF16 = jnp.bfloat16

D_MODEL = 1024
DEPTH = 4
GRID_W = 64
HEAD_DIM = 64
NA_WIDTH = 256
NA_HEADS = 4
NA_KH = 8
NA_KW = 16
HG_WIDTH = 256
HG_HEADS = 4
HG_CHUNK = 64
HY_WIDTH = 256
HY_ORDER = 2
HY_EMB = 33
HY_BANDS = 16
HY_HIDDEN = 64
HY_DECAY_TARGET = 1e-2
HY_FAST_PCT = 0.3
HY_SLOW_PCT = 1.5
MLA_WIDTH = 256
MLA_HEADS = 4
MLA_NOPE = 64
MLA_ROPE = 32
MLA_V = 64
MLA_Q_LORA = 256
MLA_KV_LORA = 128
MLA_HEAD_PAD = 128
ROPE_THETA = 10000.0
D_FF = 2816
EPS = 1e-6
NA_COLS = 3 * NA_WIDTH
HG_COLS = 5 * HG_WIDTH
HY_COLS = 3 * HY_WIDTH
MLA_COLS = MLA_Q_LORA + MLA_KV_LORA + MLA_ROPE
MLA_COLS_PAD = 512
NEG_BIG = -1e30


def _dot(a, b):
    return jnp.dot(a, b, preferred_element_type=F32)


def _dot_nt(a, b):
    return lax.dot_general(a, b, (((1,), (1,)), ((), ())), preferred_element_type=F32)


def _rms(x, g):
    return x * lax.rsqrt(jnp.mean(x * x, axis=-1, keepdims=True) + EPS) * g


def _ada_kernel(c_ref, w_ref, b_ref, o_ref):
    c = c_ref[...]
    s = c * jax.nn.sigmoid(c)
    o_ref[0] = _dot(s.astype(BF16), w_ref[0].astype(BF16)) + b_ref[0]


def _ada_modulation(cond, ada_w, ada_b):
    R = cond.shape[0]
    tn = 1536
    return pl.pallas_call(
        _ada_kernel,
        grid=(DEPTH, 6 * D_MODEL // tn),
        in_specs=[
            pl.BlockSpec((R, D_MODEL), lambda l, n: (0, 0)),
            pl.BlockSpec((1, D_MODEL, tn), lambda l, n: (l, 0, n)),
            pl.BlockSpec((1, 1, tn), lambda l, n: (l, 0, n)),
        ],
        out_specs=pl.BlockSpec((1, R, tn), lambda l, n: (l, 0, n)),
        out_shape=jax.ShapeDtypeStruct((DEPTH, R, 6 * D_MODEL), F32),
        name="ada_modulation",
    )(cond, ada_w, ada_b.reshape(DEPTH, 1, 6 * D_MODEL))


def _in_proj_kernel(x_ref, mod_ref, g_ref, wna, whg, why, wml, ona, ohg, ohy, oml):
    x = x_ref[0]
    h = _rms(x, g_ref[0:1, :]) * (1.0 + mod_ref[0, 1:2, :]) + mod_ref[0, 0:1, :]
    hb = h.astype(BF16)
    ona[0] = _dot(hb, wna[...])
    ohg[0] = _dot(hb, whg[...])
    ohy[0] = _dot(hb, why[...])
    oml[0] = _dot(hb, wml[...])


def _in_proj(x, mod, g, w_na, w_hg, w_hy, w_ml, tm=512):
    B, L, D = x.shape
    tm = min(tm, L)
    full = lambda a: pl.BlockSpec(a.shape, lambda b, i: (0, 0))
    outs = [NA_COLS, HG_COLS, HY_COLS, MLA_COLS_PAD]
    return pl.pallas_call(
        _in_proj_kernel,
        grid=(B, L // tm),
        in_specs=[
            pl.BlockSpec((1, tm, D), lambda b, i: (b, i, 0)),
            pl.BlockSpec((1, 8, D), lambda b, i: (b, 0, 0)),
            full(g), full(w_na), full(w_hg), full(w_hy), full(w_ml),
        ],
        out_specs=[pl.BlockSpec((1, tm, n), lambda b, i: (b, i, 0)) for n in outs],
        out_shape=[jax.ShapeDtypeStruct((B, L, n), F32) for n in outs],
        compiler_params=pltpu.CompilerParams(dimension_semantics=("parallel", "parallel")),
        name="in_proj",
    )(x, mod, g, w_na, w_hg, w_hy, w_ml)


def _head_mean_sq(o, sel):
    sq = o * o
    hi = sq.astype(BF16)
    lo = (sq - hi.astype(F32)).astype(BF16)
    return (_dot(hi, sel) + _dot(lo, sel)) * (1.0 / (HG_WIDTH // HG_HEADS))


def _out_proj_kernel(x_ref, mod_ref, g_ref, na, hgf, hgb, hgg, hgn, sel, hy, ml, w_ref, x1_ref, h2_ref):
    W = NA_WIDTH
    o = hgf[0, 0] + hgb[0, 0]
    gate = hgg[0]
    hg = o * lax.rsqrt(_head_mean_sq(o, sel[...]) + EPS) * hgn[...] * (gate * jax.nn.sigmoid(gate))
    mix = _dot(na[0].astype(BF16), w_ref[0 * W:1 * W, :])
    mix += _dot(hg.astype(BF16), w_ref[1 * W:2 * W, :])
    mix += _dot(hy[0].astype(BF16), w_ref[2 * W:3 * W, :])
    mix += _dot(ml[0].astype(BF16), w_ref[3 * W:4 * W, :])
    x1 = x_ref[0] + mod_ref[0, 2:3, :] * _rms(mix, g_ref[1:2, :])
    x1_ref[0] = x1
    h2 = _rms(x1, g_ref[2:3, :]) * (1.0 + mod_ref[0, 4:5, :]) + mod_ref[0, 3:4, :]
    h2_ref[0] = h2.astype(BF16)


def _out_proj(x, mod, g, o_na, hg_dirs, pb, hg_norm, o_hy, o_ml, w_out, tm=512):
    B, L, D = x.shape
    tm = min(tm, L)
    tok = lambda n: pl.BlockSpec((1, tm, n), lambda b, i: (b, i, 0))
    sel = _hgrn2_tables()[1]
    return pl.pallas_call(
        _out_proj_kernel,
        grid=(B, L // tm),
        in_specs=[
            tok(D),
            pl.BlockSpec((1, 8, D), lambda b, i: (b, 0, 0)),
            pl.BlockSpec(g.shape, lambda b, i: (0, 0)),
            tok(NA_WIDTH),
            pl.BlockSpec((1, 1, tm, HG_WIDTH), lambda b, i: (b, 0, i, 0)),
            pl.BlockSpec((1, 1, tm, HG_WIDTH), lambda b, i: (b, 1, i, 0)),
            pl.BlockSpec((1, tm, HG_WIDTH), lambda b, i: (b, i, 4)),
            pl.BlockSpec((1, HG_WIDTH), lambda b, i: (0, 0)),
            pl.BlockSpec(sel.shape, lambda b, i: (0, 0)),
            tok(HY_WIDTH), tok(MLA_WIDTH),
            pl.BlockSpec(w_out.shape, lambda b, i: (0, 0)),
        ],
        out_specs=[tok(D), tok(D)],
        out_shape=[jax.ShapeDtypeStruct((B, L, D), F32), jax.ShapeDtypeStruct((B, L, D), BF16)],
        compiler_params=pltpu.CompilerParams(dimension_semantics=("parallel", "parallel")),
        name="out_proj",
    )(x, mod, g, o_na, hg_dirs, hg_dirs, pb, hg_norm.reshape(1, HG_WIDTH), sel, o_hy, o_ml, w_out)


FFN_HALO = 16


def _ffn_kernel(hm_ref, hp_ref, hn_ref, x_ref, mod_ref, g_ref, wa_ref, wb_ref, ca_ref, cb_ref,
                wd_ref, o_ref, acc_ref):
    i = pl.program_id(1)
    f = pl.program_id(2)
    tm = hm_ref.shape[1]
    not_first = (i > 0).astype(F32)
    not_last = (i < pl.num_programs(1) - 1).astype(F32)
    row = lax.broadcasted_iota(jnp.int32, (tm, 1), 0)
    hm = hm_ref[0]
    hp = hp_ref[0]
    hn = hn_ref[0]

    def conv_branch(w_ref, c_ref):
        u = _dot(hm, w_ref[...])
        u_prev = _dot(hp, w_ref[...])[FFN_HALO - 1:FFN_HALO, :] * not_first
        u_next = _dot(hn, w_ref[...])[0:1, :] * not_last
        um1 = jnp.where(row == 0, u_prev, pltpu.roll(u, 1, 0))
        up1 = jnp.where(row == tm - 1, u_next, pltpu.roll(u, tm - 1, 0))
        return c_ref[0:1, :] * um1 + c_ref[1:2, :] * u + c_ref[2:3, :] * up1

    a = conv_branch(wa_ref, ca_ref)
    b = conv_branch(wb_ref, cb_ref)
    act = (jax.nn.gelu(a, approximate=True) * b).astype(BF16)
    part = _dot(act, wd_ref[...])

    @pl.when(f == 0)
    def _():
        acc_ref[...] = part

    @pl.when(f > 0)
    def _():
        acc_ref[...] += part

    @pl.when(f == pl.num_programs(2) - 1)
    def _():
        o_ref[0] = x_ref[0] + mod_ref[0, 5:6, :] * _rms(acc_ref[...], g_ref[3:4, :])


def _conv_ffn(h2, x1, mod, g, w_up, conv_w, w_down, tm=1024, tf=256):
    B, L, D = x1.shape
    tm = min(tm, L)
    nf = D_FF // tf
    nh = tm // FFN_HALO
    last_halo = L // FFN_HALO - 1
    return pl.pallas_call(
        _ffn_kernel,
        grid=(B, L // tm, nf),
        in_specs=[
            pl.BlockSpec((1, tm, D), lambda b, i, f: (b, i, 0)),
            pl.BlockSpec((1, FFN_HALO, D), lambda b, i, f: (b, jnp.maximum(i * nh - 1, 0), 0)),
            pl.BlockSpec((1, FFN_HALO, D), lambda b, i, f: (b, jnp.minimum((i + 1) * nh, last_halo), 0)),
            pl.BlockSpec((1, tm, D), lambda b, i, f: (b, i, 0)),
            pl.BlockSpec((1, 8, D), lambda b, i, f: (b, 0, 0)),
            pl.BlockSpec(g.shape, lambda b, i, f: (0, 0)),
            pl.BlockSpec((D, tf), lambda b, i, f: (0, f)),
            pl.BlockSpec((D, tf), lambda b, i, f: (0, nf + f)),
            pl.BlockSpec((3, tf), lambda b, i, f: (0, f)),
            pl.BlockSpec((3, tf), lambda b, i, f: (0, nf + f)),
            pl.BlockSpec((tf, D), lambda b, i, f: (f, 0)),
        ],
        out_specs=pl.BlockSpec((1, tm, D), lambda b, i, f: (b, i, 0)),
        out_shape=jax.ShapeDtypeStruct((B, L, D), F32),
        scratch_shapes=[pltpu.VMEM((tm, D), F32)],
        compiler_params=pltpu.CompilerParams(
            dimension_semantics=("parallel", "parallel", "arbitrary")),
        name="conv_ffn",
    )(h2, h2, h2, x1, mod, g, w_up, w_up, conv_w, conv_w, w_down)


NA_RB = 8


def _na_kernel(q_ref, kp_ref, kc_ref, kn_ref, vp_ref, vc_ref, vn_ref, bias_ref, o_ref,
               kbuf, vbuf, *, rows):
    i = pl.program_id(1)
    blk = NA_RB * GRID_W
    win = NA_KH * GRID_W
    kbuf[0 * blk:1 * blk, :] = kp_ref[0].astype(BF16)
    kbuf[1 * blk:2 * blk, :] = kc_ref[0].astype(BF16)
    kbuf[2 * blk:3 * blk, :] = kn_ref[0].astype(BF16)
    vbuf[0 * blk:1 * blk, :] = vp_ref[0].astype(BF16)
    vbuf[1 * blk:2 * blk, :] = vc_ref[0].astype(BF16)
    vbuf[2 * blk:3 * blk, :] = vn_ref[0].astype(BF16)
    scale = HEAD_DIM ** -0.5
    for j in range(NA_RB):
        r = i * NA_RB + j
        start = jnp.clip(r - NA_KH // 2, 0, rows - NA_KH)
        loc = start - (i - 1) * NA_RB
        dr0 = start - r + (NA_KH - 1)
        off = pl.multiple_of(loc * GRID_W, GRID_W)
        kw = kbuf[pl.ds(off, win), :]
        vw = vbuf[pl.ds(off, win), :]
        qj = q_ref[0, j * GRID_W:(j + 1) * GRID_W, :].astype(BF16)
        for h in range(NA_HEADS):
            hs = slice(h * HEAD_DIM, (h + 1) * HEAD_DIM)
            s = _dot_nt(qj[:, hs], kw[:, hs]) * scale + bias_ref[h, dr0]
            m = jnp.max(s, axis=-1, keepdims=True)
            p = jnp.exp(s - m)
            l = jnp.sum(p, axis=-1, keepdims=True)
            o = _dot(p.astype(BF16), vw[:, hs]) / l
            o_ref[0, j * GRID_W:(j + 1) * GRID_W, hs] = o


def _na_bias_table(rpb):
    c = jnp.arange(GRID_W)
    dc = jnp.clip(c[None, :] - c[:, None] + (NA_KW - 1), 0, 2 * NA_KW - 2)
    col_start = jnp.clip(c - NA_KW // 2, 0, GRID_W - NA_KW)
    ok = (c[None, :] >= col_start[:, None]) & (c[None, :] < col_start[:, None] + NA_KW)
    e = jnp.where(ok[None, None], rpb.astype(F32)[:, :, dc], NEG_BIG)
    t = jnp.stack([e[:, d0:d0 + NA_KH] for d0 in range(NA_KH)], axis=1)
    return t.transpose(0, 1, 3, 2, 4).reshape(NA_HEADS, NA_KH, GRID_W, NA_KH * GRID_W)


def _neighbourhood_attention(pa, bias):
    B, L, _ = pa.shape
    rows = L // GRID_W
    blk = NA_RB * GRID_W
    nblk = rows // NA_RB
    spec = lambda col, shift: pl.BlockSpec(
        (1, blk, NA_WIDTH), lambda b, i: (b, jnp.clip(i + shift, 0, nblk - 1), col))
    return pl.pallas_call(
        functools.partial(_na_kernel, rows=rows),
        grid=(B, nblk),
        in_specs=[spec(0, 0), spec(1, -1), spec(1, 0), spec(1, 1), spec(2, -1), spec(2, 0), spec(2, 1),
                  pl.BlockSpec(bias.shape, lambda b, i: (0, 0, 0, 0))],
        out_specs=pl.BlockSpec((1, blk, NA_WIDTH), lambda b, i: (b, i, 0)),
        out_shape=jax.ShapeDtypeStruct((B, L, NA_WIDTH), F32),
        scratch_shapes=[pltpu.VMEM((3 * blk, NA_WIDTH), BF16), pltpu.VMEM((3 * blk, NA_WIDTH), BF16)],
        compiler_params=pltpu.CompilerParams(dimension_semantics=("parallel", "parallel")),
        name="neighbourhood_attention",
    )(pa, pa, pa, pa, pa, pa, pa, bias)


def _rope_table_kernel(inv_ref, cos_ref, sin_ref):
    t = cos_ref.shape[0]
    pos = (pl.program_id(0) * t + lax.broadcasted_iota(jnp.int32, (t, MLA_HEAD_PAD), 0)).astype(F32)
    ang = pos * inv_ref[...]
    cos_ref[...] = jnp.cos(ang)
    sin_ref[...] = jnp.sin(ang)


def _rope_tables(L):
    half = MLA_ROPE // 2
    inv = ROPE_THETA ** (-jnp.arange(half, dtype=F32) / half)
    inv_row = jnp.zeros((1, MLA_HEAD_PAD), F32).at[0, MLA_NOPE:MLA_NOPE + MLA_ROPE].set(jnp.tile(inv, 2))
    t = min(L, 1024)
    return pl.pallas_call(
        _rope_table_kernel,
        grid=(L // t,),
        in_specs=[pl.BlockSpec((1, MLA_HEAD_PAD), lambda i: (0, 0))],
        out_specs=[pl.BlockSpec((t, MLA_HEAD_PAD), lambda i: (i, 0))] * 2,
        out_shape=[jax.ShapeDtypeStruct((L, MLA_HEAD_PAD), F32)] * 2,
        name="rope_tables",
    )(inv_row)


def _mla_prep_kernel(pd_ref, cos_ref, sin_ref, qn_ref, kvn_ref, wq, wqr, wk, wv, we, wer,
                     q_out, k_out, v_out):
    pd = pd_ref[0]
    cos = cos_ref[...]
    sin = sin_ref[...]
    nq = _rms(pd[:, :MLA_Q_LORA], qn_ref[...]).astype(BF16)
    nkv = _rms(pd[:, MLA_Q_LORA:MLA_Q_LORA + MLA_KV_LORA], kvn_ref[...]).astype(BF16)
    kr = pd[:, MLA_Q_LORA + MLA_KV_LORA:].astype(BF16)
    q = _dot(nq, wq[...])
    q_rot = _dot(nq, wqr[...])
    k = _dot(nkv, wk[...]) + _dot(kr, we[...])
    k_rot = _dot(kr, wer[...])
    scale = (MLA_NOPE + MLA_ROPE) ** -0.5
    for h in range(MLA_HEADS):
        hs = slice(h * MLA_HEAD_PAD, (h + 1) * MLA_HEAD_PAD)
        q_out[0, :, hs] = ((q[:, hs] * cos + q_rot[:, hs] * sin) * scale).astype(BF16)
        k_out[0, :, hs] = (k[:, hs] * cos + k_rot[:, hs] * sin).astype(BF16)
    v_out[0] = _dot(nkv, wv[...]).astype(BF16)


def _mla_weights(w_uq, w_ukv):
    half = MLA_ROPE // 2
    P = jnp.zeros((MLA_ROPE, MLA_ROPE), F32)
    P = P.at[jnp.arange(half) + half, jnp.arange(half)].set(-1.0)
    P = P.at[jnp.arange(half), jnp.arange(half) + half].set(1.0)
    HP = MLA_HEAD_PAD
    wq = jnp.zeros((MLA_Q_LORA, MLA_HEADS * HP), F32)
    wqr = jnp.zeros_like(wq)
    wk = jnp.zeros((MLA_KV_LORA, MLA_HEADS * HP), F32)
    wv = jnp.zeros((MLA_KV_LORA, MLA_HEADS * MLA_V), F32)
    we = jnp.zeros((MLA_COLS_PAD - MLA_Q_LORA - MLA_KV_LORA, MLA_HEADS * HP), F32)
    wer = jnp.zeros_like(we)
    eye = jnp.eye(MLA_ROPE, dtype=F32)
    for h in range(MLA_HEADS):
        qh = w_uq[:, h * (MLA_NOPE + MLA_ROPE):(h + 1) * (MLA_NOPE + MLA_ROPE)]
        wq = wq.at[:, h * HP:h * HP + MLA_NOPE + MLA_ROPE].set(qh)
        wqr = wqr.at[:, h * HP + MLA_NOPE:h * HP + MLA_NOPE + MLA_ROPE].set(qh[:, MLA_NOPE:] @ P)
        kvh = w_ukv[:, h * (MLA_NOPE + MLA_V):(h + 1) * (MLA_NOPE + MLA_V)]
        wk = wk.at[:, h * HP:h * HP + MLA_NOPE].set(kvh[:, :MLA_NOPE])
        wv = wv.at[:, h * MLA_V:(h + 1) * MLA_V].set(kvh[:, MLA_NOPE:])
        we = we.at[:MLA_ROPE, h * HP + MLA_NOPE:h * HP + MLA_NOPE + MLA_ROPE].set(eye)
        wer = wer.at[:MLA_ROPE, h * HP + MLA_NOPE:h * HP + MLA_NOPE + MLA_ROPE].set(P)
    return tuple(w.astype(BF16) for w in (wq, wqr, wk, wv, we, wer))


def _mla_prep(pd, cos, sin, q_norm, kv_norm, weights, tm=512):
    B, L, _ = pd.shape
    tm = min(tm, L)
    full = lambda a: pl.BlockSpec(a.shape, lambda b, i: (0, 0))
    HP = MLA_HEADS * MLA_HEAD_PAD
    return pl.pallas_call(
        _mla_prep_kernel,
        grid=(B, L // tm),
        in_specs=[
            pl.BlockSpec((1, tm, MLA_COLS_PAD), lambda b, i: (b, i, 0)),
            pl.BlockSpec((tm, MLA_HEAD_PAD), lambda b, i: (i, 0)),
            pl.BlockSpec((tm, MLA_HEAD_PAD), lambda b, i: (i, 0)),
            full(q_norm), full(kv_norm),
        ] + [full(w) for w in weights],
        out_specs=[pl.BlockSpec((1, tm, HP), lambda b, i: (b, i, 0)),
                   pl.BlockSpec((1, tm, HP), lambda b, i: (b, i, 0)),
                   pl.BlockSpec((1, tm, MLA_WIDTH), lambda b, i: (b, i, 0))],
        out_shape=[jax.ShapeDtypeStruct((B, L, HP), BF16), jax.ShapeDtypeStruct((B, L, HP), BF16),
                   jax.ShapeDtypeStruct((B, L, MLA_WIDTH), BF16)],
        compiler_params=pltpu.CompilerParams(dimension_semantics=("parallel", "parallel")),
        name="mla_prep",
    )(pd, cos, sin, q_norm, kv_norm, *weights)


def _flash_kernel(q_ref, k_ref, v_ref, o_ref, m_sc, l_sc, acc_sc):
    kv = pl.program_id(3)

    @pl.when(kv == 0)
    def _():
        m_sc[...] = jnp.full_like(m_sc, -jnp.inf)
        l_sc[...] = jnp.zeros_like(l_sc)
        acc_sc[...] = jnp.zeros_like(acc_sc)

    for h in range(2):
        q = q_ref[0, :, h * MLA_HEAD_PAD:(h + 1) * MLA_HEAD_PAD]
        k = k_ref[0, :, h * MLA_HEAD_PAD:(h + 1) * MLA_HEAD_PAD]
        v = v_ref[0, :, h * MLA_V:(h + 1) * MLA_V]
        s = _dot_nt(q, k)
        m_prev = m_sc[h]
        m_new = jnp.maximum(m_prev, jnp.max(s, axis=-1, keepdims=True))
        alpha = jnp.exp(m_prev - m_new)
        p = jnp.exp(s - m_new)
        l_sc[h] = alpha * l_sc[h] + jnp.sum(p, axis=-1, keepdims=True)
        acc_sc[h] = alpha * acc_sc[h] + _dot(p.astype(BF16), v)
        m_sc[h] = m_new

    @pl.when(kv == pl.num_programs(3) - 1)
    def _():
        for h in range(2):
            o_ref[0, :, h * MLA_V:(h + 1) * MLA_V] = acc_sc[h] / l_sc[h]


def _mla_flash(q, k, v, tq=512, tk=1024):
    B, L, _ = q.shape
    tq = min(tq, L)
    tk = min(tk, L)
    return pl.pallas_call(
        _flash_kernel,
        grid=(B, MLA_HEADS // 2, L // tq, L // tk),
        in_specs=[
            pl.BlockSpec((1, tq, 2 * MLA_HEAD_PAD), lambda b, h, i, j: (b, i, h)),
            pl.BlockSpec((1, tk, 2 * MLA_HEAD_PAD), lambda b, h, i, j: (b, j, h)),
            pl.BlockSpec((1, tk, 2 * MLA_V), lambda b, h, i, j: (b, j, h)),
        ],
        out_specs=pl.BlockSpec((1, tq, 2 * MLA_V), lambda b, h, i, j: (b, i, h)),
        out_shape=jax.ShapeDtypeStruct((B, L, MLA_WIDTH), F32),
        scratch_shapes=[pltpu.VMEM((2, tq, 1), F32), pltpu.VMEM((2, tq, 1), F32),
                        pltpu.VMEM((2, tq, MLA_V), F32)],
        compiler_params=pltpu.CompilerParams(
            dimension_semantics=("parallel", "parallel", "parallel", "arbitrary")),
        name="mla_flash",
    )(q, k, v)


HG_T = 32
HG_TB = 512
SUB = 8


def _hgrn2_kernel(q_ref, f_ref, i_ref, lb_ref, tri_ref, sel_ref, bd_ref, o_ref, st_ref, g_ref, *, nsub):
    z = pl.program_id(1)
    T = HG_T
    ngrp = T // SUB

    @pl.when(pl.program_id(2) == 0)
    def _():
        st_ref[...] = jnp.zeros_like(st_ref)

    lb = lb_ref[0]
    sel = sel_ref[...]
    row = lax.broadcasted_iota(jnp.int32, (SUB, HG_WIDTH), 0)

    def step(c, backward):
        r0 = pl.multiple_of(c * T, T)
        qs = q_ref[0, pl.ds(r0, T), :]
        qs = qs * jax.nn.sigmoid(qs)
        fg = lb + (1.0 - lb) * jax.nn.sigmoid(f_ref[0, pl.ds(r0, T), :])
        kk = 1.0 - fg
        vv = i_ref[0, pl.ds(r0, T), :]
        b = _split_dot_rhs(tri_ref[0], jnp.log(fg))
        edge = b[0:1, :] if backward else b[T - 1:T, :]
        st = st_ref[...]
        o_inter = _dot_nt((qs * jnp.exp(b)).astype(BF16), st.astype(BF16))
        off = 0
        offs = {}
        for s in range(T):
            gs = s // SUB
            groups = range(0, gs + 1) if backward else range(gs, ngrp)
            bs = b[s:s + 1, :]
            ks = kk[s:s + 1, :]
            for gidx in groups:
                rs = slice(gidx * SUB, (gidx + 1) * SUB)
                gval = qs[rs] * jnp.exp(b[rs] - bs) * ks
                if gidx == gs:
                    keep = (row <= s - gs * SUB) if backward else (row >= s - gs * SUB)
                    gval = jnp.where(keep, gval, 0.0)
                g_ref[off:off + SUB, :] = gval
                offs[(s, gidx)] = off
                off += SUB
        gall = g_ref[...]
        ghi = gall.astype(BF16)
        glo = (gall - ghi.astype(F32)).astype(BF16)
        red = _dot(ghi, sel) + _dot(glo, sel)
        outs = []
        for gidx in range(ngrp):
            acc = o_inter[gidx * SUB:(gidx + 1) * SUB]
            for s in range(T):
                if (s, gidx) in offs:
                    o0 = offs[(s, gidx)]
                    acc = acc + red[o0:o0 + SUB] * vv[s:s + 1, :]
            outs.append(acc)
        o_ref[0, 0, pl.ds(r0, T), :] = jnp.concatenate(outs, axis=0)
        kt = (kk * jnp.exp(edge - b)).astype(BF16)
        upd = lax.dot_general(vv.astype(BF16), kt, (((0,), (0,)), ((), ())), preferred_element_type=F32)
        st_ref[...] = st * jnp.exp(edge) + upd * bd_ref[...]

    @pl.when(z == 0)
    def _():
        lax.fori_loop(0, nsub, lambda c, _: (step(c, False), 0)[1], 0)

    @pl.when(z == 1)
    def _():
        lax.fori_loop(0, nsub, lambda c, _: (step(nsub - 1 - c, True), 0)[1], 0)


def _split_dot_rhs(tri, x):
    tb = tri.astype(BF16)
    acc = None
    for _ in range(3):
        hi = x.astype(BF16)
        part = _dot(tb, hi)
        acc = part if acc is None else acc + part
        x = x - hi.astype(F32)
    return acc


def _hgrn2_tables():
    T = HG_T
    r = jnp.arange(T)
    tri = jnp.stack([(r[None, :] <= r[:, None]), (r[None, :] >= r[:, None])]).astype(F32)
    head = jnp.arange(HG_WIDTH) // (HG_WIDTH // HG_HEADS)
    same = (head[:, None] == head[None, :])
    return tri, same.astype(BF16), same.astype(F32)


def _hgrn2_groups():
    T, ngrp = HG_T, HG_T // SUB
    return sum(ngrp - s // SUB for s in range(T))


def _hgrn2_scan(pb, lb):
    B, L, _ = pb.shape
    tb = min(HG_TB, L)
    nblk = L // tb
    tri, sel, bd = _hgrn2_tables()
    W = HG_WIDTH
    blk = lambda z, j: j + z * (nblk - 1 - 2 * j)
    return pl.pallas_call(
        functools.partial(_hgrn2_kernel, nsub=tb // HG_T),
        grid=(B, 2, nblk),
        in_specs=[
            pl.BlockSpec((1, tb, W), lambda b, z, j: (b, blk(z, j), 0)),
            pl.BlockSpec((1, tb, W), lambda b, z, j: (b, blk(z, j), 1 + z)),
            pl.BlockSpec((1, tb, W), lambda b, z, j: (b, blk(z, j), 3)),
            pl.BlockSpec((1, 1, W), lambda b, z, j: (z, 0, 0)),
            pl.BlockSpec((1, HG_T, HG_T), lambda b, z, j: (z, 0, 0)),
            pl.BlockSpec((W, W), lambda b, z, j: (0, 0)),
            pl.BlockSpec((W, W), lambda b, z, j: (0, 0)),
        ],
        out_specs=pl.BlockSpec((1, 1, tb, W), lambda b, z, j: (b, z, blk(z, j), 0)),
        out_shape=jax.ShapeDtypeStruct((B, 2, L, W), F32),
        scratch_shapes=[pltpu.VMEM((W, W), F32), pltpu.VMEM((_hgrn2_groups() * SUB, W), F32)],
        compiler_params=pltpu.CompilerParams(
            dimension_semantics=("parallel", "parallel", "arbitrary")),
        name="hgrn2_scan",
    )(pb, pb, pb, lb.reshape(2, 1, W), tri, sel, bd)


HY_N2 = 128
HY_K1_TILE = 8
HY_COL_TILE = 2048
HY_FILT_ROWS = 512


def _dot_hi(a, b):
    return jnp.dot(a, b, precision=lax.Precision.HIGHEST, preferred_element_type=F32)


def _hy_short_kernel(pm_ref, pp_ref, pn_ref, w_ref, v_ref, x1_ref, x2_ref):
    i = pl.program_id(1)
    tm = pm_ref.shape[1]
    row = lax.broadcasted_iota(jnp.int32, (tm, 1), 0)
    u = pm_ref[0]
    u_prev = pp_ref[0, SUB - 1:SUB, :] * (i > 0).astype(F32)
    u_next = pn_ref[0, 0:1, :] * (i < pl.num_programs(1) - 1).astype(F32)
    um1 = jnp.where(row == 0, u_prev, pltpu.roll(u, 1, 0))
    up1 = jnp.where(row == tm - 1, u_next, pltpu.roll(u, tm - 1, 0))
    y = w_ref[0:1, :] * um1 + w_ref[1:2, :] * u + w_ref[2:3, :] * up1
    W = HY_WIDTH
    v_ref[0] = y[:, 0 * W:1 * W]
    x1_ref[0] = y[:, 1 * W:2 * W]
    x2_ref[0] = y[:, 2 * W:3 * W]


def _hy_short_conv(pc, short_w, tm=512):
    B, L, C3 = pc.shape
    tm = min(tm, L)
    nh = tm // SUB
    last = L // SUB - 1
    out = pl.BlockSpec((1, tm, HY_WIDTH), lambda b, i: (b, i, 0))
    return pl.pallas_call(
        _hy_short_kernel,
        grid=(B, L // tm),
        in_specs=[
            pl.BlockSpec((1, tm, C3), lambda b, i: (b, i, 0)),
            pl.BlockSpec((1, SUB, C3), lambda b, i: (b, jnp.maximum(i * nh - 1, 0), 0)),
            pl.BlockSpec((1, SUB, C3), lambda b, i: (b, jnp.minimum((i + 1) * nh, last), 0)),
            pl.BlockSpec((3, C3), lambda b, i: (0, 0)),
        ],
        out_specs=[out, out, out],
        out_shape=[jax.ShapeDtypeStruct((B, L, HY_WIDTH), F32)] * 3,
        compiler_params=pltpu.CompilerParams(dimension_semantics=("parallel", "parallel")),
        name="hyena_short_conv",
    )(pc, pc, pc, short_w)


def _hy_filter_kernel(w1t_ref, w1c_ref, w1s_ref, b1_ref, fr_ref, w2_ref, b2_ref, w3_ref, bands_ref,
                      dl_ref, inv_ref, k_ref, sum_ref, *, L):
    tr = k_ref.shape[0]
    r = pl.program_id(0) * tr + lax.broadcasted_iota(jnp.int32, (tr, 1), 0)
    first = r < L
    pos = jnp.where(first, r, 2 * L - r).astype(F32)
    t = pos * (1.0 / (L - 1))
    bw = (pos * (2.0 * math.pi / L)) * bands_ref[...]
    pre = t * w1t_ref[...] + _dot_hi(jnp.cos(bw), w1c_ref[...]) - _dot_hi(jnp.sin(bw), w1s_ref[...])
    fr = fr_ref[...]
    a = jnp.sin(fr * (pre + b1_ref[...]))
    a = jnp.sin(fr * (_dot_hi(a, w2_ref[...]) + b2_ref[...]))
    h = _dot_hi(a, w3_ref[...])
    decay = jnp.exp(-t * dl_ref[...])
    W2 = HY_ORDER * HY_WIDTH
    use_f = first.astype(F32)
    use_b = ((r > L) | (r == 0)).astype(F32)
    hf = h[:, :W2] * decay * use_f
    hb = h[:, W2:] * decay * use_b
    k_ref[...] = (hf + hb) * inv_ref[...]

    @pl.when(pl.program_id(0) == 0)
    def _():
        sum_ref[...] = jnp.zeros_like(sum_ref)

    sum_ref[...] += jnp.sum(jnp.abs(hf) + jnp.abs(hb), axis=0, keepdims=True)


def _hy_filter(L, w1, b1, freq, w2, b2, w3):
    tr = min(HY_FILT_ROWS, L)
    W2 = HY_ORDER * HY_WIDTH
    bands = jnp.zeros((1, 128), F32).at[0, :HY_BANDS].set(jnp.linspace(1e-4, HY_BANDS - 1, HY_BANDS, dtype=F32))
    w1c = jnp.zeros((128, HY_HIDDEN), F32).at[:HY_BANDS].set(w1[1:1 + HY_BANDS])
    w1s = jnp.zeros((128, HY_HIDDEN), F32).at[:HY_BANDS].set(w1[1 + HY_BANDS:])
    deltas = jnp.abs(jnp.linspace(math.log(HY_DECAY_TARGET) / HY_SLOW_PCT,
                                  math.log(HY_DECAY_TARGET) / HY_FAST_PCT, HY_WIDTH, dtype=F32))
    dl = jnp.tile(deltas, HY_ORDER).reshape(1, W2)
    args = (w1[0:1], w1c, w1s, b1.reshape(1, -1), freq.reshape(1, -1), w2, b2.reshape(1, -1), w3, bands, dl)
    full = lambda a: pl.BlockSpec(a.shape, lambda i: (0, 0))
    call = pl.pallas_call(
        functools.partial(_hy_filter_kernel, L=L),
        grid=(2 * L // tr,),
        in_specs=[full(a) for a in args] + [pl.BlockSpec((1, W2), lambda i: (0, 0))],
        out_specs=[pl.BlockSpec((tr, W2), lambda i: (i, 0)), pl.BlockSpec((1, W2), lambda i: (0, 0))],
        out_shape=[jax.ShapeDtypeStruct((2 * L, W2), F32), jax.ShapeDtypeStruct((1, W2), F32)],
        compiler_params=pltpu.CompilerParams(dimension_semantics=("arbitrary",)),
        name="hyena_filter",
    )
    _, total = call(*args, jnp.ones((1, W2), F32))
    k, _ = call(*args, 1.0 / total)
    return k


def _hy_dft_tables(L):
    N = 2 * L
    N2 = HY_N2
    N1 = N // N2
    def cs(n, rows, cols):
        ang = (2.0 * math.pi / n) * ((jnp.arange(rows)[:, None] * jnp.arange(cols)[None, :]) % n).astype(F32)
        return jnp.cos(ang), jnp.sin(ang)
    ca, sa = cs(N1, N1, N1)
    fa_full = jnp.concatenate([ca, -sa], axis=0)
    fa_inv = jnp.concatenate([ca[:N1 // 2], -sa[:N1 // 2]], axis=1) * (1.0 / N)
    cb, sb = cs(N2, N2, N2)
    fb = jnp.concatenate([jnp.concatenate([cb, sb], axis=1), jnp.concatenate([-sb, cb], axis=1)], axis=0)
    tc, ts = cs(N, N1, N2)
    tw = jnp.stack([tc, ts])[..., None] * jnp.ones((1, 1, 1, 128), F32)
    return dict(N1=N1, fa_half=fa_full[:, :N1 // 2].astype(BF16), fa_full=fa_full.astype(BF16),
                fa_inv=fa_inv.astype(BF16), fb=fb.astype(BF16), fb_t=fb.T.astype(BF16), tw=tw)


def _hy_left_kernel(m_ref, x_ref, o_ref):
    o_ref[0] = _dot(m_ref[...], x_ref[0].astype(BF16))


def _hy_left_matmul(m, x):
    B, R, cols = x.shape
    tc = min(HY_COL_TILE, cols)
    return pl.pallas_call(
        _hy_left_kernel,
        grid=(B, cols // tc),
        in_specs=[pl.BlockSpec(m.shape, lambda b, j: (0, 0)),
                  pl.BlockSpec((1, R, tc), lambda b, j: (b, 0, j))],
        out_specs=pl.BlockSpec((1, m.shape[0], tc), lambda b, j: (b, 0, j)),
        out_shape=jax.ShapeDtypeStruct((B, m.shape[0], cols), F32),
        compiler_params=pltpu.CompilerParams(dimension_semantics=("parallel", "parallel")),
        name="hyena_dft_a",
    )(m, x)


def _hy_gate_kernel(m_ref, y_ref, z_ref, g_ref, s_ref, o_ref):
    zc = _dot(m_ref[...], y_ref[0].astype(BF16))
    o_ref[0] = g_ref[0] * (zc + s_ref[...] * z_ref[0])


def _hy_inverse_gate(m, y, z, gate, skip_row):
    B, R, cols = y.shape
    tc = min(HY_COL_TILE, cols)
    ro = m.shape[0]
    tok = pl.BlockSpec((1, ro, tc), lambda b, j: (b, 0, j))
    return pl.pallas_call(
        _hy_gate_kernel,
        grid=(B, cols // tc),
        in_specs=[pl.BlockSpec(m.shape, lambda b, j: (0, 0)),
                  pl.BlockSpec((1, R, tc), lambda b, j: (b, 0, j)),
                  tok, tok,
                  pl.BlockSpec((1, tc), lambda b, j: (0, j))],
        out_specs=tok,
        out_shape=jax.ShapeDtypeStruct((B, ro, cols), F32),
        compiler_params=pltpu.CompilerParams(dimension_semantics=("parallel", "parallel")),
        name="hyena_idft_a_gate",
    )(m, y, z, gate, skip_row)


def _cmul(ar, ai, br, bi):
    return ar * br - ai * bi, ar * bi + ai * br


def _hy_stage_b_kernel(a_ref, tw_ref, fb_ref, fbt_ref, k_ref, o_ref, *, conv):
    N2 = HY_N2
    for j in range(a_ref.shape[2]):
        tc = jnp.concatenate([tw_ref[0, j], tw_ref[0, j]], axis=1)
        ts = jnp.concatenate([tw_ref[1, j], tw_ref[1, j]], axis=1)
        br, bi = _cmul(a_ref[0, 0, j], a_ref[0, 1, j], tc, -ts)
        x = _dot(fb_ref[...], jnp.concatenate([br, bi], axis=0).astype(BF16))
        xr, xi = x[:N2], x[N2:]
        if conv:
            yr, yi = _cmul(xr, xi, k_ref[0, j], k_ref[1, j])
            y = _dot(fbt_ref[...], jnp.concatenate([yr, yi], axis=0).astype(BF16))
            xr, xi = _cmul(y[:N2], y[N2:], tc, ts)
        o_ref[0, 0, j] = xr
        o_ref[0, 1, j] = xi


def _hy_stage_b(a, tabs, kspec=None, order=0):
    B, _, N1, N2, C = a.shape
    t1 = HY_K1_TILE
    conv = kspec is not None
    blk = pl.BlockSpec((1, 2, t1, N2, HY_WIDTH), lambda i, b, c: (b, 0, i, 0, c))
    if conv:
        k_arr = kspec
        k_spec = pl.BlockSpec((2, t1, N2, HY_WIDTH), lambda i, b, c: (0, i, 0, order))
    else:
        k_arr = jnp.zeros((2, t1, 8, 128), F32)
        k_spec = pl.BlockSpec(k_arr.shape, lambda i, b, c: (0, 0, 0, 0))
    return pl.pallas_call(
        functools.partial(_hy_stage_b_kernel, conv=conv),
        grid=(N1 // t1, B, C // HY_WIDTH),
        in_specs=[blk,
                  pl.BlockSpec((2, t1, N2, 128), lambda i, b, c: (0, i, 0, 0)),
                  pl.BlockSpec(tabs['fb'].shape, lambda i, b, c: (0, 0)),
                  pl.BlockSpec(tabs['fb_t'].shape, lambda i, b, c: (0, 0)),
                  k_spec],
        out_specs=blk,
        out_shape=jax.ShapeDtypeStruct(a.shape, F32),
        compiler_params=pltpu.CompilerParams(dimension_semantics=("parallel", "parallel", "parallel")),
        name="hyena_stage_b_conv" if conv else "hyena_stage_b_spectrum",
    )(a, tabs['tw'], tabs['fb'], tabs['fb_t'], k_arr)


def _hy_filter_spectrum(L, tabs, w1, b1, freq, w2, b2, w3):
    N1, N2 = tabs['N1'], HY_N2
    W2 = HY_ORDER * HY_WIDTH
    k = _hy_filter(L, w1, b1, freq, w2, b2, w3)
    a = _hy_left_matmul(tabs['fa_full'], k.reshape(1, N1, N2 * W2))
    spec = _hy_stage_b(a.reshape(1, 2, N1, N2, W2), tabs)
    return spec[0]


def _hyena(pc, short_w, skip, tabs, kspec):
    B, L, _ = pc.shape
    N1, N2, C = tabs['N1'], HY_N2, HY_WIDTH
    v, x1, x2 = _hy_short_conv(pc, short_w)
    flat = lambda t: t.reshape(B, N1 // 2, N2 * C)
    z = flat(v)
    for n, gate in enumerate((x1, x2)):
        a = _hy_left_matmul(tabs['fa_half'], z)
        y = _hy_stage_b(a.reshape(B, 2, N1, N2, C), tabs, kspec, order=n)
        z = _hy_inverse_gate(tabs['fa_inv'], y.reshape(B, 2 * N1, N2 * C), z, flat(gate),
                             jnp.tile(skip[n], N2).reshape(1, N2 * C))
    return z.reshape(B, L, C)


def _prep_layer_weights(l, w_in, w_out, na_rpb, mla_w_uq, mla_w_ukv, ffn_w_up, ffn_w_down):
    wi = w_in[l]
    s0, s1, s2 = NA_COLS, NA_COLS + HG_COLS, NA_COLS + HG_COLS + HY_COLS
    w_ml = jnp.pad(wi[:, s2:], ((0, 0), (0, MLA_COLS_PAD - MLA_COLS)))
    return dict(
        w_na=wi[:, :s0].astype(BF16), w_hg=wi[:, s0:s1].astype(BF16), w_hy=wi[:, s1:s2].astype(BF16),
        w_ml=w_ml.astype(BF16), w_out=w_out[l].astype(BF16), na_bias=_na_bias_table(na_rpb[l]),
        mla=_mla_weights(mla_w_uq[l], mla_w_ukv[l]),
        w_up=ffn_w_up[l].astype(BF16), w_down=ffn_w_down[l].astype(BF16),
    )


def _trunk_layer(x, mod, lb, g, lw, p, cos, sin, hy_tabs, hy_spec):
    pa, pb, pc, pd = _in_proj(x, mod, g, lw['w_na'], lw['w_hg'], lw['w_hy'], lw['w_ml'])
    o_na = _neighbourhood_attention(pa, lw['na_bias'])
    hg_dirs = _hgrn2_scan(pb, lb)
    o_hy = _hyena(pc, p['hy_short'], p['hy_skip'], hy_tabs, hy_spec)
    L = x.shape[1]
    q, k, v = _mla_prep(pd, cos[:L], sin[:L], p['mla_q_norm'], p['mla_kv_norm'], lw['mla'])
    o_ml = _mla_flash(q, k, v)
    x1, h2 = _out_proj(x, mod, g, o_na, hg_dirs, pb, p['hg_norm'], o_hy, o_ml, lw['w_out'])
    return _conv_ffn(h2, x1, mod, g, lw['w_up'], p['ffn_conv'], lw['w_down'])


def kernel(x_prompt, x_sample, c_prompt, c_sample, ada_w, ada_b, norm_g, w_in, w_out, na_rpb, hg_lb,
           hg_norm, hy_short, hy_w1, hy_b1, hy_freq, hy_w2, hy_b2, hy_w3, hy_skip, mla_q_norm,
           mla_kv_norm, mla_w_uq, mla_w_ukv, ffn_w_up, ffn_conv, ffn_w_down):
    Bp, Bs = x_prompt.shape[0], x_sample.shape[0]
    Lp, Ls = x_prompt.shape[1], x_sample.shape[1]
    lb_soft = jax.nn.softmax(hg_lb.astype(F32), axis=0)
    lower_bounds = jnp.cumsum(lb_soft, axis=0) - lb_soft[:1]

    R = -(-(Bp + Bs) // 8) * 8
    cond = jnp.zeros((R, D_MODEL), F32).at[:Bp].set(c_prompt).at[Bp:Bp + Bs].set(c_sample)
    mod = _ada_modulation(cond, ada_w, ada_b).reshape(DEPTH, R, 6, D_MODEL)
    mod = jnp.pad(mod, ((0, 0), (0, 0), (0, 2), (0, 0)))

    cos, sin = _rope_tables(max(Lp, Ls))
    tabs_p, tabs_s = _hy_dft_tables(Lp), _hy_dft_tables(Ls)
    y_prompt, y_sample = x_prompt, x_sample
    for l in range(DEPTH):
        lw = _prep_layer_weights(l, w_in, w_out, na_rpb, mla_w_uq, mla_w_ukv, ffn_w_up, ffn_w_down)
        p = dict(hg_norm=hg_norm[l], hy_short=hy_short[l], hy_skip=hy_skip[l],
                 mla_q_norm=mla_q_norm[l].reshape(1, -1), mla_kv_norm=mla_kv_norm[l].reshape(1, -1),
                 ffn_conv=ffn_conv[l])
        filt = (hy_w1[l], hy_b1[l], hy_freq[l], hy_w2[l], hy_b2[l], hy_w3[l])
        spec_p = _hy_filter_spectrum(Lp, tabs_p, *filt)
        spec_s = _hy_filter_spectrum(Ls, tabs_s, *filt)
        y_prompt = _trunk_layer(y_prompt, mod[l, :Bp], lower_bounds[l], norm_g[l], lw, p, cos, sin,
                                tabs_p, spec_p)
        y_sample = _trunk_layer(y_sample, mod[l, Bp:Bp + Bs], lower_bounds[l], norm_g[l], lw, p, cos, sin,
                                tabs_s, spec_s)
    return (y_prompt, y_sample)
```

```python
import functools
import math

import jax
import jax.numpy as jnp
from jax import lax
from jax.experimental import pallas as pl
from jax.experimental.pallas import tpu as pltpu

F32 = jnp.float32
BF16 = jnp.bfloat16

D_MODEL = 1024
DEPTH = 4
GRID_W = 64
HEAD_DIM = 64
NA_WIDTH = 256
NA_HEADS = 4
NA_KH = 8
NA_KW = 16
HG_WIDTH = 256
HG_HEADS = 4
HG_CHUNK = 64
HY_WIDTH = 256
HY_ORDER = 2
HY_EMB = 33
HY_BANDS = 16
HY_HIDDEN = 64
HY_DECAY_TARGET = 1e-2
HY_FAST_PCT = 0.3
HY_SLOW_PCT = 1.5
MLA_WIDTH = 256
MLA_HEADS = 4
MLA_NOPE = 64
MLA_ROPE = 32
MLA_V = 64
MLA_Q_LORA = 256
MLA_KV_LORA = 128
MLA_HEAD_PAD = 128
ROPE_THETA = 10000.0
D_FF = 2816
EPS = 1e-6
NA_COLS = 3 * NA_WIDTH
HG_COLS = 5 * HG_WIDTH
HY_COLS = 3 * HY_WIDTH
MLA_COLS = MLA_Q_LORA + MLA_KV_LORA + MLA_ROPE
MLA_COLS_PAD = 512
NEG_BIG = -1e30


def _dot(a, b):
    return jnp.dot(a, b, preferred_element_type=F32)


def _dot_nt(a, b):
    return lax.dot_general(a, b, (((1,), (1,)), ((), ())), preferred_element_type=F32)


def _rms(x, g):
    return x * lax.rsqrt(jnp.mean(x * x, axis=-1, keepdims=True) + EPS) * g


def _ada_kernel(c_ref, w_ref, b_ref, o_ref):
    c = c_ref[...]
    s = c * jax.nn.sigmoid(c)
    o_ref[0] = _dot(s.astype(BF16), w_ref[0].astype(BF16)) + b_ref[0]


def _ada_modulation(cond, ada_w, ada_b):
    R = cond.shape[0]
    tn = 1536
    return pl.pallas_call(
        _ada_kernel,
        grid=(DEPTH, 6 * D_MODEL // tn),
        in_specs=[
            pl.BlockSpec((R, D_MODEL), lambda l, n: (0, 0)),
            pl.BlockSpec((1, D_MODEL, tn), lambda l, n: (l, 0, n)),
            pl.BlockSpec((1, 1, tn), lambda l, n: (l, 0, n)),
        ],
        out_specs=pl.BlockSpec((1, R, tn), lambda l, n: (l, 0, n)),
        out_shape=jax.ShapeDtypeStruct((DEPTH, R, 6 * D_MODEL), F32),
        name="ada_modulation",
    )(cond, ada_w, ada_b.reshape(DEPTH, 1, 6 * D_MODEL))


def _in_proj_kernel(x_ref, mod_ref, g_ref, wna, whg, why, wml, ona, ohg, ohy, oml):
    x = x_ref[0]
    h = _rms(x, g_ref[0:1, :]) * (1.0 + mod_ref[0, 1:2, :]) + mod_ref[0, 0:1, :]
    hb = h.astype(BF16)
    ona[0] = _dot(hb, wna[...])
    ohg[0] = _dot(hb, whg[...])
    ohy[0] = _dot(hb, why[...])
    oml[0] = _dot(hb, wml[...])


def _in_proj(x, mod, g, w_na, w_hg, w_hy, w_ml, tm=512):
    B, L, D = x.shape
    tm = min(tm, L)
    full = lambda a: pl.BlockSpec(a.shape, lambda b, i: (0, 0))
    outs = [NA_COLS, HG_COLS, HY_COLS, MLA_COLS_PAD]
    return pl.pallas_call(
        _in_proj_kernel,
        grid=(B, L // tm),
        in_specs=[
            pl.BlockSpec((1, tm, D), lambda b, i: (b, i, 0)),
            pl.BlockSpec((1, 8, D), lambda b, i: (b, 0, 0)),
            full(g), full(w_na), full(w_hg), full(w_hy), full(w_ml),
        ],
        out_specs=[pl.BlockSpec((1, tm, n), lambda b, i: (b, i, 0)) for n in outs],
        out_shape=[jax.ShapeDtypeStruct((B, L, n), F32) for n in outs],
        compiler_params=pltpu.CompilerParams(dimension_semantics=("parallel", "parallel")),
        name="in_proj",
    )(x, mod, g, w_na, w_hg, w_hy, w_ml)


def _head_mean_sq(o, sel):
    sq = o * o
    hi = sq.astype(BF16)
    lo = (sq - hi.astype(F32)).astype(BF16)
    return (_dot(hi, sel) + _dot(lo, sel)) * (1.0 / (HG_WIDTH // HG_HEADS))


def _out_proj_kernel(x_ref, mod_ref, g_ref, na, hgf, hgb, hgg, hgn, sel, hy, ml, w_ref, x1_ref, h2_ref):
    W = NA_WIDTH
    o = hgf[0, 0] + hgb[0, 0]
    gate = hgg[0]
    hg = o * lax.rsqrt(_head_mean_sq(o, sel[...]) + EPS) * hgn[...] * (gate * jax.nn.sigmoid(gate))
    mix = _dot(na[0].astype(BF16), w_ref[0 * W:1 * W, :])
    mix += _dot(hg.astype(BF16), w_ref[1 * W:2 * W, :])
    mix += _dot(hy[0].astype(BF16), w_ref[2 * W:3 * W, :])
    mix += _dot(ml[0].astype(BF16), w_ref[3 * W:4 * W, :])
    x1 = x_ref[0] + mod_ref[0, 2:3, :] * _rms(mix, g_ref[1:2, :])
    x1_ref[0] = x1
    h2 = _rms(x1, g_ref[2:3, :]) * (1.0 + mod_ref[0, 4:5, :]) + mod_ref[0, 3:4, :]
    h2_ref[0] = h2.astype(BF16)


def _out_proj(x, mod, g, o_na, hg_dirs, pb, hg_norm, o_hy, o_ml, w_out, tm=512):
    B, L, D = x.shape
    tm = min(tm, L)
    tok = lambda n: pl.BlockSpec((1, tm, n), lambda b, i: (b, i, 0))
    sel = _hgrn2_tables()[1]
    return pl.pallas_call(
        _out_proj_kernel,
        grid=(B, L // tm),
        in_specs=[
            tok(D),
            pl.BlockSpec((1, 8, D), lambda b, i: (b, 0, 0)),
            pl.BlockSpec(g.shape, lambda b, i: (0, 0)),
            tok(NA_WIDTH),
            pl.BlockSpec((1, 1, tm, HG_WIDTH), lambda b, i: (b, 0, i, 0)),
            pl.BlockSpec((1, 1, tm, HG_WIDTH), lambda b, i: (b, 1, i, 0)),
            pl.BlockSpec((1, tm, HG_WIDTH), lambda b, i: (b, i, 4)),
            pl.BlockSpec((1, HG_WIDTH), lambda b, i: (0, 0)),
            pl.BlockSpec(sel.shape, lambda b, i: (0, 0)),
            tok(HY_WIDTH), tok(MLA_WIDTH),
            pl.BlockSpec(w_out.shape, lambda b, i: (0, 0)),
        ],
        out_specs=[tok(D), tok(D)],
        out_shape=[jax.ShapeDtypeStruct((B, L, D), F32), jax.ShapeDtypeStruct((B, L, D), BF16)],
        compiler_params=pltpu.CompilerParams(dimension_semantics=("parallel", "parallel")),
        name="out_proj",
    )(x, mod, g, o_na, hg_dirs, hg_dirs, pb, hg_norm.reshape(1, HG_WIDTH), sel, o_hy, o_ml, w_out)


FFN_HALO = 16


def _ffn_kernel(hm_ref, hp_ref, hn_ref, x_ref, mod_ref, g_ref, wa_ref, wb_ref, ca_ref, cb_ref,
                wd_ref, o_ref, acc_ref):
    i = pl.program_id(1)
    f = pl.program_id(2)
    tm = hm_ref.shape[1]
    not_first = (i > 0).astype(F32)
    not_last = (i < pl.num_programs(1) - 1).astype(F32)
    row = lax.broadcasted_iota(jnp.int32, (tm, 1), 0)
    hm = hm_ref[0]
    hp = hp_ref[0]
    hn = hn_ref[0]

    def conv_branch(w_ref, c_ref):
        u = _dot(hm, w_ref[...])
        u_prev = _dot(hp, w_ref[...])[FFN_HALO - 1:FFN_HALO, :] * not_first
        u_next = _dot(hn, w_ref[...])[0:1, :] * not_last
        um1 = jnp.where(row == 0, u_prev, pltpu.roll(u, 1, 0))
        up1 = jnp.where(row == tm - 1, u_next, pltpu.roll(u, tm - 1, 0))
        return c_ref[0:1, :] * um1 + c_ref[1:2, :] * u + c_ref[2:3, :] * up1

    a = conv_branch(wa_ref, ca_ref)
    b = conv_branch(wb_ref, cb_ref)
    act = (jax.nn.gelu(a, approximate=True) * b).astype(BF16)
    part = _dot(act, wd_ref[...])

    @pl.when(f == 0)
    def _():
        acc_ref[...] = part

    @pl.when(f > 0)
    def _():
        acc_ref[...] += part

    @pl.when(f == pl.num_programs(2) - 1)
    def _():
        o_ref[0] = x_ref[0] + mod_ref[0, 5:6, :] * _rms(acc_ref[...], g_ref[3:4, :])


def _conv_ffn(h2, x1, mod, g, w_up, conv_w, w_down, tm=1024, tf=256):
    B, L, D = x1.shape
    tm = min(tm, L)
    nf = D_FF // tf
    nh = tm // FFN_HALO
    last_halo = L // FFN_HALO - 1
    return pl.pallas_call(
        _ffn_kernel,
        grid=(B, L // tm, nf),
        in_specs=[
            pl.BlockSpec((1, tm, D), lambda b, i, f: (b, i, 0)),
            pl.BlockSpec((1, FFN_HALO, D), lambda b, i, f: (b, jnp.maximum(i * nh - 1, 0), 0)),
            pl.BlockSpec((1, FFN_HALO, D), lambda b, i, f: (b, jnp.minimum((i + 1) * nh, last_halo), 0)),
            pl.BlockSpec((1, tm, D), lambda b, i, f: (b, i, 0)),
            pl.BlockSpec((1, 8, D), lambda b, i, f: (b, 0, 0)),
            pl.BlockSpec(g.shape, lambda b, i, f: (0, 0)),
            pl.BlockSpec((D, tf), lambda b, i, f: (0, f)),
            pl.BlockSpec((D, tf), lambda b, i, f: (0, nf + f)),
            pl.BlockSpec((3, tf), lambda b, i, f: (0, f)),
            pl.BlockSpec((3, tf), lambda b, i, f: (0, nf + f)),
            pl.BlockSpec((tf, D), lambda b, i, f: (f, 0)),
        ],
        out_specs=pl.BlockSpec((1, tm, D), lambda b, i, f: (b, i, 0)),
        out_shape=jax.ShapeDtypeStruct((B, L, D), F32),
        scratch_shapes=[pltpu.VMEM((tm, D), F32)],
        compiler_params=pltpu.CompilerParams(
            dimension_semantics=("parallel", "parallel", "arbitrary")),
        name="conv_ffn",
    )(h2, h2, h2, x1, mod, g, w_up, w_up, conv_w, conv_w, w_down)


NA_RB = 8


def _na_kernel(q_ref, kp_ref, kc_ref, kn_ref, vp_ref, vc_ref, vn_ref, bias_ref, o_ref,
               kbuf, vbuf, *, rows):
    i = pl.program_id(1)
    blk = NA_RB * GRID_W
    win = NA_KH * GRID_W
    kbuf[0 * blk:1 * blk, :] = kp_ref[0].astype(BF16)
    kbuf[1 * blk:2 * blk, :] = kc_ref[0].astype(BF16)
    kbuf[2 * blk:3 * blk, :] = kn_ref[0].astype(BF16)
    vbuf[0 * blk:1 * blk, :] = vp_ref[0].astype(BF16)
    vbuf[1 * blk:2 * blk, :] = vc_ref[0].astype(BF16)
    vbuf[2 * blk:3 * blk, :] = vn_ref[0].astype(BF16)
    scale = HEAD_DIM ** -0.5 * math.log2(math.e)
    lane_head = lax.broadcasted_iota(jnp.int32, (1, NA_WIDTH), 1) // HEAD_DIM
    for j in range(NA_RB):
        r = i * NA_RB + j
        start = jnp.clip(r - NA_KH // 2, 0, rows - NA_KH)
        loc = start - (i - 1) * NA_RB
        dr0 = start - r + (NA_KH - 1)
        off = pl.multiple_of(loc * GRID_W, GRID_W)
        kw = kbuf[pl.ds(off, win), :]
        vw = vbuf[pl.ds(off, win), :]
        qj = (q_ref[0, j * GRID_W:(j + 1) * GRID_W, :] * scale).astype(BF16)
        qbd = jnp.concatenate([jnp.where(lane_head == h, qj, jnp.zeros_like(qj)) for h in range(NA_HEADS)],
                              axis=0)
        s = _dot_nt(qbd, kw) + bias_ref[dr0]
        m = jnp.max(s, axis=-1, keepdims=True)
        p = jnp.exp2(s - m)
        l = jnp.sum(p, axis=-1, keepdims=True)
        pv = _dot(p.astype(BF16), vw) / l
        o = None
        for h in range(NA_HEADS):
            part = jnp.where(lane_head == h, pv[h * GRID_W:(h + 1) * GRID_W], 0.0)
            o = part if o is None else o + part
        o_ref[0, j * GRID_W:(j + 1) * GRID_W, :] = o


def _na_bias_table(rpb):
    c = jnp.arange(GRID_W)
    dc = jnp.clip(c[None, :] - c[:, None] + (NA_KW - 1), 0, 2 * NA_KW - 2)
    col_start = jnp.clip(c - NA_KW // 2, 0, GRID_W - NA_KW)
    ok = (c[None, :] >= col_start[:, None]) & (c[None, :] < col_start[:, None] + NA_KW)
    e = jnp.where(ok[None, None], rpb.astype(F32)[:, :, dc], NEG_BIG)
    t = jnp.stack([e[:, d0:d0 + NA_KH] for d0 in range(NA_KH)], axis=1)
    t = t.transpose(1, 0, 3, 2, 4).reshape(NA_KH, NA_HEADS * GRID_W, NA_KH * GRID_W)
    return jnp.where(t > 0.5 * NEG_BIG, t * math.log2(math.e), NEG_BIG)


def _neighbourhood_attention(pa, bias):
    B, L, _ = pa.shape
    rows = L // GRID_W
    blk = NA_RB * GRID_W
    nblk = rows // NA_RB
    spec = lambda col, shift: pl.BlockSpec(
        (1, blk, NA_WIDTH), lambda b, i: (b, jnp.clip(i + shift, 0, nblk - 1), col))
    return pl.pallas_call(
        functools.partial(_na_kernel, rows=rows),
        grid=(B, nblk),
        in_specs=[spec(0, 0), spec(1, -1), spec(1, 0), spec(1, 1), spec(2, -1), spec(2, 0), spec(2, 1),
                  pl.BlockSpec(bias.shape, lambda b, i: (0, 0, 0))],
        out_specs=pl.BlockSpec((1, blk, NA_WIDTH), lambda b, i: (b, i, 0)),
        out_shape=jax.ShapeDtypeStruct((B, L, NA_WIDTH), F32),
        scratch_shapes=[pltpu.VMEM((3 * blk, NA_WIDTH), BF16), pltpu.VMEM((3 * blk, NA_WIDTH), BF16)],
        compiler_params=pltpu.CompilerParams(dimension_semantics=("parallel", "parallel")),
        name="neighbourhood_attention",
    )(pa, pa, pa, pa, pa, pa, pa, bias)


def _rope_table_kernel(inv_ref, cos_ref, sin_ref):
    t = cos_ref.shape[0]
    pos = (pl.program_id(0) * t + lax.broadcasted_iota(jnp.int32, (t, MLA_HEAD_PAD), 0)).astype(F32)
    ang = pos * inv_ref[...]
    cos_ref[...] = jnp.cos(ang)
    sin_ref[...] = jnp.sin(ang)


def _rope_tables(L):
    half = MLA_ROPE // 2
    inv = ROPE_THETA ** (-jnp.arange(half, dtype=F32) / half)
    inv_row = jnp.zeros((1, MLA_HEAD_PAD), F32).at[0, MLA_NOPE:MLA_NOPE + MLA_ROPE].set(jnp.tile(inv, 2))
    t = min(L, 1024)
    return pl.pallas_call(
        _rope_table_kernel,
        grid=(L // t,),
        in_specs=[pl.BlockSpec((1, MLA_HEAD_PAD), lambda i: (0, 0))],
        out_specs=[pl.BlockSpec((t, MLA_HEAD_PAD), lambda i: (i, 0))] * 2,
        out_shape=[jax.ShapeDtypeStruct((L, MLA_HEAD_PAD), F32)] * 2,
        name="rope_tables",
    )(inv_row)


def _mla_prep_kernel(pd_ref, cos_ref, sin_ref, qn_ref, kvn_ref, wq, wqr, wk, wv, we, wer,
                     q_out, k_out, v_out):
    pd = pd_ref[0]
    cos = cos_ref[...]
    sin = sin_ref[...]
    nq = _rms(pd[:, :MLA_Q_LORA], qn_ref[...]).astype(BF16)
    nkv = _rms(pd[:, MLA_Q_LORA:MLA_Q_LORA + MLA_KV_LORA], kvn_ref[...]).astype(BF16)
    kr = pd[:, MLA_Q_LORA + MLA_KV_LORA:].astype(BF16)
    q = _dot(nq, wq[...])
    q_rot = _dot(nq, wqr[...])
    k = _dot(nkv, wk[...]) + _dot(kr, we[...])
    k_rot = _dot(kr, wer[...])
    scale = (MLA_NOPE + MLA_ROPE) ** -0.5 * math.log2(math.e)
    for h in range(MLA_HEADS):
        hs = slice(h * MLA_HEAD_PAD, (h + 1) * MLA_HEAD_PAD)
        q_out[0, :, hs] = ((q[:, hs] * cos + q_rot[:, hs] * sin) * scale).astype(BF16)
        k_out[0, :, hs] = (k[:, hs] * cos + k_rot[:, hs] * sin).astype(BF16)
    v_out[0] = _dot(nkv, wv[...]).astype(BF16)


def _mla_weights(w_uq, w_ukv):
    half = MLA_ROPE // 2
    P = jnp.zeros((MLA_ROPE, MLA_ROPE), F32)
    P = P.at[jnp.arange(half) + half, jnp.arange(half)].set(-1.0)
    P = P.at[jnp.arange(half), jnp.arange(half) + half].set(1.0)
    HP = MLA_HEAD_PAD
    wq = jnp.zeros((MLA_Q_LORA, MLA_HEADS * HP), F32)
    wqr = jnp.zeros_like(wq)
    wk = jnp.zeros((MLA_KV_LORA, MLA_HEADS * HP), F32)
    wv = jnp.zeros((MLA_KV_LORA, MLA_HEADS * MLA_V), F32)
    we = jnp.zeros((MLA_COLS_PAD - MLA_Q_LORA - MLA_KV_LORA, MLA_HEADS * HP), F32)
    wer = jnp.zeros_like(we)
    eye = jnp.eye(MLA_ROPE, dtype=F32)
    for h in range(MLA_HEADS):
        qh = w_uq[:, h * (MLA_NOPE + MLA_ROPE):(h + 1) * (MLA_NOPE + MLA_ROPE)]
        wq = wq.at[:, h * HP:h * HP + MLA_NOPE + MLA_ROPE].set(qh)
        wqr = wqr.at[:, h * HP + MLA_NOPE:h * HP + MLA_NOPE + MLA_ROPE].set(qh[:, MLA_NOPE:] @ P)
        kvh = w_ukv[:, h * (MLA_NOPE + MLA_V):(h + 1) * (MLA_NOPE + MLA_V)]
        wk = wk.at[:, h * HP:h * HP + MLA_NOPE].set(kvh[:, :MLA_NOPE])
        wv = wv.at[:, h * MLA_V:(h + 1) * MLA_V].set(kvh[:, MLA_NOPE:])
        we = we.at[:MLA_ROPE, h * HP + MLA_NOPE:h * HP + MLA_NOPE + MLA_ROPE].set(eye)
        wer = wer.at[:MLA_ROPE, h * HP + MLA_NOPE:h * HP + MLA_NOPE + MLA_ROPE].set(P)
    return tuple(w.astype(BF16) for w in (wq, wqr, wk, wv, we, wer))


def _mla_prep(pd, cos, sin, q_norm, kv_norm, weights, tm=512):
    B, L, _ = pd.shape
    tm = min(tm, L)
    full = lambda a: pl.BlockSpec(a.shape, lambda b, i: (0, 0))
    HP = MLA_HEADS * MLA_HEAD_PAD
    return pl.pallas_call(
        _mla_prep_kernel,
        grid=(B, L // tm),
        in_specs=[
            pl.BlockSpec((1, tm, MLA_COLS_PAD), lambda b, i: (b, i, 0)),
            pl.BlockSpec((tm, MLA_HEAD_PAD), lambda b, i: (i, 0)),
            pl.BlockSpec((tm, MLA_HEAD_PAD), lambda b, i: (i, 0)),
            full(q_norm), full(kv_norm),
        ] + [full(w) for w in weights],
        out_specs=[pl.BlockSpec((1, tm, HP), lambda b, i: (b, i, 0)),
                   pl.BlockSpec((1, tm, HP), lambda b, i: (b, i, 0)),
                   pl.BlockSpec((1, tm, MLA_WIDTH), lambda b, i: (b, i, 0))],
        out_shape=[jax.ShapeDtypeStruct((B, L, HP), BF16), jax.ShapeDtypeStruct((B, L, HP), BF16),
                   jax.ShapeDtypeStruct((B, L, MLA_WIDTH), BF16)],
        compiler_params=pltpu.CompilerParams(dimension_semantics=("parallel", "parallel")),
        name="mla_prep",
    )(pd, cos, sin, q_norm, kv_norm, *weights)


LANES = 128
FLASH_ROWS = 16
FLASH_HEADS = 4


def _flash_kernel(q_ref, k_ref, v_ref, o_ref, m_sc, l_sc, a_sc, acc_sc, p_sc):
    kv = pl.program_id(3)
    _, tq, tk = p_sc.shape
    RB = FLASH_ROWS

    @pl.when(kv == 0)
    def _():
        m_sc[...] = jnp.full_like(m_sc, -jnp.inf)
        l_sc[...] = jnp.zeros_like(l_sc)
        acc_sc[...] = jnp.zeros_like(acc_sc)

    first_head = lax.broadcasted_iota(jnp.int32, (1, LANES), 1) < MLA_V
    for h in range(FLASH_HEADS):
        q = q_ref[0, :, h * MLA_HEAD_PAD:(h + 1) * MLA_HEAD_PAD]
        k = k_ref[0, :, h * MLA_HEAD_PAD:(h + 1) * MLA_HEAD_PAD]
        s = _dot_nt(q, k)
        for r in range(tq // RB):
            rows = slice(r * RB, (r + 1) * RB)
            x = s[rows, 0:LANES]
            for c in range(1, tk // LANES):
                x = jnp.maximum(x, s[rows, c * LANES:(c + 1) * LANES])
            m_prev = m_sc[h, rows, :]
            m_new = jnp.maximum(m_prev, jnp.max(x, axis=-1, keepdims=True))
            a_sc[h, rows, :] = jnp.exp2(m_prev - m_new)
            m_sc[h, rows, :] = m_new
        for r in range(tq // RB):
            rows = slice(r * RB, (r + 1) * RB)
            m_new = m_sc[h, rows, :]
            part = None
            for c in range(tk // LANES):
                cols = slice(c * LANES, (c + 1) * LANES)
                p = jnp.exp2(s[rows, cols] - m_new)
                part = p if part is None else part + p
                p_sc[h, rows, cols] = p.astype(BF16)
            l_sc[h, rows, :] = a_sc[h, rows, :] * l_sc[h, rows, :] + part
    for g in range(FLASH_HEADS // 2):
        lanes = slice(g * LANES, (g + 1) * LANES)
        v = v_ref[0, :, lanes]
        pv0 = _dot(p_sc[2 * g], jnp.where(first_head, v, jnp.zeros_like(v)))
        pv1 = _dot(p_sc[2 * g + 1], jnp.where(first_head, jnp.zeros_like(v), v))
        alpha = jnp.where(first_head, a_sc[2 * g], a_sc[2 * g + 1])
        acc_sc[:, lanes] = alpha * acc_sc[:, lanes] + pv0 + pv1

    @pl.when(kv == pl.num_programs(3) - 1)
    def _():
        for g in range(FLASH_HEADS // 2):
            lanes = slice(g * LANES, (g + 1) * LANES)
            l0 = jnp.sum(l_sc[2 * g], axis=-1, keepdims=True)
            l1 = jnp.sum(l_sc[2 * g + 1], axis=-1, keepdims=True)
            o_ref[0, :, lanes] = acc_sc[:, lanes] / jnp.where(first_head, l0, l1)


def _mla_flash(q, k, v, tq=512, tk=2048):
    B, L, _ = q.shape
    tq = min(tq, L)
    tk = min(tk, L)
    nh = FLASH_HEADS
    stat = pltpu.VMEM((nh, tq, LANES), F32)
    return pl.pallas_call(
        _flash_kernel,
        grid=(B, MLA_HEADS // nh, L // tq, L // tk),
        in_specs=[
            pl.BlockSpec((1, tq, nh * MLA_HEAD_PAD), lambda b, h, i, j: (b, i, h)),
            pl.BlockSpec((1, tk, nh * MLA_HEAD_PAD), lambda b, h, i, j: (b, j, h)),
            pl.BlockSpec((1, tk, nh * MLA_V), lambda b, h, i, j: (b, j, h)),
        ],
        out_specs=pl.BlockSpec((1, tq, nh * MLA_V), lambda b, h, i, j: (b, i, h)),
        out_shape=jax.ShapeDtypeStruct((B, L, MLA_WIDTH), F32),
        scratch_shapes=[stat, stat, stat, pltpu.VMEM((tq, nh * MLA_V), F32),
                        pltpu.VMEM((nh, tq, tk), BF16)],
        compiler_params=pltpu.CompilerParams(
            dimension_semantics=("parallel", "parallel", "parallel", "arbitrary")),
        name="mla_flash",
    )(q, k, v)


HG_T = 32
HG_TB = 512
SUB = 8


def _hgrn2_kernel(q_ref, f_ref, i_ref, lb_ref, tri_ref, sel_ref, bd_ref, o_ref, st_ref, g_ref, *, nsub):
    z = pl.program_id(1)
    T = HG_T
    ngrp = T // SUB

    @pl.when(pl.program_id(2) == 0)
    def _():
        st_ref[...] = jnp.zeros_like(st_ref)

    lb = lb_ref[0]
    sel = sel_ref[...]
    row = lax.broadcasted_iota(jnp.int32, (SUB, HG_WIDTH), 0)

    def step(c, backward):
        r0 = pl.multiple_of(c * T, T)
        qs = q_ref[0, pl.ds(r0, T), :]
        qs = qs * jax.nn.sigmoid(qs)
        fg = lb + (1.0 - lb) * jax.nn.sigmoid(f_ref[0, pl.ds(r0, T), :])
        kk = 1.0 - fg
        vv = i_ref[0, pl.ds(r0, T), :]
        b = _split_dot_rhs(tri_ref[0], jnp.log(fg) * math.log2(math.e))
        edge = b[0:1, :] if backward else b[T - 1:T, :]
        st = st_ref[...]
        o_inter = _dot_nt((qs * jnp.exp2(b)).astype(BF16), st.astype(BF16))
        off = 0
        offs = {}
        for s in range(T):
            gs = s // SUB
            groups = range(0, gs + 1) if backward else range(gs, ngrp)
            bs = b[s:s + 1, :]
            ks = kk[s:s + 1, :]
            for gidx in groups:
                rs = slice(gidx * SUB, (gidx + 1) * SUB)
                gval = qs[rs] * jnp.exp2(b[rs] - bs) * ks
                if gidx == gs:
                    keep = (row <= s - gs * SUB) if backward else (row >= s - gs * SUB)
                    gval = jnp.where(keep, gval, 0.0)
                g_ref[off:off + SUB, :] = gval
                offs[(s, gidx)] = off
                off += SUB
        red = _dot(g_ref[...].astype(BF16), sel)
        outs = []
        for gidx in range(ngrp):
            acc = o_inter[gidx * SUB:(gidx + 1) * SUB]
            for s in range(T):
                if (s, gidx) in offs:
                    o0 = offs[(s, gidx)]
                    acc = acc + red[o0:o0 + SUB] * vv[s:s + 1, :]
            outs.append(acc)
        o_ref[0, 0, pl.ds(r0, T), :] = jnp.concatenate(outs, axis=0)
        kt = (kk * jnp.exp2(edge - b)).astype(BF16)
        upd = lax.dot_general(vv.astype(BF16), kt, (((0,), (0,)), ((), ())), preferred_element_type=F32)
        st_ref[...] = st * jnp.exp2(edge) + upd * bd_ref[...]

    @pl.when(z == 0)
    def _():
        lax.fori_loop(0, nsub, lambda c, _: (step(c, False), 0)[1], 0, unroll=4)

    @pl.when(z == 1)
    def _():
        lax.fori_loop(0, nsub, lambda c, _: (step(nsub - 1 - c, True), 0)[1], 0, unroll=4)


def _split_dot_rhs(tri, x):
    tb = tri.astype(BF16)
    acc = None
    for _ in range(3):
        hi = x.astype(BF16)
        part = _dot(tb, hi)
        acc = part if acc is None else acc + part
        x = x - hi.astype(F32)
    return acc


def _hgrn2_tables():
    T = HG_T
    r = jnp.arange(T)
    tri = jnp.stack([(r[None, :] <= r[:, None]), (r[None, :] >= r[:, None])]).astype(F32)
    head = jnp.arange(HG_WIDTH) // (HG_WIDTH // HG_HEADS)
    same = (head[:, None] == head[None, :])
    return tri, same.astype(BF16), same.astype(F32)


def _hgrn2_groups():
    T, ngrp = HG_T, HG_T // SUB
    return sum(ngrp - s // SUB for s in range(T))


def _hgrn2_scan(pb, lb):
    B, L, _ = pb.shape
    tb = min(HG_TB, L)
    nblk = L // tb
    tri, sel, bd = _hgrn2_tables()
    W = HG_WIDTH
    blk = lambda z, j: j + z * (nblk - 1 - 2 * j)
    return pl.pallas_call(
        functools.partial(_hgrn2_kernel, nsub=tb // HG_T),
        grid=(B, 2, nblk),
        in_specs=[
            pl.BlockSpec((1, tb, W), lambda b, z, j: (b, blk(z, j), 0)),
            pl.BlockSpec((1, tb, W), lambda b, z, j: (b, blk(z, j), 1 + z)),
            pl.BlockSpec((1, tb, W), lambda b, z, j: (b, blk(z, j), 3)),
            pl.BlockSpec((1, 1, W), lambda b, z, j: (z, 0, 0)),
            pl.BlockSpec((1, HG_T, HG_T), lambda b, z, j: (z, 0, 0)),
            pl.BlockSpec((W, W), lambda b, z, j: (0, 0)),
            pl.BlockSpec((W, W), lambda b, z, j: (0, 0)),
        ],
        out_specs=pl.BlockSpec((1, 1, tb, W), lambda b, z, j: (b, z, blk(z, j), 0)),
        out_shape=jax.ShapeDtypeStruct((B, 2, L, W), F32),
        scratch_shapes=[pltpu.VMEM((W, W), F32), pltpu.VMEM((_hgrn2_groups() * SUB, W), F32)],
        compiler_params=pltpu.CompilerParams(
            dimension_semantics=("parallel", "parallel", "arbitrary")),
        name="hgrn2_scan",
    )(pb, pb, pb, lb.reshape(2, 1, W), tri, sel, bd)


HY_N2 = 128
HY_K1_TILE = 8
HY_FILT_ROWS = 512


def _dot_hi(a, b):
    return jnp.dot(a, b, precision=lax.Precision.HIGHEST, preferred_element_type=F32)


def _hy_short_kernel(pm_ref, pp_ref, pn_ref, w_ref, v_ref, x1_ref, x2_ref):
    i = pl.program_id(1)
    tm = pm_ref.shape[1]
    row = lax.broadcasted_iota(jnp.int32, (tm, 1), 0)
    u = pm_ref[0]
    u_prev = pp_ref[0, SUB - 1:SUB, :] * (i > 0).astype(F32)
    u_next = pn_ref[0, 0:1, :] * (i < pl.num_programs(1) - 1).astype(F32)
    um1 = jnp.where(row == 0, u_prev, pltpu.roll(u, 1, 0))
    up1 = jnp.where(row == tm - 1, u_next, pltpu.roll(u, tm - 1, 0))
    y = w_ref[0:1, :] * um1 + w_ref[1:2, :] * u + w_ref[2:3, :] * up1
    W = HY_WIDTH
    v_ref[0] = y[:, 0 * W:1 * W]
    x1_ref[0] = y[:, 1 * W:2 * W]
    x2_ref[0] = y[:, 2 * W:3 * W]


def _hy_short_conv(pc, short_w, tm=512):
    B, L, C3 = pc.shape
    tm = min(tm, L)
    nh = tm // SUB
    last = L // SUB - 1
    out = pl.BlockSpec((1, tm, HY_WIDTH), lambda b, i: (b, i, 0))
    return pl.pallas_call(
        _hy_short_kernel,
        grid=(B, L // tm),
        in_specs=[
            pl.BlockSpec((1, tm, C3), lambda b, i: (b, i, 0)),
            pl.BlockSpec((1, SUB, C3), lambda b, i: (b, jnp.maximum(i * nh - 1, 0), 0)),
            pl.BlockSpec((1, SUB, C3), lambda b, i: (b, jnp.minimum((i + 1) * nh, last), 0)),
            pl.BlockSpec((3, C3), lambda b, i: (0, 0)),
        ],
        out_specs=[out, out, out],
        out_shape=[jax.ShapeDtypeStruct((B, L, HY_WIDTH), F32)] * 3,
        compiler_params=pltpu.CompilerParams(dimension_semantics=("parallel", "parallel")),
        name="hyena_short_conv",
    )(pc, pc, pc, short_w)


def _hy_filter_kernel(w1t_ref, w1c_ref, w1s_ref, b1_ref, fr_ref, w2_ref, b2_ref, w3_ref, bands_ref,
                      dl_ref, k_ref, sum_ref, *, L):
    tr = k_ref.shape[0]
    r = pl.program_id(0) * tr + lax.broadcasted_iota(jnp.int32, (tr, 1), 0)
    first = r < L
    pos = jnp.where(first, r, 2 * L - r).astype(F32)
    t = pos * (1.0 / (L - 1))
    bw = (pos * (2.0 * math.pi / L)) * bands_ref[...]
    pre = t * w1t_ref[...] + _dot_hi(jnp.cos(bw), w1c_ref[...]) - _dot_hi(jnp.sin(bw), w1s_ref[...])
    fr = fr_ref[...]
    a = jnp.sin(fr * (pre + b1_ref[...]))
    a = jnp.sin(fr * (_dot_hi(a, w2_ref[...]) + b2_ref[...]))
    h = _dot_hi(a, w3_ref[...])
    decay = jnp.exp(-t * dl_ref[...])
    W2 = HY_ORDER * HY_WIDTH
    use_f = first.astype(F32)
    use_b = ((r > L) | (r == 0)).astype(F32)
    hf = h[:, :W2] * decay * use_f
    hb = h[:, W2:] * decay * use_b
    k_ref[...] = hf + hb

    @pl.when(pl.program_id(0) == 0)
    def _():
        sum_ref[...] = jnp.zeros_like(sum_ref)

    sum_ref[...] += jnp.sum(jnp.abs(hf) + jnp.abs(hb), axis=0, keepdims=True)


def _hy_filter(L, w1, b1, freq, w2, b2, w3):
    tr = min(HY_FILT_ROWS, L)
    W2 = HY_ORDER * HY_WIDTH
    bands = jnp.zeros((1, 128), F32).at[0, :HY_BANDS].set(jnp.linspace(1e-4, HY_BANDS - 1, HY_BANDS, dtype=F32))
    w1c = jnp.zeros((128, HY_HIDDEN), F32).at[:HY_BANDS].set(w1[1:1 + HY_BANDS])
    w1s = jnp.zeros((128, HY_HIDDEN), F32).at[:HY_BANDS].set(w1[1 + HY_BANDS:])
    deltas = jnp.abs(jnp.linspace(math.log(HY_DECAY_TARGET) / HY_SLOW_PCT,
                                  math.log(HY_DECAY_TARGET) / HY_FAST_PCT, HY_WIDTH, dtype=F32))
    dl = jnp.tile(deltas, HY_ORDER).reshape(1, W2)
    args = (w1[0:1], w1c, w1s, b1.reshape(1, -1), freq.reshape(1, -1), w2, b2.reshape(1, -1), w3, bands, dl)
    full = lambda a: pl.BlockSpec(a.shape, lambda i: (0, 0))
    return pl.pallas_call(
        functools.partial(_hy_filter_kernel, L=L),
        grid=(2 * L // tr,),
        in_specs=[full(a) for a in args],
        out_specs=[pl.BlockSpec((tr, W2), lambda i: (i, 0)), pl.BlockSpec((1, W2), lambda i: (0, 0))],
        out_shape=[jax.ShapeDtypeStruct((2 * L, W2), F32), jax.ShapeDtypeStruct((1, W2), F32)],
        compiler_params=pltpu.CompilerParams(dimension_semantics=("arbitrary",)),
        name="hyena_filter",
    )(*args)


def _hy_dft_tables(L):
    N = 2 * L
    N2 = HY_N2
    N1 = N // N2
    def cs(n, rows, cols):
        ang = (2.0 * math.pi / n) * ((jnp.arange(rows)[:, None] * jnp.arange(cols)[None, :]) % n).astype(F32)
        return jnp.cos(ang), jnp.sin(ang)
    ca, sa = cs(N1, N1, N1)
    fa_full = jnp.concatenate([ca, -sa], axis=0)
    fa_inv = jnp.concatenate([ca[:N1 // 2], -sa[:N1 // 2]], axis=1) * (1.0 / N)
    cb, sb = cs(N2, N2, N2)
    fb = jnp.concatenate([jnp.concatenate([cb, sb], axis=1), jnp.concatenate([-sb, cb], axis=1)], axis=0)
    tc, ts = cs(N, N1, N2)
    tw = jnp.stack([tc, ts])[..., None] * jnp.ones((1, 1, 1, 128), F32)
    return dict(N1=N1, fa_half=fa_full[:, :N1 // 2].astype(BF16), fa_full=fa_full.astype(BF16),
                fa_inv=fa_inv.astype(BF16), fb=fb.astype(BF16), fb_t=fb.T.astype(BF16), tw=tw)


def _hy_n2_rows(N1):
    return SUB * max(1, min(HY_N2 // SUB, 256 // N1))


def _hy_dft_a_kernel(m_ref, x_ref, o_ref):
    N1, C = o_ref.shape[2], o_ref.shape[4]
    for g0 in range(0, x_ref.shape[2], SUB):
        x = jnp.concatenate([x_ref[0, :, g0 + j, :] for j in range(SUB)], axis=1).astype(BF16)
        y = _dot(m_ref[...], x)
        for j in range(SUB):
            o_ref[0, 0, :, g0 + j, :] = y[:N1, j * C:(j + 1) * C]
            o_ref[0, 1, :, g0 + j, :] = y[N1:, j * C:(j + 1) * C]


def _hy_dft_a(m, x):
    B, R, N2, C = x.shape
    N1 = m.shape[0] // 2
    W = HY_WIDTH
    G = _hy_n2_rows(N1)
    return pl.pallas_call(
        _hy_dft_a_kernel,
        grid=(B, N2 // G, C // W),
        in_specs=[pl.BlockSpec(m.shape, lambda b, g, c: (0, 0)),
                  pl.BlockSpec((1, R, G, W), lambda b, g, c: (b, 0, g, c))],
        out_specs=pl.BlockSpec((1, 2, N1, G, W), lambda b, g, c: (b, 0, 0, g, c)),
        out_shape=jax.ShapeDtypeStruct((B, 2, N1, N2, C), F32),
        compiler_params=pltpu.CompilerParams(dimension_semantics=("parallel", "parallel", "parallel")),
        name="hyena_dft_a",
    )(m, x)


def _hy_gate_kernel(m_ref, y_ref, z_ref, g_ref, s_ref, o_ref):
    C = o_ref.shape[3]
    for g0 in range(0, o_ref.shape[2], SUB):
        y = jnp.concatenate(
            [jnp.concatenate([y_ref[0, 0, :, g0 + j, :], y_ref[0, 1, :, g0 + j, :]], axis=0) for j in range(SUB)],
            axis=1)
        zc = _dot(m_ref[...], y.astype(BF16))
        for j in range(SUB):
            o_ref[0, :, g0 + j, :] = g_ref[0, :, g0 + j, :] * (
                zc[:, j * C:(j + 1) * C] + s_ref[...] * z_ref[0, :, g0 + j, :])


def _hy_inverse_gate(m, y, z, gate, skip_row):
    B, _, N1, N2, C = y.shape
    G = _hy_n2_rows(N1)
    tok = pl.BlockSpec((1, N1 // 2, G, C), lambda b, g: (b, 0, g, 0))
    return pl.pallas_call(
        _hy_gate_kernel,
        grid=(B, N2 // G),
        in_specs=[pl.BlockSpec(m.shape, lambda b, g: (0, 0)),
                  pl.BlockSpec((1, 2, N1, G, C), lambda b, g: (b, 0, 0, g, 0)),
                  tok, tok,
                  pl.BlockSpec((1, C), lambda b, g: (0, 0))],
        out_specs=tok,
        out_shape=jax.ShapeDtypeStruct((B, N1 // 2, N2, C), F32),
        compiler_params=pltpu.CompilerParams(dimension_semantics=("parallel", "parallel")),
        name="hyena_idft_a_gate",
    )(m, y, z, gate, skip_row)


def _cmul(ar, ai, br, bi):
    return ar * br - ai * bi, ar * bi + ai * br


def _hy_stage_b_kernel(a_ref, tw_ref, fb_ref, fbt_ref, k_ref, o_ref, *, conv):
    N2 = HY_N2
    for j in range(a_ref.shape[2]):
        tc = jnp.concatenate([tw_ref[0, j], tw_ref[0, j]], axis=1)
        ts = jnp.concatenate([tw_ref[1, j], tw_ref[1, j]], axis=1)
        br, bi = _cmul(a_ref[0, 0, j], a_ref[0, 1, j], tc, -ts)
        x = _dot(fb_ref[...], jnp.concatenate([br, bi], axis=0).astype(BF16))
        xr, xi = x[:N2], x[N2:]
        if conv:
            yr, yi = _cmul(xr, xi, k_ref[0, j], k_ref[1, j])
            y = _dot(fbt_ref[...], jnp.concatenate([yr, yi], axis=0).astype(BF16))
            xr, xi = _cmul(y[:N2], y[N2:], tc, ts)
        else:
            xr, xi = xr * k_ref[...], xi * k_ref[...]
        o_ref[0, 0, j] = xr
        o_ref[0, 1, j] = xi


def _hy_stage_b(a, tabs, kspec=None, order=0, scale=None):
    B, _, N1, N2, C = a.shape
    t1 = HY_K1_TILE
    conv = kspec is not None
    blk = pl.BlockSpec((1, 2, t1, N2, HY_WIDTH), lambda i, b, c: (b, 0, i, 0, c))
    if conv:
        k_arr = kspec
        k_spec = pl.BlockSpec((2, t1, N2, HY_WIDTH), lambda i, b, c: (0, i, 0, order))
    else:
        k_arr = scale
        k_spec = pl.BlockSpec((1, HY_WIDTH), lambda i, b, c: (0, c))
    return pl.pallas_call(
        functools.partial(_hy_stage_b_kernel, conv=conv),
        grid=(N1 // t1, B, C // HY_WIDTH),
        in_specs=[blk,
                  pl.BlockSpec((2, t1, N2, 128), lambda i, b, c: (0, i, 0, 0)),
                  pl.BlockSpec(tabs['fb'].shape, lambda i, b, c: (0, 0)),
                  pl.BlockSpec(tabs['fb_t'].shape, lambda i, b, c: (0, 0)),
                  k_spec],
        out_specs=blk,
        out_shape=jax.ShapeDtypeStruct(a.shape, F32),
        compiler_params=pltpu.CompilerParams(dimension_semantics=("parallel", "parallel", "parallel")),
        name="hyena_stage_b_conv" if conv else "hyena_stage_b_spectrum",
    )(a, tabs['tw'], tabs['fb'], tabs['fb_t'], k_arr)


def _hy_filter_spectrum(L, tabs, w1, b1, freq, w2, b2, w3):
    N1, N2 = tabs['N1'], HY_N2
    W2 = HY_ORDER * HY_WIDTH
    k, total = _hy_filter(L, w1, b1, freq, w2, b2, w3)
    a = _hy_dft_a(tabs['fa_full'], k.reshape(1, N1, N2, W2))
    return _hy_stage_b(a, tabs, scale=1.0 / total)[0]


def _hyena(pc, short_w, skip, tabs, kspec):
    B, L, _ = pc.shape
    N1, N2, C = tabs['N1'], HY_N2, HY_WIDTH
    v, x1, x2 = _hy_short_conv(pc, short_w)
    grid = lambda t: t.reshape(B, N1 // 2, N2, C)
    z = grid(v)
    for n, gate in enumerate((x1, x2)):
        a = _hy_dft_a(tabs['fa_half'], z)
        y = _hy_stage_b(a, tabs, kspec, order=n)
        z = _hy_inverse_gate(tabs['fa_inv'], y, z, grid(gate), skip[n].reshape(1, C))
    return z.reshape(B, L, C)


def _prep_layer_weights(l, w_in, w_out, na_rpb, mla_w_uq, mla_w_ukv, ffn_w_up, ffn_w_down):
    wi = w_in[l]
    s0, s1, s2 = NA_COLS, NA_COLS + HG_COLS, NA_COLS + HG_COLS + HY_COLS
    w_ml = jnp.pad(wi[:, s2:], ((0, 0), (0, MLA_COLS_PAD - MLA_COLS)))
    return dict(
        w_na=wi[:, :s0].astype(BF16), w_hg=wi[:, s0:s1].astype(BF16), w_hy=wi[:, s1:s2].astype(BF16),
        w_ml=w_ml.astype(BF16), w_out=w_out[l].astype(BF16), na_bias=_na_bias_table(na_rpb[l]),
        mla=_mla_weights(mla_w_uq[l], mla_w_ukv[l]),
        w_up=ffn_w_up[l].astype(BF16), w_down=ffn_w_down[l].astype(BF16),
    )


def _trunk_layer(x, mod, lb, g, lw, p, cos, sin, hy_tabs, hy_spec):
    pa, pb, pc, pd = _in_proj(x, mod, g, lw['w_na'], lw['w_hg'], lw['w_hy'], lw['w_ml'])
    o_na = _neighbourhood_attention(pa, lw['na_bias'])
    hg_dirs = _hgrn2_scan(pb, lb)
    o_hy = _hyena(pc, p['hy_short'], p['hy_skip'], hy_tabs, hy_spec)
    L = x.shape[1]
    q, k, v = _mla_prep(pd, cos[:L], sin[:L], p['mla_q_norm'], p['mla_kv_norm'], lw['mla'])
    o_ml = _mla_flash(q, k, v)
    x1, h2 = _out_proj(x, mod, g, o_na, hg_dirs, pb, p['hg_norm'], o_hy, o_ml, lw['w_out'])
    return _conv_ffn(h2, x1, mod, g, lw['w_up'], p['ffn_conv'], lw['w_down'])


def kernel(x_prompt, x_sample, c_prompt, c_sample, ada_w, ada_b, norm_g, w_in, w_out, na_rpb, hg_lb,
           hg_norm, hy_short, hy_w1, hy_b1, hy_freq, hy_w2, hy_b2, hy_w3, hy_skip, mla_q_norm,
           mla_kv_norm, mla_w_uq, mla_w_ukv, ffn_w_up, ffn_conv, ffn_w_down):
    Bp, Bs = x_prompt.shape[0], x_sample.shape[0]
    Lp, Ls = x_prompt.shape[1], x_sample.shape[1]
    lb_soft = jax.nn.softmax(hg_lb.astype(F32), axis=0)
    lower_bounds = jnp.cumsum(lb_soft, axis=0) - lb_soft[:1]

    R = -(-(Bp + Bs) // 8) * 8
    cond = jnp.zeros((R, D_MODEL), F32).at[:Bp].set(c_prompt).at[Bp:Bp + Bs].set(c_sample)
    mod = _ada_modulation(cond, ada_w, ada_b).reshape(DEPTH, R, 6, D_MODEL)
    mod = jnp.pad(mod, ((0, 0), (0, 0), (0, 2), (0, 0)))

    cos, sin = _rope_tables(max(Lp, Ls))
    tabs_p, tabs_s = _hy_dft_tables(Lp), _hy_dft_tables(Ls)
    y_prompt, y_sample = x_prompt, x_sample
    for l in range(DEPTH):
        lw = _prep_layer_weights(l, w_in, w_out, na_rpb, mla_w_uq, mla_w_ukv, ffn_w_up, ffn_w_down)
        p = dict(hg_norm=hg_norm[l], hy_short=hy_short[l], hy_skip=hy_skip[l],
                 mla_q_norm=mla_q_norm[l].reshape(1, -1), mla_kv_norm=mla_kv_norm[l].reshape(1, -1),
                 ffn_conv=ffn_conv[l])
        filt = (hy_w1[l], hy_b1[l], hy_freq[l], hy_w2[l], hy_b2[l], hy_w3[l])
        spec_p = _hy_filter_spectrum(Lp, tabs_p, *filt)
        spec_s = _hy_filter_spectrum(Ls, tabs_s, *filt)
        y_prompt = _trunk_layer(y_prompt, mod[l, :Bp], lower_bounds[l], norm_g[l], lw, p, cos, sin,
                                tabs_p, spec_p)
        y_sample = _trunk_layer(y_sample, mod[l, Bp:Bp + Bs], lower_bounds[l], norm_g[l], lw, p, cos, sin,
                                tabs_s, spec_s)
    return (y_prompt, y_sample)
```

```python
import functools
import math

import jax
import jax.numpy as jnp
from jax import lax
from jax.experimental import pallas as pl
from jax.experimental.pallas import tpu as pltpu

F32 = jnp.float32
BF16 = jnp.bfloat16

D_MODEL = 1024
DEPTH = 4
GRID_W = 64
HEAD_DIM = 64
NA_WIDTH = 256
NA_HEADS = 4
NA_KH = 8
NA_KW = 16
HG_WIDTH = 256
HG_HEADS = 4
HG_CHUNK = 64
HY_WIDTH = 256
HY_ORDER = 2
HY_EMB = 33
HY_BANDS = 16
HY_HIDDEN = 64
HY_DECAY_TARGET = 1e-2
HY_FAST_PCT = 0.3
HY_SLOW_PCT = 1.5
MLA_WIDTH = 256
MLA_HEADS = 4
MLA_NOPE = 64
MLA_ROPE = 32
MLA_V = 64
MLA_Q_LORA = 256
MLA_KV_LORA = 128
MLA_HEAD_PAD = 128
ROPE_THETA = 10000.0
D_FF = 2816
EPS = 1e-6
NA_COLS = 3 * NA_WIDTH
HG_COLS = 5 * HG_WIDTH
HY_COLS = 3 * HY_WIDTH
MLA_COLS = MLA_Q_LORA + MLA_KV_LORA + MLA_ROPE
MLA_COLS_PAD = 512
NEG_BIG = -1e30
SUB = 8
LANES = 128


def _dot(a, b):
    return jnp.dot(a, b, preferred_element_type=F32)


def _dot_nt(a, b):
    return lax.dot_general(a, b, (((1,), (1,)), ((), ())), preferred_element_type=F32)


def _rms(x, g):
    return x * lax.rsqrt(jnp.mean(x * x, axis=-1, keepdims=True) + EPS) * g


def _ada_kernel(c_ref, w_ref, b_ref, o_ref):
    c = c_ref[...]
    s = c * jax.nn.sigmoid(c)
    o_ref[0] = _dot(s.astype(BF16), w_ref[0].astype(BF16)) + b_ref[0]


def _ada_modulation(cond, ada_w, ada_b):
    R = cond.shape[0]
    tn = 1536
    return pl.pallas_call(
        _ada_kernel,
        grid=(DEPTH, 6 * D_MODEL // tn),
        in_specs=[
            pl.BlockSpec((R, D_MODEL), lambda l, n: (0, 0)),
            pl.BlockSpec((1, D_MODEL, tn), lambda l, n: (l, 0, n)),
            pl.BlockSpec((1, 1, tn), lambda l, n: (l, 0, n)),
        ],
        out_specs=pl.BlockSpec((1, R, tn), lambda l, n: (l, 0, n)),
        out_shape=jax.ShapeDtypeStruct((DEPTH, R, 6 * D_MODEL), F32),
        name="ada_modulation",
    )(cond, ada_w, ada_b.reshape(DEPTH, 1, 6 * D_MODEL))


def _in_proj_kernel(x_ref, mod_ref, g_ref, wna, whg, why, wml, ona, ohg, ohy, oml):
    x = x_ref[0]
    h = _rms(x, g_ref[0:1, :]) * (1.0 + mod_ref[0, 1:2, :]) + mod_ref[0, 0:1, :]
    hb = h.astype(BF16)
    ona[0] = _dot(hb, wna[...])
    ohg[0] = _dot(hb, whg[...])
    ohy[0] = _dot(hb, why[...])
    oml[0] = _dot(hb, wml[...])


def _in_proj(x, mod, g, w_na, w_hg, w_hy, w_ml, tm=512):
    B, L, D = x.shape
    tm = min(tm, L)
    full = lambda a: pl.BlockSpec(a.shape, lambda b, i: (0, 0))
    outs = [NA_COLS, HG_COLS, HY_COLS, MLA_COLS_PAD]
    return pl.pallas_call(
        _in_proj_kernel,
        grid=(B, L // tm),
        in_specs=[
            pl.BlockSpec((1, tm, D), lambda b, i: (b, i, 0)),
            pl.BlockSpec((1, 8, D), lambda b, i: (b, 0, 0)),
            full(g), full(w_na), full(w_hg), full(w_hy), full(w_ml),
        ],
        out_specs=[pl.BlockSpec((1, tm, n), lambda b, i: (b, i, 0)) for n in outs],
        out_shape=[jax.ShapeDtypeStruct((B, L, n), F32) for n in outs],
        compiler_params=pltpu.CompilerParams(dimension_semantics=("parallel", "parallel")),
        name="in_proj",
    )(x, mod, g, w_na, w_hg, w_hy, w_ml)


def _head_mean_sq(o, sel):
    sq = o * o
    hi = sq.astype(BF16)
    lo = (sq - hi.astype(F32)).astype(BF16)
    return (_dot(hi, sel) + _dot(lo, sel)) * (1.0 / (HG_WIDTH // HG_HEADS))


def _out_proj_kernel(x_ref, mod_ref, g_ref, na, hgf, hgb, hgg, hgn, sel, hy, ml, w_ref, x1_ref, h2_ref):
    W = NA_WIDTH
    o = hgf[0, 0] + hgb[0, 0]
    gate = hgg[0]
    hg = o * lax.rsqrt(_head_mean_sq(o, sel[...]) + EPS) * hgn[...] * (gate * jax.nn.sigmoid(gate))
    mix = _dot(na[0].astype(BF16), w_ref[0 * W:1 * W, :])
    mix += _dot(hg.astype(BF16), w_ref[1 * W:2 * W, :])
    mix += _dot(hy[0].astype(BF16), w_ref[2 * W:3 * W, :])
    mix += _dot(ml[0].astype(BF16), w_ref[3 * W:4 * W, :])
    x1 = x_ref[0] + mod_ref[0, 2:3, :] * _rms(mix, g_ref[1:2, :])
    x1_ref[0] = x1
    h2 = _rms(x1, g_ref[2:3, :]) * (1.0 + mod_ref[0, 4:5, :]) + mod_ref[0, 3:4, :]
    h2_ref[0] = h2.astype(BF16)


def _out_proj(x, mod, g, o_na, hg_dirs, pb, hg_norm, o_hy, o_ml, w_out, tm=512):
    B, L, D = x.shape
    tm = min(tm, L)
    tok = lambda n: pl.BlockSpec((1, tm, n), lambda b, i: (b, i, 0))
    sel = _hgrn2_tables()[1]
    return pl.pallas_call(
        _out_proj_kernel,
        grid=(B, L // tm),
        in_specs=[
            tok(D),
            pl.BlockSpec((1, 8, D), lambda b, i: (b, 0, 0)),
            pl.BlockSpec(g.shape, lambda b, i: (0, 0)),
            tok(NA_WIDTH),
            pl.BlockSpec((1, 1, tm, HG_WIDTH), lambda b, i: (b, 0, i, 0)),
            pl.BlockSpec((1, 1, tm, HG_WIDTH), lambda b, i: (b, 1, i, 0)),
            pl.BlockSpec((1, tm, HG_WIDTH), lambda b, i: (b, i, 4)),
            pl.BlockSpec((1, HG_WIDTH), lambda b, i: (0, 0)),
            pl.BlockSpec(sel.shape, lambda b, i: (0, 0)),
            tok(HY_WIDTH), tok(MLA_WIDTH),
            pl.BlockSpec(w_out.shape, lambda b, i: (0, 0)),
        ],
        out_specs=[tok(D), tok(D)],
        out_shape=[jax.ShapeDtypeStruct((B, L, D), F32), jax.ShapeDtypeStruct((B, L, D), BF16)],
        compiler_params=pltpu.CompilerParams(dimension_semantics=("parallel", "parallel")),
        name="out_proj",
    )(x, mod, g, o_na, hg_dirs, hg_dirs, pb, hg_norm.reshape(1, HG_WIDTH), sel, o_hy, o_ml, w_out)


FFN_HALO = 16


def _ffn_kernel(hm_ref, hp_ref, hn_ref, x_ref, mod_ref, g_ref, wu_ref, cw_ref, wd_ref, o_ref, acc_ref,
                hext_ref):
    i = pl.program_id(1)
    tm = hm_ref.shape[1]
    nf = wd_ref.shape[0]
    not_first = (i > 0).astype(F32)
    not_last = (i < pl.num_programs(1) - 1).astype(F32)
    sub = lax.broadcasted_iota(jnp.int32, (SUB, 1), 0)

    H = FFN_HALO
    hext_ref[0:H, :] = hp_ref[0]
    hext_ref[H:H + tm, :] = hm_ref[0]
    hext_ref[H + tm:2 * H + tm, :] = hn_ref[0]

    def conv_branch(t):
        c = cw_ref[t]
        ue = _dot(hext_ref[...], wu_ref[t])
        u = ue[H:H + tm]
        u_prev = ue[H - 1:H] * not_first
        u_next = ue[H + tm:H + tm + 1] * not_last
        down = pltpu.roll(u, 1, 0)
        up = pltpu.roll(u, tm - 1, 0)
        down = jnp.concatenate([jnp.where(sub == 0, u_prev, down[:SUB]), down[SUB:]], axis=0)
        up = jnp.concatenate([up[:tm - SUB], jnp.where(sub == SUB - 1, u_next, up[tm - SUB:])], axis=0)
        return c[0:1, :] * down + c[1:2, :] * u + c[2:3, :] * up

    def tile(f, carry):
        a = conv_branch(f)
        b = conv_branch(nf + f)
        act = (jax.nn.gelu(a, approximate=True) * b).astype(BF16)
        acc_ref[...] += _dot(act, wd_ref[f])
        return carry

    acc_ref[...] = jnp.zeros_like(acc_ref)
    lax.fori_loop(0, nf, tile, 0, unroll=FFN_UNROLL)
    o_ref[0] = x_ref[0] + mod_ref[0, 5:6, :] * _rms(acc_ref[...], g_ref[3:4, :])


FFN_TF = 256
FFN_UNROLL = 2


def _conv_ffn(h2, x1, mod, g, w_up, conv_w, w_down, tm=512):
    B, L, D = x1.shape
    tm = min(tm, L)
    nh = tm // FFN_HALO
    last_halo = L // FFN_HALO - 1
    once = lambda a: pl.BlockSpec(a.shape, lambda b, i: (0,) * a.ndim, pipeline_mode=pl.Buffered(1))
    return pl.pallas_call(
        _ffn_kernel,
        grid=(B, L // tm),
        in_specs=[
            pl.BlockSpec((1, tm, D), lambda b, i: (b, i, 0)),
            pl.BlockSpec((1, FFN_HALO, D), lambda b, i: (b, jnp.maximum(i * nh - 1, 0), 0)),
            pl.BlockSpec((1, FFN_HALO, D), lambda b, i: (b, jnp.minimum((i + 1) * nh, last_halo), 0)),
            pl.BlockSpec((1, tm, D), lambda b, i: (b, i, 0)),
            pl.BlockSpec((1, 8, D), lambda b, i: (b, 0, 0)),
            pl.BlockSpec(g.shape, lambda b, i: (0, 0)),
            once(w_up), once(conv_w), once(w_down),
        ],
        out_specs=pl.BlockSpec((1, tm, D), lambda b, i: (b, i, 0)),
        out_shape=jax.ShapeDtypeStruct((B, L, D), F32),
        scratch_shapes=[pltpu.VMEM((tm, D), F32), pltpu.VMEM((tm + 2 * FFN_HALO, D), BF16)],
        compiler_params=pltpu.CompilerParams(dimension_semantics=("parallel", "parallel")),
        name="conv_ffn",
    )(h2, h2, h2, x1, mod, g, w_up, conv_w, w_down)


def _ffn_weight_tiles(w_up, conv_w, w_down):
    D = w_up.shape[0]
    n2 = 2 * D_FF // FFN_TF
    wu = w_up.reshape(D, n2, FFN_TF).transpose(1, 0, 2).astype(BF16)
    cw = conv_w.reshape(3, n2, FFN_TF).transpose(1, 0, 2)
    wd = w_down.reshape(D_FF // FFN_TF, FFN_TF, D).astype(BF16)
    return wu, cw, wd


NA_RB = 8


def _na_kernel(q_ref, kp_ref, kc_ref, kn_ref, vp_ref, vc_ref, vn_ref, bias_ref, o_ref,
               kbuf, vbuf, *, rows):
    i = pl.program_id(1)
    blk = NA_RB * GRID_W
    win = NA_KH * GRID_W
    kbuf[0 * blk:1 * blk, :] = kp_ref[0].astype(BF16)
    kbuf[1 * blk:2 * blk, :] = kc_ref[0].astype(BF16)
    kbuf[2 * blk:3 * blk, :] = kn_ref[0].astype(BF16)
    vbuf[0 * blk:1 * blk, :] = vp_ref[0].astype(BF16)
    vbuf[1 * blk:2 * blk, :] = vc_ref[0].astype(BF16)
    vbuf[2 * blk:3 * blk, :] = vn_ref[0].astype(BF16)
    scale = HEAD_DIM ** -0.5 * math.log2(math.e)
    lane_head = lax.broadcasted_iota(jnp.int32, (1, NA_WIDTH), 1) // HEAD_DIM
    for j in range(NA_RB):
        r = i * NA_RB + j
        start = jnp.clip(r - NA_KH // 2, 0, rows - NA_KH)
        loc = start - (i - 1) * NA_RB
        dr0 = start - r + (NA_KH - 1)
        off = pl.multiple_of(loc * GRID_W, GRID_W)
        kw = kbuf[pl.ds(off, win), :]
        vw = vbuf[pl.ds(off, win), :]
        qj = (q_ref[0, j * GRID_W:(j + 1) * GRID_W, :] * scale).astype(BF16)
        qbd = jnp.concatenate([jnp.where(lane_head == h, qj, jnp.zeros_like(qj)) for h in range(NA_HEADS)],
                              axis=0)
        s = _dot_nt(qbd, kw) + bias_ref[dr0]
        m = jnp.max(s, axis=-1, keepdims=True)
        p = jnp.exp2(s - m)
        l = jnp.sum(p, axis=-1, keepdims=True)
        pv = _dot(p.astype(BF16), vw) / l
        o = None
        for h in range(NA_HEADS):
            part = jnp.where(lane_head == h, pv[h * GRID_W:(h + 1) * GRID_W], 0.0)
            o = part if o is None else o + part
        o_ref[0, j * GRID_W:(j + 1) * GRID_W, :] = o


def _na_bias_table(rpb):
    c = jnp.arange(GRID_W)
    dc = jnp.clip(c[None, :] - c[:, None] + (NA_KW - 1), 0, 2 * NA_KW - 2)
    col_start = jnp.clip(c - NA_KW // 2, 0, GRID_W - NA_KW)
    ok = (c[None, :] >= col_start[:, None]) & (c[None, :] < col_start[:, None] + NA_KW)
    e = jnp.where(ok[None, None], rpb.astype(F32)[:, :, dc], NEG_BIG)
    t = jnp.stack([e[:, d0:d0 + NA_KH] for d0 in range(NA_KH)], axis=1)
    t = t.transpose(1, 0, 3, 2, 4).reshape(NA_KH, NA_HEADS * GRID_W, NA_KH * GRID_W)
    return jnp.where(t > 0.5 * NEG_BIG, t * math.log2(math.e), NEG_BIG)


def _neighbourhood_attention(pa, bias):
    B, L, _ = pa.shape
    rows = L // GRID_W
    blk = NA_RB * GRID_W
    nblk = rows // NA_RB
    spec = lambda col, shift: pl.BlockSpec(
        (1, blk, NA_WIDTH), lambda b, i: (b, jnp.clip(i + shift, 0, nblk - 1), col))
    return pl.pallas_call(
        functools.partial(_na_kernel, rows=rows),
        grid=(B, nblk),
        in_specs=[spec(0, 0), spec(1, -1), spec(1, 0), spec(1, 1), spec(2, -1), spec(2, 0), spec(2, 1),
                  pl.BlockSpec(bias.shape, lambda b, i: (0, 0, 0))],
        out_specs=pl.BlockSpec((1, blk, NA_WIDTH), lambda b, i: (b, i, 0)),
        out_shape=jax.ShapeDtypeStruct((B, L, NA_WIDTH), F32),
        scratch_shapes=[pltpu.VMEM((3 * blk, NA_WIDTH), BF16), pltpu.VMEM((3 * blk, NA_WIDTH), BF16)],
        compiler_params=pltpu.CompilerParams(dimension_semantics=("parallel", "parallel")),
        name="neighbourhood_attention",
    )(pa, pa, pa, pa, pa, pa, pa, bias)


def _rope_table_kernel(inv_ref, cos_ref, sin_ref):
    t = cos_ref.shape[0]
    pos = (pl.program_id(0) * t + lax.broadcasted_iota(jnp.int32, (t, MLA_HEAD_PAD), 0)).astype(F32)
    ang = pos * inv_ref[...]
    cos_ref[...] = jnp.cos(ang)
    sin_ref[...] = jnp.sin(ang)


def _rope_tables(L):
    half = MLA_ROPE // 2
    inv = ROPE_THETA ** (-jnp.arange(half, dtype=F32) / half)
    inv_row = jnp.zeros((1, MLA_HEAD_PAD), F32).at[0, MLA_NOPE:MLA_NOPE + MLA_ROPE].set(jnp.tile(inv, 2))
    t = min(L, 1024)
    return pl.pallas_call(
        _rope_table_kernel,
        grid=(L // t,),
        in_specs=[pl.BlockSpec((1, MLA_HEAD_PAD), lambda i: (0, 0))],
        out_specs=[pl.BlockSpec((t, MLA_HEAD_PAD), lambda i: (i, 0))] * 2,
        out_shape=[jax.ShapeDtypeStruct((L, MLA_HEAD_PAD), F32)] * 2,
        name="rope_tables",
    )(inv_row)


def _mla_prep_kernel(pd_ref, cos_ref, sin_ref, qn_ref, kvn_ref, wq, wqr, wk, wv, we, wer,
                     q_out, k_out, v_out):
    pd = pd_ref[0]
    cos = cos_ref[...]
    sin = sin_ref[...]
    nq = _rms(pd[:, :MLA_Q_LORA], qn_ref[...]).astype(BF16)
    nkv = _rms(pd[:, MLA_Q_LORA:MLA_Q_LORA + MLA_KV_LORA], kvn_ref[...]).astype(BF16)
    kr = pd[:, MLA_Q_LORA + MLA_KV_LORA:].astype(BF16)
    q = _dot(nq, wq[...])
    q_rot = _dot(nq, wqr[...])
    k = _dot(nkv, wk[...]) + _dot(kr, we[...])
    k_rot = _dot(kr, wer[...])
    scale = (MLA_NOPE + MLA_ROPE) ** -0.5 * math.log2(math.e)
    for h in range(MLA_HEADS):
        hs = slice(h * MLA_HEAD_PAD, (h + 1) * MLA_HEAD_PAD)
        q_out[0, :, hs] = ((q[:, hs] * cos + q_rot[:, hs] * sin) * scale).astype(BF16)
        k_out[0, :, hs] = (k[:, hs] * cos + k_rot[:, hs] * sin).astype(BF16)
    v_out[0] = _dot(nkv, wv[...]).astype(BF16)


def _mla_weights(w_uq, w_ukv):
    half = MLA_ROPE // 2
    P = jnp.zeros((MLA_ROPE, MLA_ROPE), F32)
    P = P.at[jnp.arange(half) + half, jnp.arange(half)].set(-1.0)
    P = P.at[jnp.arange(half), jnp.arange(half) + half].set(1.0)
    HP = MLA_HEAD_PAD
    wq = jnp.zeros((MLA_Q_LORA, MLA_HEADS * HP), F32)
    wqr = jnp.zeros_like(wq)
    wk = jnp.zeros((MLA_KV_LORA, MLA_HEADS * HP), F32)
    wv = jnp.zeros((MLA_KV_LORA, MLA_HEADS * MLA_V), F32)
    we = jnp.zeros((MLA_COLS_PAD - MLA_Q_LORA - MLA_KV_LORA, MLA_HEADS * HP), F32)
    wer = jnp.zeros_like(we)
    eye = jnp.eye(MLA_ROPE, dtype=F32)
    for h in range(MLA_HEADS):
        qh = w_uq[:, h * (MLA_NOPE + MLA_ROPE):(h + 1) * (MLA_NOPE + MLA_ROPE)]
        wq = wq.at[:, h * HP:h * HP + MLA_NOPE + MLA_ROPE].set(qh)
        wqr = wqr.at[:, h * HP + MLA_NOPE:h * HP + MLA_NOPE + MLA_ROPE].set(qh[:, MLA_NOPE:] @ P)
        kvh = w_ukv[:, h * (MLA_NOPE + MLA_V):(h + 1) * (MLA_NOPE + MLA_V)]
        wk = wk.at[:, h * HP:h * HP + MLA_NOPE].set(kvh[:, :MLA_NOPE])
        wv = wv.at[:, h * MLA_V:(h + 1) * MLA_V].set(kvh[:, MLA_NOPE:])
        we = we.at[:MLA_ROPE, h * HP + MLA_NOPE:h * HP + MLA_NOPE + MLA_ROPE].set(eye)
        wer = wer.at[:MLA_ROPE, h * HP + MLA_NOPE:h * HP + MLA_NOPE + MLA_ROPE].set(P)
    return tuple(w.astype(BF16) for w in (wq, wqr, wk, wv, we, wer))


def _mla_prep(pd, cos, sin, q_norm, kv_norm, weights, tm=512):
    B, L, _ = pd.shape
    tm = min(tm, L)
    full = lambda a: pl.BlockSpec(a.shape, lambda b, i: (0, 0))
    HP = MLA_HEADS * MLA_HEAD_PAD
    return pl.pallas_call(
        _mla_prep_kernel,
        grid=(B, L // tm),
        in_specs=[
            pl.BlockSpec((1, tm, MLA_COLS_PAD), lambda b, i: (b, i, 0)),
            pl.BlockSpec((tm, MLA_HEAD_PAD), lambda b, i: (i, 0)),
            pl.BlockSpec((tm, MLA_HEAD_PAD), lambda b, i: (i, 0)),
            full(q_norm), full(kv_norm),
        ] + [full(w) for w in weights],
        out_specs=[pl.BlockSpec((1, tm, HP), lambda b, i: (b, i, 0)),
                   pl.BlockSpec((1, tm, HP), lambda b, i: (b, i, 0)),
                   pl.BlockSpec((1, tm, MLA_WIDTH), lambda b, i: (b, i, 0))],
        out_shape=[jax.ShapeDtypeStruct((B, L, HP), BF16), jax.ShapeDtypeStruct((B, L, HP), BF16),
                   jax.ShapeDtypeStruct((B, L, MLA_WIDTH), BF16)],
        compiler_params=pltpu.CompilerParams(dimension_semantics=("parallel", "parallel")),
        name="mla_prep",
    )(pd, cos, sin, q_norm, kv_norm, *weights)


FLASH_ROWS = 16
FLASH_HEADS = 4


def _flash_kernel(q_ref, k_ref, v_ref, o_ref, m_sc, acc_sc, p_sc):
    kv = pl.program_id(3)
    _, tq, tk = p_sc.shape
    RB = FLASH_ROWS

    @pl.when(kv == 0)
    def _():
        m_sc[...] = jnp.full_like(m_sc, -jnp.inf)
        acc_sc[...] = jnp.zeros_like(acc_sc)

    first_head = lax.broadcasted_iota(jnp.int32, (1, LANES), 1) < MLA_V
    for h in range(FLASH_HEADS):
        q = q_ref[0, :, h * MLA_HEAD_PAD:(h + 1) * MLA_HEAD_PAD]
        k = k_ref[0, :, h * MLA_HEAD_PAD:(h + 1) * MLA_HEAD_PAD]
        s = _dot_nt(q, k)
        alphas = []
        for r in range(tq // RB):
            rows = slice(r * RB, (r + 1) * RB)
            x = s[rows, 0:LANES]
            for c in range(1, tk // LANES):
                x = jnp.maximum(x, s[rows, c * LANES:(c + 1) * LANES])
            m_prev = m_sc[h, rows, :]
            m_new = jnp.maximum(m_prev, jnp.max(x, axis=-1, keepdims=True))
            alphas.append(jnp.exp2(m_prev - m_new))
            m_sc[h, rows, :] = m_new
            for c in range(tk // LANES):
                cols = slice(c * LANES, (c + 1) * LANES)
                p_sc[h, rows, cols] = jnp.exp2((s[rows, cols] - m_new).astype(BF16))
        v = v_ref[0, :, (h // 2) * LANES:(h // 2 + 1) * LANES]
        ones = jnp.ones_like(v)
        v_aug = jnp.where(first_head, v, ones) if h % 2 == 0 else jnp.where(first_head, ones, v)
        acc_sc[h] = jnp.concatenate(alphas, axis=0) * acc_sc[h] + _dot(p_sc[h], v_aug)

    @pl.when(kv == pl.num_programs(3) - 1)
    def _():
        for g in range(FLASH_HEADS // 2):
            a0 = acc_sc[2 * g]
            a1 = acc_sc[2 * g + 1]
            num = jnp.where(first_head, a0, a1)
            den = jnp.where(first_head, pltpu.roll(a0, MLA_V, 1), pltpu.roll(a1, MLA_V, 1))
            o_ref[0, :, g * LANES:(g + 1) * LANES] = num / den


def _mla_flash(q, k, v, tq=512, tk=2048):
    B, L, _ = q.shape
    tq = min(tq, L)
    tk = min(tk, L)
    nh = FLASH_HEADS
    stat = pltpu.VMEM((nh, tq, LANES), F32)
    return pl.pallas_call(
        _flash_kernel,
        grid=(B, MLA_HEADS // nh, L // tq, L // tk),
        in_specs=[
            pl.BlockSpec((1, tq, nh * MLA_HEAD_PAD), lambda b, h, i, j: (b, i, h)),
            pl.BlockSpec((1, tk, nh * MLA_HEAD_PAD), lambda b, h, i, j: (b, j, h)),
            pl.BlockSpec((1, tk, nh * MLA_V), lambda b, h, i, j: (b, j, h)),
        ],
        out_specs=pl.BlockSpec((1, tq, nh * MLA_V), lambda b, h, i, j: (b, i, h)),
        out_shape=jax.ShapeDtypeStruct((B, L, MLA_WIDTH), F32),
        scratch_shapes=[stat, stat, pltpu.VMEM((nh, tq, tk), BF16)],
        compiler_params=pltpu.CompilerParams(
            dimension_semantics=("parallel", "parallel", "parallel", "arbitrary")),
        name="mla_flash",
    )(q, k, v)


HG_T = 32
HG_TB = 512


def _hgrn2_kernel(q_ref, f_ref, i_ref, lb_ref, tri_ref, sel_ref, bd_ref, o_ref, st_ref, g_ref, *, nsub):
    z = pl.program_id(1)
    T = HG_T
    ngrp = T // SUB

    @pl.when(pl.program_id(2) == 0)
    def _():
        st_ref[...] = jnp.zeros_like(st_ref)

    lb = lb_ref[0]
    sel = sel_ref[...]
    row = lax.broadcasted_iota(jnp.int32, (SUB, HG_WIDTH), 0)

    def step(c, backward):
        r0 = pl.multiple_of(c * T, T)
        qs = q_ref[0, pl.ds(r0, T), :]
        qs = qs * jax.nn.sigmoid(qs)
        fg = lb + (1.0 - lb) * jax.nn.sigmoid(f_ref[0, pl.ds(r0, T), :])
        kk = 1.0 - fg
        vv = i_ref[0, pl.ds(r0, T), :]
        b = _split_dot_rhs(tri_ref[0], jnp.log(fg) * math.log2(math.e))
        edge = b[0:1, :] if backward else b[T - 1:T, :]
        st = st_ref[...]
        o_inter = _dot_nt((qs * jnp.exp2(b)).astype(BF16), st.astype(BF16))
        off = 0
        offs = {}
        for s in range(T):
            gs = s // SUB
            groups = range(0, gs + 1) if backward else range(gs, ngrp)
            bs = b[s:s + 1, :]
            ks = kk[s:s + 1, :]
            for gidx in groups:
                rs = slice(gidx * SUB, (gidx + 1) * SUB)
                gval = qs[rs] * jnp.exp2(b[rs] - bs) * ks
                if gidx == gs:
                    keep = (row <= s - gs * SUB) if backward else (row >= s - gs * SUB)
                    gval = jnp.where(keep, gval, 0.0)
                g_ref[off:off + SUB, :] = gval
                offs[(s, gidx)] = off
                off += SUB
        red = _dot(g_ref[...].astype(BF16), sel)
        outs = []
        for gidx in range(ngrp):
            acc = o_inter[gidx * SUB:(gidx + 1) * SUB]
            for s in range(T):
                if (s, gidx) in offs:
                    o0 = offs[(s, gidx)]
                    acc = acc + red[o0:o0 + SUB] * vv[s:s + 1, :]
            outs.append(acc)
        o_ref[0, 0, pl.ds(r0, T), :] = jnp.concatenate(outs, axis=0)
        kt = (kk * jnp.exp2(edge - b)).astype(BF16)
        upd = lax.dot_general(vv.astype(BF16), kt, (((0,), (0,)), ((), ())), preferred_element_type=F32)
        st_ref[...] = st * jnp.exp2(edge) + upd * bd_ref[...]

    @pl.when(z == 0)
    def _():
        lax.fori_loop(0, nsub, lambda c, _: (step(c, False), 0)[1], 0, unroll=4)

    @pl.when(z == 1)
    def _():
        lax.fori_loop(0, nsub, lambda c, _: (step(nsub - 1 - c, True), 0)[1], 0, unroll=4)


def _split_dot_rhs(tri, x):
    tb = tri.astype(BF16)
    acc = None
    for _ in range(3):
        hi = x.astype(BF16)
        part = _dot(tb, hi)
        acc = part if acc is None else acc + part
        x = x - hi.astype(F32)
    return acc


def _hgrn2_tables():
    T = HG_T
    r = jnp.arange(T)
    tri = jnp.stack([(r[None, :] <= r[:, None]), (r[None, :] >= r[:, None])]).astype(F32)
    head = jnp.arange(HG_WIDTH) // (HG_WIDTH // HG_HEADS)
    same = (head[:, None] == head[None, :])
    return tri, same.astype(BF16), same.astype(F32)


def _hgrn2_groups():
    T, ngrp = HG_T, HG_T // SUB
    return sum(ngrp - s // SUB for s in range(T))


def _hgrn2_scan(pb, lb):
    B, L, _ = pb.shape
    tb = min(HG_TB, L)
    nblk = L // tb
    tri, sel, bd = _hgrn2_tables()
    W = HG_WIDTH
    blk = lambda z, j: j + z * (nblk - 1 - 2 * j)
    return pl.pallas_call(
        functools.partial(_hgrn2_kernel, nsub=tb // HG_T),
        grid=(B, 2, nblk),
        in_specs=[
            pl.BlockSpec((1, tb, W), lambda b, z, j: (b, blk(z, j), 0)),
            pl.BlockSpec((1, tb, W), lambda b, z, j: (b, blk(z, j), 1 + z)),
            pl.BlockSpec((1, tb, W), lambda b, z, j: (b, blk(z, j), 3)),
            pl.BlockSpec((1, 1, W), lambda b, z, j: (z, 0, 0)),
            pl.BlockSpec((1, HG_T, HG_T), lambda b, z, j: (z, 0, 0)),
            pl.BlockSpec((W, W), lambda b, z, j: (0, 0)),
            pl.BlockSpec((W, W), lambda b, z, j: (0, 0)),
        ],
        out_specs=pl.BlockSpec((1, 1, tb, W), lambda b, z, j: (b, z, blk(z, j), 0)),
        out_shape=jax.ShapeDtypeStruct((B, 2, L, W), F32),
        scratch_shapes=[pltpu.VMEM((W, W), F32), pltpu.VMEM((_hgrn2_groups() * SUB, W), F32)],
        compiler_params=pltpu.CompilerParams(
            dimension_semantics=("parallel", "parallel", "arbitrary")),
        name="hgrn2_scan",
    )(pb, pb, pb, lb.reshape(2, 1, W), tri, sel, bd)


HY_N2 = 128
HY_K1_TILE = 8
HY_FILT_ROWS = 512
HY_KRON_MAX_N1 = 32


def _dot_hi(a, b):
    return jnp.dot(a, b, precision=lax.Precision.HIGHEST, preferred_element_type=F32)


def _hy_short_kernel(pm_ref, pp_ref, pn_ref, w_ref, v_ref, x1_ref, x2_ref):
    i = pl.program_id(1)
    tm = pm_ref.shape[1]
    row = lax.broadcasted_iota(jnp.int32, (tm, 1), 0)
    u = pm_ref[0]
    u_prev = pp_ref[0, SUB - 1:SUB, :] * (i > 0).astype(F32)
    u_next = pn_ref[0, 0:1, :] * (i < pl.num_programs(1) - 1).astype(F32)
    um1 = jnp.where(row == 0, u_prev, pltpu.roll(u, 1, 0))
    up1 = jnp.where(row == tm - 1, u_next, pltpu.roll(u, tm - 1, 0))
    y = w_ref[0:1, :] * um1 + w_ref[1:2, :] * u + w_ref[2:3, :] * up1
    W = HY_WIDTH
    v_ref[0] = y[:, 0 * W:1 * W]
    x1_ref[0] = y[:, 1 * W:2 * W]
    x2_ref[0] = y[:, 2 * W:3 * W]


def _hy_short_conv(pc, short_w, tm=512):
    B, L, C3 = pc.shape
    tm = min(tm, L)
    nh = tm // SUB
    last = L // SUB - 1
    out = pl.BlockSpec((1, tm, HY_WIDTH), lambda b, i: (b, i, 0))
    return pl.pallas_call(
        _hy_short_kernel,
        grid=(B, L // tm),
        in_specs=[
            pl.BlockSpec((1, tm, C3), lambda b, i: (b, i, 0)),
            pl.BlockSpec((1, SUB, C3), lambda b, i: (b, jnp.maximum(i * nh - 1, 0), 0)),
            pl.BlockSpec((1, SUB, C3), lambda b, i: (b, jnp.minimum((i + 1) * nh, last), 0)),
            pl.BlockSpec((3, C3), lambda b, i: (0, 0)),
        ],
        out_specs=[out, out, out],
        out_shape=[jax.ShapeDtypeStruct((B, L, HY_WIDTH), F32)] * 3,
        compiler_params=pltpu.CompilerParams(dimension_semantics=("parallel", "parallel")),
        name="hyena_short_conv",
    )(pc, pc, pc, short_w)


def _hy_filter_kernel(w1t_ref, w1c_ref, w1s_ref, b1_ref, fr_ref, w2_ref, b2_ref, w3_ref, bands_ref,
                      dl_ref, k_ref, sum_ref, *, L):
    tr = k_ref.shape[0]
    r = pl.program_id(0) * tr + lax.broadcasted_iota(jnp.int32, (tr, 1), 0)
    first = r < L
    pos = jnp.where(first, r, 2 * L - r).astype(F32)
    t = pos * (1.0 / (L - 1))
    bw = (pos * (2.0 * math.pi / L)) * bands_ref[...]
    pre = t * w1t_ref[...] + _dot_hi(jnp.cos(bw), w1c_ref[...]) - _dot_hi(jnp.sin(bw), w1s_ref[...])
    fr = fr_ref[...]
    a = jnp.sin(fr * (pre + b1_ref[...]))
    a = jnp.sin(fr * (_dot_hi(a, w2_ref[...]) + b2_ref[...]))
    h = _dot_hi(a, w3_ref[...])
    decay = jnp.exp(-t * dl_ref[...])
    W2 = HY_ORDER * HY_WIDTH
    use_f = first.astype(F32)
    use_b = ((r > L) | (r == 0)).astype(F32)
    hf = h[:, :W2] * decay * use_f
    hb = h[:, W2:] * decay * use_b
    k_ref[...] = hf + hb

    @pl.when(pl.program_id(0) == 0)
    def _():
        sum_ref[...] = jnp.zeros_like(sum_ref)

    sum_ref[...] += jnp.sum(jnp.abs(hf) + jnp.abs(hb), axis=0, keepdims=True)


def _hy_filter(L, w1, b1, freq, w2, b2, w3):
    tr = min(HY_FILT_ROWS, L)
    W2 = HY_ORDER * HY_WIDTH
    bands = jnp.zeros((1, 128), F32).at[0, :HY_BANDS].set(jnp.linspace(1e-4, HY_BANDS - 1, HY_BANDS, dtype=F32))
    w1c = jnp.zeros((128, HY_HIDDEN), F32).at[:HY_BANDS].set(w1[1:1 + HY_BANDS])
    w1s = jnp.zeros((128, HY_HIDDEN), F32).at[:HY_BANDS].set(w1[1 + HY_BANDS:])
    deltas = jnp.abs(jnp.linspace(math.log(HY_DECAY_TARGET) / HY_SLOW_PCT,
                                  math.log(HY_DECAY_TARGET) / HY_FAST_PCT, HY_WIDTH, dtype=F32))
    dl = jnp.tile(deltas, HY_ORDER).reshape(1, W2)
    args = (w1[0:1], w1c, w1s, b1.reshape(1, -1), freq.reshape(1, -1), w2, b2.reshape(1, -1), w3, bands, dl)
    full = lambda a: pl.BlockSpec(a.shape, lambda i: (0, 0))
    return pl.pallas_call(
        functools.partial(_hy_filter_kernel, L=L),
        grid=(2 * L // tr,),
        in_specs=[full(a) for a in args],
        out_specs=[pl.BlockSpec((tr, W2), lambda i: (i, 0)), pl.BlockSpec((1, W2), lambda i: (0, 0))],
        out_shape=[jax.ShapeDtypeStruct((2 * L, W2), F32), jax.ShapeDtypeStruct((1, W2), F32)],
        compiler_params=pltpu.CompilerParams(dimension_semantics=("arbitrary",)),
        name="hyena_filter",
    )(*args)


def _hy_dft_tables(L):
    N = 2 * L
    N2 = HY_N2
    N1 = N // N2
    def cs(n, rows, cols):
        ang = (2.0 * math.pi / n) * ((jnp.arange(rows)[:, None] * jnp.arange(cols)[None, :]) % n).astype(F32)
        return jnp.cos(ang), jnp.sin(ang)
    ca, sa = cs(N1, N1, N1)
    fa_full = jnp.concatenate([ca, -sa], axis=0)
    fa_inv = jnp.concatenate([ca[:N1 // 2], -sa[:N1 // 2]], axis=1) * (1.0 / N)
    cb, sb = cs(N2, N2, N2)
    fb = jnp.concatenate([jnp.concatenate([cb, sb], axis=1), jnp.concatenate([-sb, cb], axis=1)], axis=0)
    tc, ts = cs(N, N1, N2)
    tw = jnp.stack([tc, ts])[..., None] * jnp.ones((1, 1, 1, 128), F32)
    tabs = dict(N1=N1, fa_half=fa_full[:, :N1 // 2].astype(BF16), fa_full=fa_full.astype(BF16),
                fa_inv=fa_inv.astype(BF16), fb=fb.astype(BF16), fb_t=fb.T.astype(BF16), tw=tw)
    if N1 <= HY_KRON_MAX_N1:
        eye = jnp.eye(SUB, dtype=F32)
        tabs['fa_half_kron'] = jnp.kron(fa_full[:, :N1 // 2], eye).astype(BF16)
        tabs['fa_inv_kron'] = jnp.kron(fa_inv, eye).astype(BF16)
    return tabs


def _hy_n2_rows(N1):
    return SUB * max(1, min(HY_N2 // SUB, 256 // N1))


def _hy_dft_a_kernel(m_ref, x_ref, o_ref):
    N1, C = o_ref.shape[2], o_ref.shape[4]
    for g0 in range(0, x_ref.shape[2], SUB):
        x = jnp.concatenate([x_ref[0, :, g0 + j, :] for j in range(SUB)], axis=1).astype(BF16)
        y = _dot(m_ref[...], x)
        for j in range(SUB):
            o_ref[0, 0, :, g0 + j, :] = y[:N1, j * C:(j + 1) * C]
            o_ref[0, 1, :, g0 + j, :] = y[N1:, j * C:(j + 1) * C]


def _hy_dft_a(m, x):
    B, R, N2, C = x.shape
    N1 = m.shape[0] // 2
    W = HY_WIDTH
    G = _hy_n2_rows(N1)
    return pl.pallas_call(
        _hy_dft_a_kernel,
        grid=(B, N2 // G, C // W),
        in_specs=[pl.BlockSpec(m.shape, lambda b, g, c: (0, 0)),
                  pl.BlockSpec((1, R, G, W), lambda b, g, c: (b, 0, g, c))],
        out_specs=pl.BlockSpec((1, 2, N1, G, W), lambda b, g, c: (b, 0, 0, g, c)),
        out_shape=jax.ShapeDtypeStruct((B, 2, N1, N2, C), F32),
        compiler_params=pltpu.CompilerParams(dimension_semantics=("parallel", "parallel", "parallel")),
        name="hyena_dft_a",
    )(m, x)


def _hy_gate_kernel(m_ref, y_ref, z_ref, g_ref, s_ref, o_ref):
    C = o_ref.shape[3]
    for g0 in range(0, o_ref.shape[2], SUB):
        y = jnp.concatenate(
            [jnp.concatenate([y_ref[0, 0, :, g0 + j, :], y_ref[0, 1, :, g0 + j, :]], axis=0) for j in range(SUB)],
            axis=1)
        zc = _dot(m_ref[...], y.astype(BF16))
        for j in range(SUB):
            o_ref[0, :, g0 + j, :] = g_ref[0, :, g0 + j, :] * (
                zc[:, j * C:(j + 1) * C] + s_ref[...] * z_ref[0, :, g0 + j, :])


def _hy_inverse_gate(m, y, z, gate, skip_row):
    B, _, N1, N2, C = y.shape
    G = _hy_n2_rows(N1)
    tok = pl.BlockSpec((1, N1 // 2, G, C), lambda b, g: (b, 0, g, 0))
    return pl.pallas_call(
        _hy_gate_kernel,
        grid=(B, N2 // G),
        in_specs=[pl.BlockSpec(m.shape, lambda b, g: (0, 0)),
                  pl.BlockSpec((1, 2, N1, G, C), lambda b, g: (b, 0, 0, g, 0)),
                  tok, tok,
                  pl.BlockSpec((1, C), lambda b, g: (0, 0))],
        out_specs=tok,
        out_shape=jax.ShapeDtypeStruct((B, N1 // 2, N2, C), F32),
        compiler_params=pltpu.CompilerParams(dimension_semantics=("parallel", "parallel")),
        name="hyena_idft_a_gate",
    )(m, y, z, gate, skip_row)


def _hy_dft_a_kron_kernel(m_ref, x_ref, o_ref):
    R, G, C = x_ref.shape[1:]
    N1 = o_ref.shape[2]
    for g0 in range(0, G, SUB):
        x = x_ref[0, :, g0:g0 + SUB, :].reshape(R * SUB, C).astype(BF16)
        y = _dot(m_ref[...], x).reshape(2, N1, SUB, C)
        o_ref[0, :, :, g0:g0 + SUB, :] = y


def _hy_dft_a_kron(m, x, N1):
    B, R, N2, C = x.shape
    G = _hy_n2_rows(N1)
    return pl.pallas_call(
        _hy_dft_a_kron_kernel,
        grid=(B, N2 // G),
        in_specs=[pl.BlockSpec(m.shape, lambda b, g: (0, 0)),
                  pl.BlockSpec((1, R, G, C), lambda b, g: (b, 0, g, 0))],
        out_specs=pl.BlockSpec((1, 2, N1, G, C), lambda b, g: (b, 0, 0, g, 0)),
        out_shape=jax.ShapeDtypeStruct((B, 2, N1, N2, C), F32),
        compiler_params=pltpu.CompilerParams(dimension_semantics=("parallel", "parallel")),
        name="hyena_dft_a_kron",
    )(m, x)


def _hy_gate_kron_kernel(m_ref, y_ref, z_ref, g_ref, s_ref, o_ref):
    N1, G, C = y_ref.shape[2:]
    for g0 in range(0, G, SUB):
        y = y_ref[0, :, :, g0:g0 + SUB, :].reshape(2 * N1 * SUB, C).astype(BF16)
        zc = _dot(m_ref[...], y).reshape(N1 // 2, SUB, C)
        rows = slice(g0, g0 + SUB)
        o_ref[0, :, rows, :] = g_ref[0, :, rows, :] * (zc + s_ref[...] * z_ref[0, :, rows, :])


def _hy_inverse_gate_kron(m, y, z, gate, skip_row):
    B, _, N1, N2, C = y.shape
    G = _hy_n2_rows(N1)
    tok = pl.BlockSpec((1, N1 // 2, G, C), lambda b, g: (b, 0, g, 0))
    return pl.pallas_call(
        _hy_gate_kron_kernel,
        grid=(B, N2 // G),
        in_specs=[pl.BlockSpec(m.shape, lambda b, g: (0, 0)),
                  pl.BlockSpec((1, 2, N1, G, C), lambda b, g: (b, 0, 0, g, 0)),
                  tok, tok,
                  pl.BlockSpec((1, C), lambda b, g: (0, 0))],
        out_specs=tok,
        out_shape=jax.ShapeDtypeStruct((B, N1 // 2, N2, C), F32),
        compiler_params=pltpu.CompilerParams(dimension_semantics=("parallel", "parallel")),
        name="hyena_idft_a_gate_kron",
    )(m, y, z, gate, skip_row)


def _cmul(ar, ai, br, bi):
    return ar * br - ai * bi, ar * bi + ai * br


def _hy_stage_b_kernel(a_ref, tw_ref, fb_ref, fbt_ref, k_ref, o_ref, *, conv):
    N2 = HY_N2
    for j in range(a_ref.shape[2]):
        tc = jnp.concatenate([tw_ref[0, j], tw_ref[0, j]], axis=1)
        ts = jnp.concatenate([tw_ref[1, j], tw_ref[1, j]], axis=1)
        br, bi = _cmul(a_ref[0, 0, j], a_ref[0, 1, j], tc, -ts)
        x = _dot(fb_ref[...], jnp.concatenate([br, bi], axis=0).astype(BF16))
        xr, xi = x[:N2], x[N2:]
        if conv:
            yr, yi = _cmul(xr, xi, k_ref[0, j], k_ref[1, j])
            y = _dot(fbt_ref[...], jnp.concatenate([yr, yi], axis=0).astype(BF16))
            xr, xi = _cmul(y[:N2], y[N2:], tc, ts)
        else:
            xr, xi = xr * k_ref[...], xi * k_ref[...]
        o_ref[0, 0, j] = xr
        o_ref[0, 1, j] = xi


def _hy_stage_b(a, tabs, kspec=None, order=0, scale=None):
    B, _, N1, N2, C = a.shape
    t1 = HY_K1_TILE
    conv = kspec is not None
    blk = pl.BlockSpec((1, 2, t1, N2, HY_WIDTH), lambda i, b, c: (b, 0, i, 0, c))
    if conv:
        k_arr = kspec
        k_spec = pl.BlockSpec((2, t1, N2, HY_WIDTH), lambda i, b, c: (0, i, 0, order))
    else:
        k_arr = scale
        k_spec = pl.BlockSpec((1, HY_WIDTH), lambda i, b, c: (0, c))
    return pl.pallas_call(
        functools.partial(_hy_stage_b_kernel, conv=conv),
        grid=(N1 // t1, B, C // HY_WIDTH),
        in_specs=[blk,
                  pl.BlockSpec((2, t1, N2, 128), lambda i, b, c: (0, i, 0, 0)),
                  pl.BlockSpec(tabs['fb'].shape, lambda i, b, c: (0, 0)),
                  pl.BlockSpec(tabs['fb_t'].shape, lambda i, b, c: (0, 0)),
                  k_spec],
        out_specs=blk,
        out_shape=jax.ShapeDtypeStruct(a.shape, F32),
        compiler_params=pltpu.CompilerParams(dimension_semantics=("parallel", "parallel", "parallel")),
        name="hyena_stage_b_conv" if conv else "hyena_stage_b_spectrum",
    )(a, tabs['tw'], tabs['fb'], tabs['fb_t'], k_arr)


def _hy_filter_spectrum(L, tabs, w1, b1, freq, w2, b2, w3):
    N1, N2 = tabs['N1'], HY_N2
    W2 = HY_ORDER * HY_WIDTH
    k, total = _hy_filter(L, w1, b1, freq, w2, b2, w3)
    a = _hy_dft_a(tabs['fa_full'], k.reshape(1, N1, N2, W2))
    return _hy_stage_b(a, tabs, scale=1.0 / total)[0]


def _hyena(pc, short_w, skip, tabs, kspec):
    B, L, _ = pc.shape
    N1, N2, C = tabs['N1'], HY_N2, HY_WIDTH
    v, x1, x2 = _hy_short_conv(pc, short_w)
    grid = lambda t: t.reshape(B, N1 // 2, N2, C)
    z = grid(v)
    for n, gate in enumerate((x1, x2)):
        kron = 'fa_half_kron' in tabs
        a = (_hy_dft_a_kron(tabs['fa_half_kron'], z, N1) if kron
             else _hy_dft_a(tabs['fa_half'], z))
        y = _hy_stage_b(a, tabs, kspec, order=n)
        inverse = _hy_inverse_gate_kron if kron else _hy_inverse_gate
        z = inverse(tabs['fa_inv_kron' if kron else 'fa_inv'], y, z, grid(gate), skip[n].reshape(1, C))
    return z.reshape(B, L, C)


def _prep_layer_weights(l, w_in, w_out, na_rpb, mla_w_uq, mla_w_ukv, ffn_w_up, ffn_conv, ffn_w_down):
    wi = w_in[l]
    s0, s1, s2 = NA_COLS, NA_COLS + HG_COLS, NA_COLS + HG_COLS + HY_COLS
    w_ml = jnp.pad(wi[:, s2:], ((0, 0), (0, MLA_COLS_PAD - MLA_COLS)))
    return dict(
        w_na=wi[:, :s0].astype(BF16), w_hg=wi[:, s0:s1].astype(BF16), w_hy=wi[:, s1:s2].astype(BF16),
        w_ml=w_ml.astype(BF16), w_out=w_out[l].astype(BF16), na_bias=_na_bias_table(na_rpb[l]),
        mla=_mla_weights(mla_w_uq[l], mla_w_ukv[l]),
        ffn=_ffn_weight_tiles(ffn_w_up[l], ffn_conv[l], ffn_w_down[l]),
    )


def _trunk_layer(x, mod, lb, g, lw, p, cos, sin, hy_tabs, hy_spec):
    pa, pb, pc, pd = _in_proj(x, mod, g, lw['w_na'], lw['w_hg'], lw['w_hy'], lw['w_ml'])
    o_na = _neighbourhood_attention(pa, lw['na_bias'])
    hg_dirs = _hgrn2_scan(pb, lb)
    o_hy = _hyena(pc, p['hy_short'], p['hy_skip'], hy_tabs, hy_spec)
    L = x.shape[1]
    q, k, v = _mla_prep(pd, cos[:L], sin[:L], p['mla_q_norm'], p['mla_kv_norm'], lw['mla'])
    o_ml = _mla_flash(q, k, v)
    x1, h2 = _out_proj(x, mod, g, o_na, hg_dirs, pb, p['hg_norm'], o_hy, o_ml, lw['w_out'])
    return _conv_ffn(h2, x1, mod, g, *lw['ffn'])


def kernel(x_prompt, x_sample, c_prompt, c_sample, ada_w, ada_b, norm_g, w_in, w_out, na_rpb, hg_lb,
           hg_norm, hy_short, hy_w1, hy_b1, hy_freq, hy_w2, hy_b2, hy_w3, hy_skip, mla_q_norm,
           mla_kv_norm, mla_w_uq, mla_w_ukv, ffn_w_up, ffn_conv, ffn_w_down):
    Bp, Bs = x_prompt.shape[0], x_sample.shape[0]
    Lp, Ls = x_prompt.shape[1], x_sample.shape[1]
    lb_soft = jax.nn.softmax(hg_lb.astype(F32), axis=0)
    lower_bounds = jnp.cumsum(lb_soft, axis=0) - lb_soft[:1]

    R = -(-(Bp + Bs) // 8) * 8
    cond = jnp.zeros((R, D_MODEL), F32).at[:Bp].set(c_prompt).at[Bp:Bp + Bs].set(c_sample)
    mod = _ada_modulation(cond, ada_w, ada_b).reshape(DEPTH, R, 6, D_MODEL)
    mod = jnp.pad(mod, ((0, 0), (0, 0), (0, 2), (0, 0)))

    cos, sin = _rope_tables(max(Lp, Ls))
    tabs_p, tabs_s = _hy_dft_tables(Lp), _hy_dft_tables(Ls)
    y_prompt, y_sample = x_prompt, x_sample
    for l in range(DEPTH):
        lw = _prep_layer_weights(l, w_in, w_out, na_rpb, mla_w_uq, mla_w_ukv, ffn_w_up, ffn_conv, ffn_w_down)
        p = dict(hg_norm=hg_norm[l], hy_short=hy_short[l], hy_skip=hy_skip[l],
                 mla_q_norm=mla_q_norm[l].reshape(1, -1), mla_kv_norm=mla_kv_norm[l].reshape(1, -1))
        filt = (hy_w1[l], hy_b1[l], hy_freq[l], hy_w2[l], hy_b2[l], hy_w3[l])
        spec_p = _hy_filter_spectrum(Lp, tabs_p, *filt)
        spec_s = _hy_filter_spectrum(Ls, tabs_s, *filt)
        y_prompt = _trunk_layer(y_prompt, mod[l, :Bp], lower_bounds[l], norm_g[l], lw, p, cos, sin,
                                tabs_p, spec_p)
        y_sample = _trunk_layer(y_sample, mod[l, Bp:Bp + Bs], lower_bounds[l], norm_g[l], lw, p, cos, sin,
                                tabs_s, spec_s)
    return (y_prompt, y_sample)
```

```python
import functools
import math

import jax
import jax.numpy as jnp
from jax import lax
from jax.experimental import pallas as pl
from jax.experimental.pallas import tpu as pltpu

F32 = jnp.float32
BF16 = jnp.bfloat16

D_MODEL = 1024
DEPTH = 4
GRID_W = 64
HEAD_DIM = 64
NA_WIDTH = 256
NA_HEADS = 4
NA_KH = 8
NA_KW = 16
HG_WIDTH = 256
HG_HEADS = 4
HG_CHUNK = 64
HY_WIDTH = 256
HY_ORDER = 2
HY_EMB = 33
HY_BANDS = 16
HY_HIDDEN = 64
HY_DECAY_TARGET = 1e-2
HY_FAST_PCT = 0.3
HY_SLOW_PCT = 1.5
MLA_WIDTH = 256
MLA_HEADS = 4
MLA_NOPE = 64
MLA_ROPE = 32
MLA_V = 64
MLA_Q_LORA = 256
MLA_KV_LORA = 128
MLA_HEAD_PAD = 128
ROPE_THETA = 10000.0
D_FF = 2816
EPS = 1e-6
NA_COLS = 3 * NA_WIDTH
HG_COLS = 5 * HG_WIDTH
HY_COLS = 3 * HY_WIDTH
MLA_COLS = MLA_Q_LORA + MLA_KV_LORA + MLA_ROPE
MLA_COLS_PAD = 512
NEG_BIG = -1e30
SUB = 8
LANES = 128


def _dot(a, b):
    return jnp.dot(a, b, preferred_element_type=F32)


def _dot_nt(a, b):
    return lax.dot_general(a, b, (((1,), (1,)), ((), ())), preferred_element_type=F32)


def _rms(x, g):
    return x * lax.rsqrt(jnp.mean(x * x, axis=-1, keepdims=True) + EPS) * g


def _ada_kernel(c_ref, w_ref, b_ref, o_ref):
    c = c_ref[...]
    s = c * jax.nn.sigmoid(c)
    o_ref[0] = _dot(s.astype(BF16), w_ref[0].astype(BF16)) + b_ref[0]


def _ada_modulation(cond, ada_w, ada_b):
    R = cond.shape[0]
    tn = 1536
    return pl.pallas_call(
        _ada_kernel,
        grid=(DEPTH, 6 * D_MODEL // tn),
        in_specs=[
            pl.BlockSpec((R, D_MODEL), lambda l, n: (0, 0)),
            pl.BlockSpec((1, D_MODEL, tn), lambda l, n: (l, 0, n)),
            pl.BlockSpec((1, 1, tn), lambda l, n: (l, 0, n)),
        ],
        out_specs=pl.BlockSpec((1, R, tn), lambda l, n: (l, 0, n)),
        out_shape=jax.ShapeDtypeStruct((DEPTH, R, 6 * D_MODEL), F32),
        name="ada_modulation",
    )(cond, ada_w, ada_b.reshape(DEPTH, 1, 6 * D_MODEL))


IN_HALO = 16


def _in_proj_kernel(x_ref, xp_ref, xn_ref, mod_ref, g_ref, wna, whg, why, wml, sw_ref,
                    ona, ohg, ov, ox1, ox2, oml, hext_ref):
    i = pl.program_id(1)
    tm = x_ref.shape[1]
    H = IN_HALO
    g = g_ref[0:1, :]
    scale = 1.0 + mod_ref[0, 1:2, :]
    shift = mod_ref[0, 0:1, :]
    prep = lambda x: (_rms(x, g) * scale + shift).astype(BF16)
    hext_ref[0:H, :] = prep(xp_ref[0])
    hext_ref[H:H + tm, :] = prep(x_ref[0])
    hext_ref[H + tm:2 * H + tm, :] = prep(xn_ref[0])
    hb = hext_ref[H:H + tm, :]
    ona[0] = _dot(hb, wna[...])
    ohg[0] = _dot(hb, whg[...])
    oml[0] = _dot(hb, wml[...])
    ue = _dot(hext_ref[...], why[...])
    u = ue[H:H + tm]
    u_prev = ue[H - 1:H] * (i > 0).astype(F32)
    u_next = ue[H + tm:H + tm + 1] * (i < pl.num_programs(1) - 1).astype(F32)
    sub = lax.broadcasted_iota(jnp.int32, (SUB, 1), 0)
    down = pltpu.roll(u, 1, 0)
    up = pltpu.roll(u, tm - 1, 0)
    down = jnp.concatenate([jnp.where(sub == 0, u_prev, down[:SUB]), down[SUB:]], axis=0)
    up = jnp.concatenate([up[:tm - SUB], jnp.where(sub == SUB - 1, u_next, up[tm - SUB:])], axis=0)
    y = sw_ref[0:1, :] * down + sw_ref[1:2, :] * u + sw_ref[2:3, :] * up
    W = HY_WIDTH
    ov[0] = y[:, 0 * W:1 * W]
    ox1[0] = y[:, 1 * W:2 * W]
    ox2[0] = y[:, 2 * W:3 * W]


def _in_proj(x, mod, g, w_na, w_hg, w_hy, w_ml, short_w, tm=512):
    B, L, D = x.shape
    tm = min(tm, L)
    nh = tm // IN_HALO
    last = L // IN_HALO - 1
    full = lambda a: pl.BlockSpec(a.shape, lambda b, i: (0, 0))
    outs = [NA_COLS, HG_COLS, HY_WIDTH, HY_WIDTH, HY_WIDTH, MLA_COLS_PAD]
    return pl.pallas_call(
        _in_proj_kernel,
        grid=(B, L // tm),
        in_specs=[
            pl.BlockSpec((1, tm, D), lambda b, i: (b, i, 0)),
            pl.BlockSpec((1, IN_HALO, D), lambda b, i: (b, jnp.maximum(i * nh - 1, 0), 0)),
            pl.BlockSpec((1, IN_HALO, D), lambda b, i: (b, jnp.minimum((i + 1) * nh, last), 0)),
            pl.BlockSpec((1, 8, D), lambda b, i: (b, 0, 0)),
            full(g), full(w_na), full(w_hg), full(w_hy), full(w_ml), full(short_w),
        ],
        out_specs=[pl.BlockSpec((1, tm, n), lambda b, i: (b, i, 0)) for n in outs],
        out_shape=[jax.ShapeDtypeStruct((B, L, n), F32) for n in outs],
        scratch_shapes=[pltpu.VMEM((tm + 2 * IN_HALO, D), BF16)],
        compiler_params=pltpu.CompilerParams(dimension_semantics=("parallel", "parallel")),
        name="in_proj",
    )(x, x, x, mod, g, w_na, w_hg, w_hy, w_ml, short_w)


def _head_mean_sq(o, sel):
    sq = o * o
    hi = sq.astype(BF16)
    lo = (sq - hi.astype(F32)).astype(BF16)
    return (_dot(hi, sel) + _dot(lo, sel)) * (1.0 / (HG_WIDTH // HG_HEADS))


def _out_proj_kernel(x_ref, mod_ref, g_ref, na, hgf, hgb, hgg, hgn, sel, hy, ml, w_ref, x1_ref, h2_ref):
    W = NA_WIDTH
    o = hgf[0, 0] + hgb[0, 0]
    gate = hgg[0]
    hg = o * lax.rsqrt(_head_mean_sq(o, sel[...]) + EPS) * hgn[...] * (gate * jax.nn.sigmoid(gate))
    mix = _dot(na[0].astype(BF16), w_ref[0 * W:1 * W, :])
    mix += _dot(hg.astype(BF16), w_ref[1 * W:2 * W, :])
    mix += _dot(hy[0].astype(BF16), w_ref[2 * W:3 * W, :])
    mix += _dot(ml[0].astype(BF16), w_ref[3 * W:4 * W, :])
    x1 = x_ref[0] + mod_ref[0, 2:3, :] * _rms(mix, g_ref[1:2, :])
    x1_ref[0] = x1
    h2 = _rms(x1, g_ref[2:3, :]) * (1.0 + mod_ref[0, 4:5, :]) + mod_ref[0, 3:4, :]
    h2_ref[0] = h2.astype(BF16)


def _out_proj(x, mod, g, o_na, hg_dirs, pb, hg_norm, o_hy, o_ml, w_out, tm=512):
    B, L, D = x.shape
    tm = min(tm, L)
    tok = lambda n: pl.BlockSpec((1, tm, n), lambda b, i: (b, i, 0))
    sel = _hgrn2_tables()[1]
    return pl.pallas_call(
        _out_proj_kernel,
        grid=(B, L // tm),
        in_specs=[
            tok(D),
            pl.BlockSpec((1, 8, D), lambda b, i: (b, 0, 0)),
            pl.BlockSpec(g.shape, lambda b, i: (0, 0)),
            tok(NA_WIDTH),
            pl.BlockSpec((1, 1, tm, HG_WIDTH), lambda b, i: (b, 0, i, 0)),
            pl.BlockSpec((1, 1, tm, HG_WIDTH), lambda b, i: (b, 1, i, 0)),
            pl.BlockSpec((1, tm, HG_WIDTH), lambda b, i: (b, i, 4)),
            pl.BlockSpec((1, HG_WIDTH), lambda b, i: (0, 0)),
            pl.BlockSpec(sel.shape, lambda b, i: (0, 0)),
            tok(HY_WIDTH), tok(MLA_WIDTH),
            pl.BlockSpec(w_out.shape, lambda b, i: (0, 0)),
        ],
        out_specs=[tok(D), tok(D)],
        out_shape=[jax.ShapeDtypeStruct((B, L, D), F32), jax.ShapeDtypeStruct((B, L, D), BF16)],
        compiler_params=pltpu.CompilerParams(dimension_semantics=("parallel", "parallel")),
        name="out_proj",
    )(x, mod, g, o_na, hg_dirs, hg_dirs, pb, hg_norm.reshape(1, HG_WIDTH), sel, o_hy, o_ml, w_out)


FFN_HALO = 16


def _ffn_kernel(hm_ref, hp_ref, hn_ref, x_ref, mod_ref, g_ref, wu_ref, cw_ref, wd_ref, o_ref, acc_ref,
                hext_ref):
    i = pl.program_id(1)
    tm = hm_ref.shape[1]
    nf = wd_ref.shape[0]
    not_first = (i > 0).astype(F32)
    not_last = (i < pl.num_programs(1) - 1).astype(F32)
    sub = lax.broadcasted_iota(jnp.int32, (SUB, 1), 0)

    H = FFN_HALO
    hext_ref[0:H, :] = hp_ref[0]
    hext_ref[H:H + tm, :] = hm_ref[0]
    hext_ref[H + tm:2 * H + tm, :] = hn_ref[0]

    def conv_branch(t):
        c = cw_ref[t]
        ue = _dot(hext_ref[...], wu_ref[t])
        u = ue[H:H + tm]
        u_prev = ue[H - 1:H] * not_first
        u_next = ue[H + tm:H + tm + 1] * not_last
        down = pltpu.roll(u, 1, 0)
        up = pltpu.roll(u, tm - 1, 0)
        down = jnp.concatenate([jnp.where(sub == 0, u_prev, down[:SUB]), down[SUB:]], axis=0)
        up = jnp.concatenate([up[:tm - SUB], jnp.where(sub == SUB - 1, u_next, up[tm - SUB:])], axis=0)
        return c[0:1, :] * down + c[1:2, :] * u + c[2:3, :] * up

    def tile(f, carry):
        a = conv_branch(f)
        b = conv_branch(nf + f)
        act = (jax.nn.gelu(a, approximate=True) * b).astype(BF16)
        acc_ref[...] += _dot(act, wd_ref[f])
        return carry

    acc_ref[...] = jnp.zeros_like(acc_ref)
    lax.fori_loop(0, nf, tile, 0, unroll=FFN_UNROLL)
    o_ref[0] = x_ref[0] + mod_ref[0, 5:6, :] * _rms(acc_ref[...], g_ref[3:4, :])


FFN_TF = 256
FFN_UNROLL = 2


def _conv_ffn(h2, x1, mod, g, w_up, conv_w, w_down, tm=1024):
    B, L, D = x1.shape
    tm = min(tm, L)
    nh = tm // FFN_HALO
    last_halo = L // FFN_HALO - 1
    once = lambda a: pl.BlockSpec(a.shape, lambda b, i: (0,) * a.ndim, pipeline_mode=pl.Buffered(1))
    return pl.pallas_call(
        _ffn_kernel,
        grid=(B, L // tm),
        in_specs=[
            pl.BlockSpec((1, tm, D), lambda b, i: (b, i, 0)),
            pl.BlockSpec((1, FFN_HALO, D), lambda b, i: (b, jnp.maximum(i * nh - 1, 0), 0)),
            pl.BlockSpec((1, FFN_HALO, D), lambda b, i: (b, jnp.minimum((i + 1) * nh, last_halo), 0)),
            pl.BlockSpec((1, tm, D), lambda b, i: (b, i, 0)),
            pl.BlockSpec((1, 8, D), lambda b, i: (b, 0, 0)),
            pl.BlockSpec(g.shape, lambda b, i: (0, 0)),
            once(w_up), once(conv_w), once(w_down),
        ],
        out_specs=pl.BlockSpec((1, tm, D), lambda b, i: (b, i, 0)),
        out_shape=jax.ShapeDtypeStruct((B, L, D), F32),
        scratch_shapes=[pltpu.VMEM((tm, D), F32), pltpu.VMEM((tm + 2 * FFN_HALO, D), BF16)],
        compiler_params=pltpu.CompilerParams(dimension_semantics=("parallel", "parallel")),
        name="conv_ffn",
    )(h2, h2, h2, x1, mod, g, w_up, conv_w, w_down)


def _ffn_weight_tiles(w_up, conv_w, w_down):
    D = w_up.shape[0]
    n2 = 2 * D_FF // FFN_TF
    wu = w_up.reshape(D, n2, FFN_TF).transpose(1, 0, 2).astype(BF16)
    cw = conv_w.reshape(3, n2, FFN_TF).transpose(1, 0, 2)
    wd = w_down.reshape(D_FF // FFN_TF, FFN_TF, D).astype(BF16)
    return wu, cw, wd


NA_RB = 8


def _na_kernel(q_ref, kp_ref, kc_ref, kn_ref, vp_ref, vc_ref, vn_ref, bias_ref, o_ref,
               kbuf, vbuf, *, rows):
    i = pl.program_id(1)
    blk = NA_RB * GRID_W
    win = NA_KH * GRID_W
    kbuf[0 * blk:1 * blk, :] = kp_ref[0].astype(BF16)
    kbuf[1 * blk:2 * blk, :] = kc_ref[0].astype(BF16)
    kbuf[2 * blk:3 * blk, :] = kn_ref[0].astype(BF16)
    vbuf[0 * blk:1 * blk, :] = vp_ref[0].astype(BF16)
    vbuf[1 * blk:2 * blk, :] = vc_ref[0].astype(BF16)
    vbuf[2 * blk:3 * blk, :] = vn_ref[0].astype(BF16)
    scale = HEAD_DIM ** -0.5 * math.log2(math.e)
    lane_head = lax.broadcasted_iota(jnp.int32, (1, NA_WIDTH), 1) // HEAD_DIM
    for j in range(NA_RB):
        r = i * NA_RB + j
        start = jnp.clip(r - NA_KH // 2, 0, rows - NA_KH)
        loc = start - (i - 1) * NA_RB
        dr0 = start - r + (NA_KH - 1)
        off = pl.multiple_of(loc * GRID_W, GRID_W)
        kw = kbuf[pl.ds(off, win), :]
        vw = vbuf[pl.ds(off, win), :]
        qj = (q_ref[0, j * GRID_W:(j + 1) * GRID_W, :] * scale).astype(BF16)
        qbd = jnp.concatenate([jnp.where(lane_head == h, qj, jnp.zeros_like(qj)) for h in range(NA_HEADS)],
                              axis=0)
        s = _dot_nt(qbd, kw) + bias_ref[dr0]
        m = jnp.max(s, axis=-1, keepdims=True)
        p = jnp.exp2(s - m)
        l = jnp.sum(p, axis=-1, keepdims=True)
        pv = _dot(p.astype(BF16), vw) / l
        o = None
        for h in range(NA_HEADS):
            part = jnp.where(lane_head == h, pv[h * GRID_W:(h + 1) * GRID_W], 0.0)
            o = part if o is None else o + part
        o_ref[0, j * GRID_W:(j + 1) * GRID_W, :] = o


def _na_bias_table(rpb):
    c = jnp.arange(GRID_W)
    dc = jnp.clip(c[None, :] - c[:, None] + (NA_KW - 1), 0, 2 * NA_KW - 2)
    col_start = jnp.clip(c - NA_KW // 2, 0, GRID_W - NA_KW)
    ok = (c[None, :] >= col_start[:, None]) & (c[None, :] < col_start[:, None] + NA_KW)
    e = jnp.where(ok[None, None], rpb.astype(F32)[:, :, dc], NEG_BIG)
    t = jnp.stack([e[:, d0:d0 + NA_KH] for d0 in range(NA_KH)], axis=1)
    t = t.transpose(1, 0, 3, 2, 4).reshape(NA_KH, NA_HEADS * GRID_W, NA_KH * GRID_W)
    return jnp.where(t > 0.5 * NEG_BIG, t * math.log2(math.e), NEG_BIG)


def _neighbourhood_attention(pa, bias):
    B, L, _ = pa.shape
    rows = L // GRID_W
    blk = NA_RB * GRID_W
    nblk = rows // NA_RB
    spec = lambda col, shift: pl.BlockSpec(
        (1, blk, NA_WIDTH), lambda b, i: (b, jnp.clip(i + shift, 0, nblk - 1), col))
    return pl.pallas_call(
        functools.partial(_na_kernel, rows=rows),
        grid=(B, nblk),
        in_specs=[spec(0, 0), spec(1, -1), spec(1, 0), spec(1, 1), spec(2, -1), spec(2, 0), spec(2, 1),
                  pl.BlockSpec(bias.shape, lambda b, i: (0, 0, 0))],
        out_specs=pl.BlockSpec((1, blk, NA_WIDTH), lambda b, i: (b, i, 0)),
        out_shape=jax.ShapeDtypeStruct((B, L, NA_WIDTH), F32),
        scratch_shapes=[pltpu.VMEM((3 * blk, NA_WIDTH), BF16), pltpu.VMEM((3 * blk, NA_WIDTH), BF16)],
        compiler_params=pltpu.CompilerParams(dimension_semantics=("parallel", "parallel")),
        name="neighbourhood_attention",
    )(pa, pa, pa, pa, pa, pa, pa, bias)


def _rope_table_kernel(inv_ref, cos_ref, sin_ref):
    t = cos_ref.shape[0]
    pos = (pl.program_id(0) * t + lax.broadcasted_iota(jnp.int32, (t, MLA_HEAD_PAD), 0)).astype(F32)
    ang = pos * inv_ref[...]
    cos_ref[...] = jnp.cos(ang)
    sin_ref[...] = jnp.sin(ang)


def _rope_tables(L):
    half = MLA_ROPE // 2
    inv = ROPE_THETA ** (-jnp.arange(half, dtype=F32) / half)
    inv_row = jnp.zeros((1, MLA_HEAD_PAD), F32).at[0, MLA_NOPE:MLA_NOPE + MLA_ROPE].set(jnp.tile(inv, 2))
    t = min(L, 1024)
    return pl.pallas_call(
        _rope_table_kernel,
        grid=(L // t,),
        in_specs=[pl.BlockSpec((1, MLA_HEAD_PAD), lambda i: (0, 0))],
        out_specs=[pl.BlockSpec((t, MLA_HEAD_PAD), lambda i: (i, 0))] * 2,
        out_shape=[jax.ShapeDtypeStruct((L, MLA_HEAD_PAD), F32)] * 2,
        name="rope_tables",
    )(inv_row)


def _mla_prep_kernel(pd_ref, cos_ref, sin_ref, qn_ref, kvn_ref, wq, wqr, wk, wv, we, wer,
                     q_out, k_out, v_out):
    pd = pd_ref[0]
    cos = cos_ref[...]
    sin = sin_ref[...]
    nq = _rms(pd[:, :MLA_Q_LORA], qn_ref[...]).astype(BF16)
    nkv = _rms(pd[:, MLA_Q_LORA:MLA_Q_LORA + MLA_KV_LORA], kvn_ref[...]).astype(BF16)
    kr = pd[:, MLA_Q_LORA + MLA_KV_LORA:].astype(BF16)
    q = _dot(nq, wq[...])
    q_rot = _dot(nq, wqr[...])
    k = _dot(nkv, wk[...]) + _dot(kr, we[...])
    k_rot = _dot(kr, wer[...])
    scale = (MLA_NOPE + MLA_ROPE) ** -0.5 * math.log2(math.e)
    for h in range(MLA_HEADS):
        hs = slice(h * MLA_HEAD_PAD, (h + 1) * MLA_HEAD_PAD)
        q_out[0, :, hs] = ((q[:, hs] * cos + q_rot[:, hs] * sin) * scale).astype(BF16)
        k_out[0, :, hs] = (k[:, hs] * cos + k_rot[:, hs] * sin).astype(BF16)
    v_out[0] = _dot(nkv, wv[...]).astype(BF16)


def _mla_weights(w_uq, w_ukv):
    half = MLA_ROPE // 2
    P = jnp.zeros((MLA_ROPE, MLA_ROPE), F32)
    P = P.at[jnp.arange(half) + half, jnp.arange(half)].set(-1.0)
    P = P.at[jnp.arange(half), jnp.arange(half) + half].set(1.0)
    HP = MLA_HEAD_PAD
    wq = jnp.zeros((MLA_Q_LORA, MLA_HEADS * HP), F32)
    wqr = jnp.zeros_like(wq)
    wk = jnp.zeros((MLA_KV_LORA, MLA_HEADS * HP), F32)
    wv = jnp.zeros((MLA_KV_LORA, MLA_HEADS * MLA_V), F32)
    we = jnp.zeros((MLA_COLS_PAD - MLA_Q_LORA - MLA_KV_LORA, MLA_HEADS * HP), F32)
    wer = jnp.zeros_like(we)
    eye = jnp.eye(MLA_ROPE, dtype=F32)
    for h in range(MLA_HEADS):
        qh = w_uq[:, h * (MLA_NOPE + MLA_ROPE):(h + 1) * (MLA_NOPE + MLA_ROPE)]
        wq = wq.at[:, h * HP:h * HP + MLA_NOPE + MLA_ROPE].set(qh)
        wqr = wqr.at[:, h * HP + MLA_NOPE:h * HP + MLA_NOPE + MLA_ROPE].set(qh[:, MLA_NOPE:] @ P)
        kvh = w_ukv[:, h * (MLA_NOPE + MLA_V):(h + 1) * (MLA_NOPE + MLA_V)]
        wk = wk.at[:, h * HP:h * HP + MLA_NOPE].set(kvh[:, :MLA_NOPE])
        wv = wv.at[:, h * MLA_V:(h + 1) * MLA_V].set(kvh[:, MLA_NOPE:])
        we = we.at[:MLA_ROPE, h * HP + MLA_NOPE:h * HP + MLA_NOPE + MLA_ROPE].set(eye)
        wer = wer.at[:MLA_ROPE, h * HP + MLA_NOPE:h * HP + MLA_NOPE + MLA_ROPE].set(P)
    return tuple(w.astype(BF16) for w in (wq, wqr, wk, wv, we, wer))


def _mla_prep(pd, cos, sin, q_norm, kv_norm, weights, tm=512):
    B, L, _ = pd.shape
    tm = min(tm, L)
    full = lambda a: pl.BlockSpec(a.shape, lambda b, i: (0, 0))
    HP = MLA_HEADS * MLA_HEAD_PAD
    return pl.pallas_call(
        _mla_prep_kernel,
        grid=(B, L // tm),
        in_specs=[
            pl.BlockSpec((1, tm, MLA_COLS_PAD), lambda b, i: (b, i, 0)),
            pl.BlockSpec((tm, MLA_HEAD_PAD), lambda b, i: (i, 0)),
            pl.BlockSpec((tm, MLA_HEAD_PAD), lambda b, i: (i, 0)),
            full(q_norm), full(kv_norm),
        ] + [full(w) for w in weights],
        out_specs=[pl.BlockSpec((1, tm, HP), lambda b, i: (b, i, 0)),
                   pl.BlockSpec((1, tm, HP), lambda b, i: (b, i, 0)),
                   pl.BlockSpec((1, tm, MLA_WIDTH), lambda b, i: (b, i, 0))],
        out_shape=[jax.ShapeDtypeStruct((B, L, HP), BF16), jax.ShapeDtypeStruct((B, L, HP), BF16),
                   jax.ShapeDtypeStruct((B, L, MLA_WIDTH), BF16)],
        compiler_params=pltpu.CompilerParams(dimension_semantics=("parallel", "parallel")),
        name="mla_prep",
    )(pd, cos, sin, q_norm, kv_norm, *weights)


FLASH_ROWS = 16
FLASH_HEADS = 4


def _flash_kernel(q_ref, k_ref, v_ref, o_ref, m_sc, acc_sc, p_sc):
    kv = pl.program_id(3)
    _, tq, tk = p_sc.shape
    RB = FLASH_ROWS

    @pl.when(kv == 0)
    def _():
        m_sc[...] = jnp.full_like(m_sc, -jnp.inf)
        acc_sc[...] = jnp.zeros_like(acc_sc)

    first_head = lax.broadcasted_iota(jnp.int32, (1, LANES), 1) < MLA_V
    for h in range(FLASH_HEADS):
        q = q_ref[0, :, h * MLA_HEAD_PAD:(h + 1) * MLA_HEAD_PAD]
        k = k_ref[0, :, h * MLA_HEAD_PAD:(h + 1) * MLA_HEAD_PAD]
        s = _dot_nt(q, k)
        alphas = []
        for r in range(tq // RB):
            rows = slice(r * RB, (r + 1) * RB)
            x = s[rows, 0:LANES]
            for c in range(1, tk // LANES):
                x = jnp.maximum(x, s[rows, c * LANES:(c + 1) * LANES])
            m_prev = m_sc[h, rows, :]
            m_new = jnp.maximum(m_prev, jnp.max(x, axis=-1, keepdims=True))
            alphas.append(jnp.exp2(m_prev - m_new))
            m_sc[h, rows, :] = m_new
            for c in range(tk // LANES):
                cols = slice(c * LANES, (c + 1) * LANES)
                p_sc[h, rows, cols] = jnp.exp2((s[rows, cols] - m_new).astype(BF16))
        v = v_ref[0, :, (h // 2) * LANES:(h // 2 + 1) * LANES]
        ones = jnp.ones_like(v)
        v_aug = jnp.where(first_head, v, ones) if h % 2 == 0 else jnp.where(first_head, ones, v)
        acc_sc[h] = jnp.concatenate(alphas, axis=0) * acc_sc[h] + _dot(p_sc[h], v_aug)

    @pl.when(kv == pl.num_programs(3) - 1)
    def _():
        for g in range(FLASH_HEADS // 2):
            a0 = acc_sc[2 * g]
            a1 = acc_sc[2 * g + 1]
            num = jnp.where(first_head, a0, a1)
            den = jnp.where(first_head, pltpu.roll(a0, MLA_V, 1), pltpu.roll(a1, MLA_V, 1))
            o_ref[0, :, g * LANES:(g + 1) * LANES] = num / den


def _mla_flash(q, k, v, tq=512, tk=2048):
    B, L, _ = q.shape
    tq = min(tq, L)
    tk = min(tk, L)
    nh = FLASH_HEADS
    stat = pltpu.VMEM((nh, tq, LANES), F32)
    return pl.pallas_call(
        _flash_kernel,
        grid=(B, MLA_HEADS // nh, L // tq, L // tk),
        in_specs=[
            pl.BlockSpec((1, tq, nh * MLA_HEAD_PAD), lambda b, h, i, j: (b, i, h)),
            pl.BlockSpec((1, tk, nh * MLA_HEAD_PAD), lambda b, h, i, j: (b, j, h)),
            pl.BlockSpec((1, tk, nh * MLA_V), lambda b, h, i, j: (b, j, h)),
        ],
        out_specs=pl.BlockSpec((1, tq, nh * MLA_V), lambda b, h, i, j: (b, i, h)),
        out_shape=jax.ShapeDtypeStruct((B, L, MLA_WIDTH), F32),
        scratch_shapes=[stat, stat, pltpu.VMEM((nh, tq, tk), BF16)],
        compiler_params=pltpu.CompilerParams(
            dimension_semantics=("parallel", "parallel", "parallel", "arbitrary")),
        name="mla_flash",
    )(q, k, v)


HG_T = 32
HG_TB = 512


def _hgrn2_kernel(q_ref, f_ref, i_ref, lb_ref, tri_ref, sel_ref, bd_ref, o_ref, st_ref, g_ref, *, nsub):
    z = pl.program_id(1)
    T = HG_T
    ngrp = T // SUB

    @pl.when(pl.program_id(2) == 0)
    def _():
        st_ref[...] = jnp.zeros_like(st_ref)

    lb = lb_ref[0]
    sel = sel_ref[...]
    row = lax.broadcasted_iota(jnp.int32, (SUB, HG_WIDTH), 0)

    def step(c, backward):
        r0 = pl.multiple_of(c * T, T)
        qs = q_ref[0, pl.ds(r0, T), :]
        qs = qs * jax.nn.sigmoid(qs)
        fg = lb + (1.0 - lb) * jax.nn.sigmoid(f_ref[0, pl.ds(r0, T), :])
        kk = 1.0 - fg
        vv = i_ref[0, pl.ds(r0, T), :]
        b = _split_dot_rhs(tri_ref[0], jnp.log(fg) * math.log2(math.e))
        edge = b[0:1, :] if backward else b[T - 1:T, :]
        st = st_ref[...]
        o_inter = _dot_nt((qs * jnp.exp2(b)).astype(BF16), st.astype(BF16))
        off = 0
        offs = {}
        for s in range(T):
            gs = s // SUB
            groups = range(0, gs + 1) if backward else range(gs, ngrp)
            bs = b[s:s + 1, :]
            ks = kk[s:s + 1, :]
            for gidx in groups:
                rs = slice(gidx * SUB, (gidx + 1) * SUB)
                gval = qs[rs] * jnp.exp2(b[rs] - bs) * ks
                if gidx == gs:
                    keep = (row <= s - gs * SUB) if backward else (row >= s - gs * SUB)
                    gval = jnp.where(keep, gval, 0.0)
                g_ref[off:off + SUB, :] = gval
                offs[(s, gidx)] = off
                off += SUB
        red = _dot(g_ref[...].astype(BF16), sel)
        outs = []
        for gidx in range(ngrp):
            acc = o_inter[gidx * SUB:(gidx + 1) * SUB]
            for s in range(T):
                if (s, gidx) in offs:
                    o0 = offs[(s, gidx)]
                    acc = acc + red[o0:o0 + SUB] * vv[s:s + 1, :]
            outs.append(acc)
        o_ref[0, 0, pl.ds(r0, T), :] = jnp.concatenate(outs, axis=0)
        kt = (kk * jnp.exp2(edge - b)).astype(BF16)
        upd = lax.dot_general(vv.astype(BF16), kt, (((0,), (0,)), ((), ())), preferred_element_type=F32)
        st_ref[...] = st * jnp.exp2(edge) + upd * bd_ref[...]

    @pl.when(z == 0)
    def _():
        lax.fori_loop(0, nsub, lambda c, _: (step(c, False), 0)[1], 0, unroll=4)

    @pl.when(z == 1)
    def _():
        lax.fori_loop(0, nsub, lambda c, _: (step(nsub - 1 - c, True), 0)[1], 0, unroll=4)


def _split_dot_rhs(tri, x):
    tb = tri.astype(BF16)
    acc = None
    for _ in range(3):
        hi = x.astype(BF16)
        part = _dot(tb, hi)
        acc = part if acc is None else acc + part
        x = x - hi.astype(F32)
    return acc


def _hgrn2_tables():
    T = HG_T
    r = jnp.arange(T)
    tri = jnp.stack([(r[None, :] <= r[:, None]), (r[None, :] >= r[:, None])]).astype(F32)
    head = jnp.arange(HG_WIDTH) // (HG_WIDTH // HG_HEADS)
    same = (head[:, None] == head[None, :])
    return tri, same.astype(BF16), same.astype(F32)


def _hgrn2_groups():
    T, ngrp = HG_T, HG_T // SUB
    return sum(ngrp - s // SUB for s in range(T))


def _hgrn2_scan(pb, lb):
    B, L, _ = pb.shape
    tb = min(HG_TB, L)
    nblk = L // tb
    tri, sel, bd = _hgrn2_tables()
    W = HG_WIDTH
    blk = lambda z, j: j + z * (nblk - 1 - 2 * j)
    return pl.pallas_call(
        functools.partial(_hgrn2_kernel, nsub=tb // HG_T),
        grid=(B, 2, nblk),
        in_specs=[
            pl.BlockSpec((1, tb, W), lambda b, z, j: (b, blk(z, j), 0)),
            pl.BlockSpec((1, tb, W), lambda b, z, j: (b, blk(z, j), 1 + z)),
            pl.BlockSpec((1, tb, W), lambda b, z, j: (b, blk(z, j), 3)),
            pl.BlockSpec((1, 1, W), lambda b, z, j: (z, 0, 0)),
            pl.BlockSpec((1, HG_T, HG_T), lambda b, z, j: (z, 0, 0)),
            pl.BlockSpec((W, W), lambda b, z, j: (0, 0)),
            pl.BlockSpec((W, W), lambda b, z, j: (0, 0)),
        ],
        out_specs=pl.BlockSpec((1, 1, tb, W), lambda b, z, j: (b, z, blk(z, j), 0)),
        out_shape=jax.ShapeDtypeStruct((B, 2, L, W), F32),
        scratch_shapes=[pltpu.VMEM((W, W), F32), pltpu.VMEM((_hgrn2_groups() * SUB, W), F32)],
        compiler_params=pltpu.CompilerParams(
            dimension_semantics=("parallel", "parallel", "arbitrary")),
        name="hgrn2_scan",
    )(pb, pb, pb, lb.reshape(2, 1, W), tri, sel, bd)


HY_N2 = 128
HY_K1_TILE = 8
HY_FILT_ROWS = 512
HY_KRON_MAX_N1 = 32


def _dot_hi(a, b):
    return jnp.dot(a, b, precision=lax.Precision.HIGHEST, preferred_element_type=F32)


def _hy_filter_kernel(w1t_ref, w1c_ref, w1s_ref, b1_ref, fr_ref, w2_ref, b2_ref, w3_ref, bands_ref,
                      dl_ref, k_ref, sum_ref, *, L):
    tr = k_ref.shape[0]
    r = pl.program_id(0) * tr + lax.broadcasted_iota(jnp.int32, (tr, 1), 0)
    first = r < L
    pos = jnp.where(first, r, 2 * L - r).astype(F32)
    t = pos * (1.0 / (L - 1))
    bw = (pos * (2.0 * math.pi / L)) * bands_ref[...]
    pre = t * w1t_ref[...] + _dot_hi(jnp.cos(bw), w1c_ref[...]) - _dot_hi(jnp.sin(bw), w1s_ref[...])
    fr = fr_ref[...]
    a = jnp.sin(fr * (pre + b1_ref[...]))
    a = jnp.sin(fr * (_dot_hi(a, w2_ref[...]) + b2_ref[...]))
    h = _dot_hi(a, w3_ref[...])
    decay = jnp.exp(-t * dl_ref[...])
    W2 = HY_ORDER * HY_WIDTH
    use_f = first.astype(F32)
    use_b = ((r > L) | (r == 0)).astype(F32)
    hf = h[:, :W2] * decay * use_f
    hb = h[:, W2:] * decay * use_b
    k_ref[...] = hf + hb

    @pl.when(pl.program_id(0) == 0)
    def _():
        sum_ref[...] = jnp.zeros_like(sum_ref)

    sum_ref[...] += jnp.sum(jnp.abs(hf) + jnp.abs(hb), axis=0, keepdims=True)


def _hy_filter(L, w1, b1, freq, w2, b2, w3):
    tr = min(HY_FILT_ROWS, L)
    W2 = HY_ORDER * HY_WIDTH
    bands = jnp.zeros((1, 128), F32).at[0, :HY_BANDS].set(jnp.linspace(1e-4, HY_BANDS - 1, HY_BANDS, dtype=F32))
    w1c = jnp.zeros((128, HY_HIDDEN), F32).at[:HY_BANDS].set(w1[1:1 + HY_BANDS])
    w1s = jnp.zeros((128, HY_HIDDEN), F32).at[:HY_BANDS].set(w1[1 + HY_BANDS:])
    deltas = jnp.abs(jnp.linspace(math.log(HY_DECAY_TARGET) / HY_SLOW_PCT,
                                  math.log(HY_DECAY_TARGET) / HY_FAST_PCT, HY_WIDTH, dtype=F32))
    dl = jnp.tile(deltas, HY_ORDER).reshape(1, W2)
    args = (w1[0:1], w1c, w1s, b1.reshape(1, -1), freq.reshape(1, -1), w2, b2.reshape(1, -1), w3, bands, dl)
    full = lambda a: pl.BlockSpec(a.shape, lambda i: (0, 0))
    return pl.pallas_call(
        functools.partial(_hy_filter_kernel, L=L),
        grid=(2 * L // tr,),
        in_specs=[full(a) for a in args],
        out_specs=[pl.BlockSpec((tr, W2), lambda i: (i, 0)), pl.BlockSpec((1, W2), lambda i: (0, 0))],
        out_shape=[jax.ShapeDtypeStruct((2 * L, W2), F32), jax.ShapeDtypeStruct((1, W2), F32)],
        compiler_params=pltpu.CompilerParams(dimension_semantics=("arbitrary",)),
        name="hyena_filter",
    )(*args)


def _hy_dft_tables(L):
    N = 2 * L
    N2 = HY_N2
    N1 = N // N2
    def cs(n, rows, cols):
        ang = (2.0 * math.pi / n) * ((jnp.arange(rows)[:, None] * jnp.arange(cols)[None, :]) % n).astype(F32)
        return jnp.cos(ang), jnp.sin(ang)
    ca, sa = cs(N1, N1, N1)
    fa_full = jnp.concatenate([ca, -sa], axis=0)
    fa_inv = jnp.concatenate([ca[:N1 // 2], -sa[:N1 // 2]], axis=1) * (1.0 / N)
    cb, sb = cs(N2, N2, N2)
    fb = jnp.concatenate([jnp.concatenate([cb, sb], axis=1), jnp.concatenate([-sb, cb], axis=1)], axis=0)
    tc, ts = cs(N, N1, N2)
    tw = jnp.stack([tc, ts])[..., None] * jnp.ones((1, 1, 1, 128), F32)
    tabs = dict(N1=N1, fa_half=fa_full[:, :N1 // 2].astype(BF16), fa_full=fa_full.astype(BF16),
                fa_inv=fa_inv.astype(BF16), fb=fb.astype(BF16), fb_t=fb.T.astype(BF16), tw=tw)
    if N1 <= HY_KRON_MAX_N1:
        eye = jnp.eye(SUB, dtype=F32)
        tabs['fa_half_kron'] = jnp.kron(fa_full[:, :N1 // 2], eye).astype(BF16)
        tabs['fa_inv_kron'] = jnp.kron(fa_inv, eye).astype(BF16)
    return tabs


def _hy_n2_rows(N1):
    return SUB * max(1, min(HY_N2 // SUB, 256 // N1))


def _hy_dft_a_kernel(m_ref, x_ref, o_ref):
    N1, C = o_ref.shape[2], o_ref.shape[4]
    for g0 in range(0, x_ref.shape[2], SUB):
        x = jnp.concatenate([x_ref[0, :, g0 + j, :] for j in range(SUB)], axis=1).astype(BF16)
        y = _dot(m_ref[...], x)
        for j in range(SUB):
            o_ref[0, 0, :, g0 + j, :] = y[:N1, j * C:(j + 1) * C]
            o_ref[0, 1, :, g0 + j, :] = y[N1:, j * C:(j + 1) * C]


def _hy_dft_a(m, x):
    B, R, N2, C = x.shape
    N1 = m.shape[0] // 2
    W = HY_WIDTH
    G = _hy_n2_rows(N1)
    return pl.pallas_call(
        _hy_dft_a_kernel,
        grid=(B, N2 // G, C // W),
        in_specs=[pl.BlockSpec(m.shape, lambda b, g, c: (0, 0)),
                  pl.BlockSpec((1, R, G, W), lambda b, g, c: (b, 0, g, c))],
        out_specs=pl.BlockSpec((1, 2, N1, G, W), lambda b, g, c: (b, 0, 0, g, c)),
        out_shape=jax.ShapeDtypeStruct((B, 2, N1, N2, C), F32),
        compiler_params=pltpu.CompilerParams(dimension_semantics=("parallel", "parallel", "parallel")),
        name="hyena_dft_a",
    )(m, x)


def _hy_gate_kernel(m_ref, y_ref, z_ref, g_ref, s_ref, o_ref):
    C = o_ref.shape[3]
    for g0 in range(0, o_ref.shape[2], SUB):
        y = jnp.concatenate(
            [jnp.concatenate([y_ref[0, 0, :, g0 + j, :], y_ref[0, 1, :, g0 + j, :]], axis=0) for j in range(SUB)],
            axis=1)
        zc = _dot(m_ref[...], y.astype(BF16))
        for j in range(SUB):
            o_ref[0, :, g0 + j, :] = g_ref[0, :, g0 + j, :] * (
                zc[:, j * C:(j + 1) * C] + s_ref[...] * z_ref[0, :, g0 + j, :])


def _hy_inverse_gate(m, y, z, gate, skip_row):
    B, _, N1, N2, C = y.shape
    G = _hy_n2_rows(N1)
    tok = pl.BlockSpec((1, N1 // 2, G, C), lambda b, g: (b, 0, g, 0))
    return pl.pallas_call(
        _hy_gate_kernel,
        grid=(B, N2 // G),
        in_specs=[pl.BlockSpec(m.shape, lambda b, g: (0, 0)),
                  pl.BlockSpec((1, 2, N1, G, C), lambda b, g: (b, 0, 0, g, 0)),
                  tok, tok,
                  pl.BlockSpec((1, C), lambda b, g: (0, 0))],
        out_specs=tok,
        out_shape=jax.ShapeDtypeStruct((B, N1 // 2, N2, C), F32),
        compiler_params=pltpu.CompilerParams(dimension_semantics=("parallel", "parallel")),
        name="hyena_idft_a_gate",
    )(m, y, z, gate, skip_row)


def _cmul(ar, ai, br, bi):
    return ar * br - ai * bi, ar * bi + ai * br


def _hy_stage_b_one(ar, ai, tcos, tsin, fb, fbt, kr, ki):
    N2 = HY_N2
    tc = jnp.concatenate([tcos, tcos], axis=1)
    ts = jnp.concatenate([tsin, tsin], axis=1)
    br, bi = _cmul(ar, ai, tc, -ts)
    x = _dot(fb, jnp.concatenate([br, bi], axis=0).astype(BF16))
    yr, yi = _cmul(x[:N2], x[N2:], kr, ki)
    y = _dot(fbt, jnp.concatenate([yr, yi], axis=0).astype(BF16))
    return _cmul(y[:N2], y[N2:], tc, ts)


def _hy_fused_kernel(ma_ref, mi_ref, tw_ref, fb_ref, fbt_ref, k_ref, z_ref, g_ref, s_ref, o_ref, w_sc):
    R, N2, C = z_ref.shape[1:]
    N1 = k_ref.shape[1]
    for g0 in range(0, N2, SUB):
        x = z_ref[0, :, g0:g0 + SUB, :].reshape(R * SUB, C).astype(BF16)
        w_sc[:, :, g0:g0 + SUB, :] = _dot(ma_ref[...], x).reshape(2, N1, SUB, C)

    def k1_step(j, carry):
        yr, yi = _hy_stage_b_one(w_sc[0, j], w_sc[1, j], tw_ref[0, j], tw_ref[1, j], fb_ref[...], fbt_ref[...],
                                 k_ref[0, j], k_ref[1, j])
        w_sc[0, j] = yr
        w_sc[1, j] = yi
        return carry

    lax.fori_loop(0, N1, k1_step, 0, unroll=2)
    for g0 in range(0, N2, SUB):
        y = w_sc[:, :, g0:g0 + SUB, :].reshape(2 * N1 * SUB, C).astype(BF16)
        zc = _dot(mi_ref[...], y).reshape(R, SUB, C)
        rows = slice(g0, g0 + SUB)
        o_ref[0, :, rows, :] = g_ref[0, :, rows, :] * (zc + s_ref[...] * z_ref[0, :, rows, :])


def _hy_fused_conv(tabs, kspec, order, z, gate, skip_row):
    B, R, N2, C = z.shape
    N1 = tabs['N1']
    once = lambda a: pl.BlockSpec(a.shape, lambda b: (0,) * a.ndim, pipeline_mode=pl.Buffered(1))
    tok = pl.BlockSpec((1, R, N2, C), lambda b: (b, 0, 0, 0))
    ma, mi = tabs['fa_half_kron'], tabs['fa_inv_kron']
    return pl.pallas_call(
        _hy_fused_kernel,
        grid=(B,),
        in_specs=[once(ma), once(mi), once(tabs['tw']), once(tabs['fb']), once(tabs['fb_t']),
                  pl.BlockSpec((2, N1, N2, C), lambda b: (0, 0, 0, order), pipeline_mode=pl.Buffered(1)),
                  tok, tok, pl.BlockSpec((1, C), lambda b: (0, 0))],
        out_specs=tok,
        out_shape=jax.ShapeDtypeStruct(z.shape, F32),
        scratch_shapes=[pltpu.VMEM((2, N1, N2, C), F32)],
        compiler_params=pltpu.CompilerParams(dimension_semantics=("parallel",)),
        name="hyena_conv_fused",
    )(ma, mi, tabs['tw'], tabs['fb'], tabs['fb_t'], kspec, z, gate, skip_row)


def _hy_stage_b_kernel(a_ref, tw_ref, fb_ref, fbt_ref, k_ref, o_ref, *, conv):
    N2 = HY_N2
    for j in range(a_ref.shape[2]):
        if conv:
            xr, xi = _hy_stage_b_one(a_ref[0, 0, j], a_ref[0, 1, j], tw_ref[0, j], tw_ref[1, j], fb_ref[...],
                                     fbt_ref[...], k_ref[0, j], k_ref[1, j])
        else:
            tc = jnp.concatenate([tw_ref[0, j], tw_ref[0, j]], axis=1)
            ts = jnp.concatenate([tw_ref[1, j], tw_ref[1, j]], axis=1)
            br, bi = _cmul(a_ref[0, 0, j], a_ref[0, 1, j], tc, -ts)
            x = _dot(fb_ref[...], jnp.concatenate([br, bi], axis=0).astype(BF16))
            xr, xi = x[:N2] * k_ref[...], x[N2:] * k_ref[...]
        o_ref[0, 0, j] = xr
        o_ref[0, 1, j] = xi


def _hy_stage_b(a, tabs, kspec=None, order=0, scale=None):
    B, _, N1, N2, C = a.shape
    t1 = HY_K1_TILE
    conv = kspec is not None
    blk = pl.BlockSpec((1, 2, t1, N2, HY_WIDTH), lambda i, b, c: (b, 0, i, 0, c))
    if conv:
        k_arr = kspec
        k_spec = pl.BlockSpec((2, t1, N2, HY_WIDTH), lambda i, b, c: (0, i, 0, order))
    else:
        k_arr = scale
        k_spec = pl.BlockSpec((1, HY_WIDTH), lambda i, b, c: (0, c))
    return pl.pallas_call(
        functools.partial(_hy_stage_b_kernel, conv=conv),
        grid=(N1 // t1, B, C // HY_WIDTH),
        in_specs=[blk,
                  pl.BlockSpec((2, t1, N2, 128), lambda i, b, c: (0, i, 0, 0)),
                  pl.BlockSpec(tabs['fb'].shape, lambda i, b, c: (0, 0)),
                  pl.BlockSpec(tabs['fb_t'].shape, lambda i, b, c: (0, 0)),
                  k_spec],
        out_specs=blk,
        out_shape=jax.ShapeDtypeStruct(a.shape, F32),
        compiler_params=pltpu.CompilerParams(dimension_semantics=("parallel", "parallel", "parallel")),
        name="hyena_stage_b_conv" if conv else "hyena_stage_b_spectrum",
    )(a, tabs['tw'], tabs['fb'], tabs['fb_t'], k_arr)


def _hy_filter_spectrum(L, tabs, w1, b1, freq, w2, b2, w3):
    N1, N2 = tabs['N1'], HY_N2
    W2 = HY_ORDER * HY_WIDTH
    k, total = _hy_filter(L, w1, b1, freq, w2, b2, w3)
    a = _hy_dft_a(tabs['fa_full'], k.reshape(1, N1, N2, W2))
    return _hy_stage_b(a, tabs, scale=1.0 / total)[0]


def _hyena(v, x1, x2, skip, tabs, kspec):
    B, L, _ = v.shape
    N1, N2, C = tabs['N1'], HY_N2, HY_WIDTH
    grid = lambda t: t.reshape(B, N1 // 2, N2, C)
    z = grid(v)
    for n, gate in enumerate((x1, x2)):
        if 'fa_half_kron' in tabs:
            z = _hy_fused_conv(tabs, kspec, n, z, grid(gate), skip[n].reshape(1, C))
        else:
            a = _hy_dft_a(tabs['fa_half'], z)
            y = _hy_stage_b(a, tabs, kspec, order=n)
            z = _hy_inverse_gate(tabs['fa_inv'], y, z, grid(gate), skip[n].reshape(1, C))
    return z.reshape(B, L, C)


def _prep_layer_weights(l, w_in, w_out, na_rpb, mla_w_uq, mla_w_ukv, ffn_w_up, ffn_conv, ffn_w_down):
    wi = w_in[l]
    s0, s1, s2 = NA_COLS, NA_COLS + HG_COLS, NA_COLS + HG_COLS + HY_COLS
    w_ml = jnp.pad(wi[:, s2:], ((0, 0), (0, MLA_COLS_PAD - MLA_COLS)))
    return dict(
        w_na=wi[:, :s0].astype(BF16), w_hg=wi[:, s0:s1].astype(BF16), w_hy=wi[:, s1:s2].astype(BF16),
        w_ml=w_ml.astype(BF16), w_out=w_out[l].astype(BF16), na_bias=_na_bias_table(na_rpb[l]),
        mla=_mla_weights(mla_w_uq[l], mla_w_ukv[l]),
        ffn=_ffn_weight_tiles(ffn_w_up[l], ffn_conv[l], ffn_w_down[l]),
    )


def _trunk_layer(x, mod, lb, g, lw, p, cos, sin, hy_tabs, hy_spec):
    pa, pb, hv, hx1, hx2, pd = _in_proj(x, mod, g, lw['w_na'], lw['w_hg'], lw['w_hy'], lw['w_ml'], p['hy_short'])
    o_na = _neighbourhood_attention(pa, lw['na_bias'])
    hg_dirs = _hgrn2_scan(pb, lb)
    o_hy = _hyena(hv, hx1, hx2, p['hy_skip'], hy_tabs, hy_spec)
    L = x.shape[1]
    q, k, v = _mla_prep(pd, cos[:L], sin[:L], p['mla_q_norm'], p['mla_kv_norm'], lw['mla'])
    o_ml = _mla_flash(q, k, v)
    x1, h2 = _out_proj(x, mod, g, o_na, hg_dirs, pb, p['hg_norm'], o_hy, o_ml, lw['w_out'])
    return _conv_ffn(h2, x1, mod, g, *lw['ffn'])


def kernel(x_prompt, x_sample, c_prompt, c_sample, ada_w, ada_b, norm_g, w_in, w_out, na_rpb, hg_lb,
           hg_norm, hy_short, hy_w1, hy_b1, hy_freq, hy_w2, hy_b2, hy_w3, hy_skip, mla_q_norm,
           mla_kv_norm, mla_w_uq, mla_w_ukv, ffn_w_up, ffn_conv, ffn_w_down):
    Bp, Bs = x_prompt.shape[0], x_sample.shape[0]
    Lp, Ls = x_prompt.shape[1], x_sample.shape[1]
    lb_soft = jax.nn.softmax(hg_lb.astype(F32), axis=0)
    lower_bounds = jnp.cumsum(lb_soft, axis=0) - lb_soft[:1]

    R = -(-(Bp + Bs) // 8) * 8
    cond = jnp.zeros((R, D_MODEL), F32).at[:Bp].set(c_prompt).at[Bp:Bp + Bs].set(c_sample)
    mod = _ada_modulation(cond, ada_w, ada_b).reshape(DEPTH, R, 6, D_MODEL)
    mod = jnp.pad(mod, ((0, 0), (0, 0), (0, 2), (0, 0)))

    cos, sin = _rope_tables(max(Lp, Ls))
    tabs_p, tabs_s = _hy_dft_tables(Lp), _hy_dft_tables(Ls)
    y_prompt, y_sample = x_prompt, x_sample
    for l in range(DEPTH):
        lw = _prep_layer_weights(l, w_in, w_out, na_rpb, mla_w_uq, mla_w_ukv, ffn_w_up, ffn_conv, ffn_w_down)
        p = dict(hg_norm=hg_norm[l], hy_short=hy_short[l], hy_skip=hy_skip[l],
                 mla_q_norm=mla_q_norm[l].reshape(1, -1), mla_kv_norm=mla_kv_norm[l].reshape(1, -1))
        filt = (hy_w1[l], hy_b1[l], hy_freq[l], hy_w2[l], hy_b2[l], hy_w3[l])
        spec_p = _hy_filter_spectrum(Lp, tabs_p, *filt)
        spec_s = _hy_filter_spectrum(Ls, tabs_s, *filt)
        y_prompt = _trunk_layer(y_prompt, mod[l, :Bp], lower_bounds[l], norm_g[l], lw, p, cos, sin,
                                tabs_p, spec_p)
        y_sample = _trunk_layer(y_sample, mod[l, Bp:Bp + Bs], lower_bounds[l], norm_g[l], lw, p, cos, sin,
                                tabs_s, spec_s)
    return (y_prompt, y_sample)
```

```python
import functools
import math

import jax
import jax.numpy as jnp
from jax import lax
from jax.experimental import pallas as pl
from jax.experimental.pallas import tpu as pltpu

F32 = jnp.float32
BF16 = jnp.bfloat16

D_MODEL = 1024
DEPTH = 4
GRID_W = 64
HEAD_DIM = 64
NA_WIDTH = 256
NA_HEADS = 4
NA_KH = 8
NA_KW = 16
HG_WIDTH = 256
HG_HEADS = 4
HG_CHUNK = 64
HY_WIDTH = 256
HY_ORDER = 2
HY_EMB = 33
HY_BANDS = 16
HY_HIDDEN = 64
HY_DECAY_TARGET = 1e-2
HY_FAST_PCT = 0.3
HY_SLOW_PCT = 1.5
MLA_WIDTH = 256
MLA_HEADS = 4
MLA_NOPE = 64
MLA_ROPE = 32
MLA_V = 64
MLA_Q_LORA = 256
MLA_KV_LORA = 128
MLA_HEAD_PAD = 128
ROPE_THETA = 10000.0
D_FF = 2816
EPS = 1e-6
NA_COLS = 3 * NA_WIDTH
HG_COLS = 5 * HG_WIDTH
HY_COLS = 3 * HY_WIDTH
MLA_COLS = MLA_Q_LORA + MLA_KV_LORA + MLA_ROPE
MLA_COLS_PAD = 512
NEG_BIG = -1e30
SUB = 8
LANES = 128


def _dot(a, b):
    return jnp.dot(a, b, preferred_element_type=F32)


def _dot_nt(a, b):
    return lax.dot_general(a, b, (((1,), (1,)), ((), ())), preferred_element_type=F32)


def _rms(x, g):
    return x * lax.rsqrt(jnp.mean(x * x, axis=-1, keepdims=True) + EPS) * g


def _ada_kernel(c_ref, w_ref, b_ref, o_ref):
    c = c_ref[...]
    s = c * jax.nn.sigmoid(c)
    o_ref[0] = _dot(s.astype(BF16), w_ref[0].astype(BF16)) + b_ref[0]


def _ada_modulation(cond, ada_w, ada_b):
    R = cond.shape[0]
    tn = 1536
    return pl.pallas_call(
        _ada_kernel,
        grid=(DEPTH, 6 * D_MODEL // tn),
        in_specs=[
            pl.BlockSpec((R, D_MODEL), lambda l, n: (0, 0)),
            pl.BlockSpec((1, D_MODEL, tn), lambda l, n: (l, 0, n)),
            pl.BlockSpec((1, 1, tn), lambda l, n: (l, 0, n)),
        ],
        out_specs=pl.BlockSpec((1, R, tn), lambda l, n: (l, 0, n)),
        out_shape=jax.ShapeDtypeStruct((DEPTH, R, 6 * D_MODEL), F32),
        name="ada_modulation",
    )(cond, ada_w, ada_b.reshape(DEPTH, 1, 6 * D_MODEL))


IN_HALO = 16


def _in_proj_kernel(x_ref, xp_ref, xn_ref, mod_ref, g_ref, wna, whg, why, wml, sw_ref,
                    ona, ohg, ov, ox1, ox2, oml, hext_ref):
    i = pl.program_id(1)
    tm = x_ref.shape[1]
    H = IN_HALO
    g = g_ref[0:1, :]
    scale = 1.0 + mod_ref[0, 1:2, :]
    shift = mod_ref[0, 0:1, :]
    prep = lambda x: (_rms(x, g) * scale + shift).astype(BF16)
    hext_ref[0:H, :] = prep(xp_ref[0])
    hext_ref[H:H + tm, :] = prep(x_ref[0])
    hext_ref[H + tm:2 * H + tm, :] = prep(xn_ref[0])
    hb = hext_ref[H:H + tm, :]
    ona[0] = _dot(hb, wna[...])
    ohg[0] = _dot(hb, whg[...])
    oml[0] = _dot(hb, wml[...])
    ue = _dot(hext_ref[...], why[...])
    u = ue[H:H + tm]
    u_prev = ue[H - 1:H] * (i > 0).astype(F32)
    u_next = ue[H + tm:H + tm + 1] * (i < pl.num_programs(1) - 1).astype(F32)
    sub = lax.broadcasted_iota(jnp.int32, (SUB, 1), 0)
    down = pltpu.roll(u, 1, 0)
    up = pltpu.roll(u, tm - 1, 0)
    down = jnp.concatenate([jnp.where(sub == 0, u_prev, down[:SUB]), down[SUB:]], axis=0)
    up = jnp.concatenate([up[:tm - SUB], jnp.where(sub == SUB - 1, u_next, up[tm - SUB:])], axis=0)
    y = sw_ref[0:1, :] * down + sw_ref[1:2, :] * u + sw_ref[2:3, :] * up
    W = HY_WIDTH
    ov[0] = y[:, 0 * W:1 * W]
    ox1[0] = y[:, 1 * W:2 * W]
    ox2[0] = y[:, 2 * W:3 * W]


def _in_proj(x, mod, g, w_na, w_hg, w_hy, w_ml, short_w, tm=512):
    B, L, D = x.shape
    tm = min(tm, L)
    nh = tm // IN_HALO
    last = L // IN_HALO - 1
    full = lambda a: pl.BlockSpec(a.shape, lambda b, i: (0, 0))
    outs = [NA_COLS, HG_COLS, HY_WIDTH, HY_WIDTH, HY_WIDTH, MLA_COLS_PAD]
    return pl.pallas_call(
        _in_proj_kernel,
        grid=(B, L // tm),
        in_specs=[
            pl.BlockSpec((1, tm, D), lambda b, i: (b, i, 0)),
            pl.BlockSpec((1, IN_HALO, D), lambda b, i: (b, jnp.maximum(i * nh - 1, 0), 0)),
            pl.BlockSpec((1, IN_HALO, D), lambda b, i: (b, jnp.minimum((i + 1) * nh, last), 0)),
            pl.BlockSpec((1, 8, D), lambda b, i: (b, 0, 0)),
            full(g), full(w_na), full(w_hg), full(w_hy), full(w_ml), full(short_w),
        ],
        out_specs=[pl.BlockSpec((1, tm, n), lambda b, i: (b, i, 0)) for n in outs],
        out_shape=[jax.ShapeDtypeStruct((B, L, n), F32) for n in outs],
        scratch_shapes=[pltpu.VMEM((tm + 2 * IN_HALO, D), BF16)],
        compiler_params=pltpu.CompilerParams(dimension_semantics=("parallel", "parallel")),
        name="in_proj",
    )(x, x, x, mod, g, w_na, w_hg, w_hy, w_ml, short_w)


def _head_mean_sq(o, sel):
    sq = o * o
    hi = sq.astype(BF16)
    lo = (sq - hi.astype(F32)).astype(BF16)
    return (_dot(hi, sel) + _dot(lo, sel)) * (1.0 / (HG_WIDTH // HG_HEADS))


def _out_proj_kernel(x_ref, mod_ref, g_ref, na, hgf, hgb, hgg, hgn, sel, hy, ml, w_ref, x1_ref, h2_ref):
    W = NA_WIDTH
    o = hgf[0] + hgb[0]
    gate = hgg[0]
    hg = o * lax.rsqrt(_head_mean_sq(o, sel[...]) + EPS) * hgn[...] * (gate * jax.nn.sigmoid(gate))
    mix = _dot(na[0].astype(BF16), w_ref[0 * W:1 * W, :])
    mix += _dot(hg.astype(BF16), w_ref[1 * W:2 * W, :])
    mix += _dot(hy[0].astype(BF16), w_ref[2 * W:3 * W, :])
    mix += _dot(ml[0].astype(BF16), w_ref[3 * W:4 * W, :])
    x1 = x_ref[0] + mod_ref[0, 2:3, :] * _rms(mix, g_ref[1:2, :])
    x1_ref[0] = x1
    h2 = _rms(x1, g_ref[2:3, :]) * (1.0 + mod_ref[0, 4:5, :]) + mod_ref[0, 3:4, :]
    h2_ref[0] = h2.astype(BF16)


def _out_proj(x, mod, g, o_na, hg_dirs, pb, hg_norm, o_hy, o_ml, w_out, tm=512):
    B, L, D = x.shape
    tm = min(tm, L)
    tok = lambda n: pl.BlockSpec((1, tm, n), lambda b, i: (b, i, 0))
    sel = _hgrn2_tables()[1]
    return pl.pallas_call(
        _out_proj_kernel,
        grid=(B, L // tm),
        in_specs=[
            tok(D),
            pl.BlockSpec((1, 8, D), lambda b, i: (b, 0, 0)),
            pl.BlockSpec(g.shape, lambda b, i: (0, 0)),
            tok(NA_WIDTH),
            tok(HG_WIDTH), tok(HG_WIDTH),
            pl.BlockSpec((1, tm, HG_WIDTH), lambda b, i: (b, i, 4)),
            pl.BlockSpec((1, HG_WIDTH), lambda b, i: (0, 0)),
            pl.BlockSpec(sel.shape, lambda b, i: (0, 0)),
            tok(HY_WIDTH), tok(MLA_WIDTH),
            pl.BlockSpec(w_out.shape, lambda b, i: (0, 0)),
        ],
        out_specs=[tok(D), tok(D)],
        out_shape=[jax.ShapeDtypeStruct((B, L, D), F32), jax.ShapeDtypeStruct((B, L, D), BF16)],
        compiler_params=pltpu.CompilerParams(dimension_semantics=("parallel", "parallel")),
        name="out_proj",
    )(x, mod, g, o_na, hg_dirs[0], hg_dirs[1], pb, hg_norm.reshape(1, HG_WIDTH), sel, o_hy, o_ml, w_out)


FFN_HALO = 16


def _ffn_kernel(hm_ref, hp_ref, hn_ref, x_ref, mod_ref, g_ref, wu_ref, cw_ref, wd_ref, o_ref, acc_ref,
                hext_ref):
    i = pl.program_id(1)
    tm = hm_ref.shape[1]
    nf = wd_ref.shape[0]
    not_first = (i > 0).astype(F32)
    not_last = (i < pl.num_programs(1) - 1).astype(F32)
    sub = lax.broadcasted_iota(jnp.int32, (SUB, 1), 0)

    H = FFN_HALO
    hext_ref[0:H, :] = hp_ref[0]
    hext_ref[H:H + tm, :] = hm_ref[0]
    hext_ref[H + tm:2 * H + tm, :] = hn_ref[0]

    def conv_branch(t):
        c = cw_ref[t]
        ue = _dot(hext_ref[...], wu_ref[t])
        u = ue[H:H + tm]
        u_prev = ue[H - 1:H] * not_first
        u_next = ue[H + tm:H + tm + 1] * not_last
        down = pltpu.roll(u, 1, 0)
        up = pltpu.roll(u, tm - 1, 0)
        down = jnp.concatenate([jnp.where(sub == 0, u_prev, down[:SUB]), down[SUB:]], axis=0)
        up = jnp.concatenate([up[:tm - SUB], jnp.where(sub == SUB - 1, u_next, up[tm - SUB:])], axis=0)
        return c[0:1, :] * down + c[1:2, :] * u + c[2:3, :] * up

    def tile(f, carry):
        a = conv_branch(f)
        b = conv_branch(nf + f)
        act = (jax.nn.gelu(a, approximate=True) * b).astype(BF16)
        acc_ref[...] += _dot(act, wd_ref[f])
        return carry

    acc_ref[...] = jnp.zeros_like(acc_ref)
    lax.fori_loop(0, nf, tile, 0, unroll=FFN_UNROLL)
    o_ref[0] = x_ref[0] + mod_ref[0, 5:6, :] * _rms(acc_ref[...], g_ref[3:4, :])


FFN_TF = 256
FFN_UNROLL = 2


def _conv_ffn(h2, x1, mod, g, w_up, conv_w, w_down, tm=1024):
    B, L, D = x1.shape
    tm = min(tm, L)
    nh = tm // FFN_HALO
    last_halo = L // FFN_HALO - 1
    once = lambda a: pl.BlockSpec(a.shape, lambda b, i: (0,) * a.ndim, pipeline_mode=pl.Buffered(1))
    return pl.pallas_call(
        _ffn_kernel,
        grid=(B, L // tm),
        in_specs=[
            pl.BlockSpec((1, tm, D), lambda b, i: (b, i, 0)),
            pl.BlockSpec((1, FFN_HALO, D), lambda b, i: (b, jnp.maximum(i * nh - 1, 0), 0)),
            pl.BlockSpec((1, FFN_HALO, D), lambda b, i: (b, jnp.minimum((i + 1) * nh, last_halo), 0)),
            pl.BlockSpec((1, tm, D), lambda b, i: (b, i, 0)),
            pl.BlockSpec((1, 8, D), lambda b, i: (b, 0, 0)),
            pl.BlockSpec(g.shape, lambda b, i: (0, 0)),
            once(w_up), once(conv_w), once(w_down),
        ],
        out_specs=pl.BlockSpec((1, tm, D), lambda b, i: (b, i, 0)),
        out_shape=jax.ShapeDtypeStruct((B, L, D), F32),
        scratch_shapes=[pltpu.VMEM((tm, D), F32), pltpu.VMEM((tm + 2 * FFN_HALO, D), BF16)],
        compiler_params=pltpu.CompilerParams(dimension_semantics=("parallel", "parallel")),
        name="conv_ffn",
    )(h2, h2, h2, x1, mod, g, w_up, conv_w, w_down)


def _ffn_weight_tiles(w_up, conv_w, w_down):
    D = w_up.shape[0]
    n2 = 2 * D_FF // FFN_TF
    wu = w_up.reshape(D, n2, FFN_TF).transpose(1, 0, 2).astype(BF16)
    cw = conv_w.reshape(3, n2, FFN_TF).transpose(1, 0, 2)
    wd = w_down.reshape(D_FF // FFN_TF, FFN_TF, D).astype(BF16)
    return wu, cw, wd


NA_RB = 8


def _na_kernel(q_ref, kp_ref, kc_ref, kn_ref, vp_ref, vc_ref, vn_ref, bias_ref, o_ref,
               kbuf, vbuf, *, rows):
    i = pl.program_id(1)
    blk = NA_RB * GRID_W
    win = NA_KH * GRID_W
    kbuf[0 * blk:1 * blk, :] = kp_ref[0].astype(BF16)
    kbuf[1 * blk:2 * blk, :] = kc_ref[0].astype(BF16)
    kbuf[2 * blk:3 * blk, :] = kn_ref[0].astype(BF16)
    vbuf[0 * blk:1 * blk, :] = vp_ref[0].astype(BF16)
    vbuf[1 * blk:2 * blk, :] = vc_ref[0].astype(BF16)
    vbuf[2 * blk:3 * blk, :] = vn_ref[0].astype(BF16)
    scale = HEAD_DIM ** -0.5 * math.log2(math.e)
    lane_head = lax.broadcasted_iota(jnp.int32, (1, NA_WIDTH), 1) // HEAD_DIM
    for j in range(NA_RB):
        r = i * NA_RB + j
        start = jnp.clip(r - NA_KH // 2, 0, rows - NA_KH)
        loc = start - (i - 1) * NA_RB
        dr0 = start - r + (NA_KH - 1)
        off = pl.multiple_of(loc * GRID_W, GRID_W)
        kw = kbuf[pl.ds(off, win), :]
        vw = vbuf[pl.ds(off, win), :]
        qj = (q_ref[0, j * GRID_W:(j + 1) * GRID_W, :] * scale).astype(BF16)
        qbd = jnp.concatenate([jnp.where(lane_head == h, qj, jnp.zeros_like(qj)) for h in range(NA_HEADS)],
                              axis=0)
        s = _dot_nt(qbd, kw) + bias_ref[dr0]
        m = jnp.max(s, axis=-1, keepdims=True)
        p = jnp.exp2(s - m)
        l = jnp.sum(p, axis=-1, keepdims=True)
        pv = _dot(p.astype(BF16), vw) / l
        o = None
        for h in range(NA_HEADS):
            part = jnp.where(lane_head == h, pv[h * GRID_W:(h + 1) * GRID_W], 0.0)
            o = part if o is None else o + part
        o_ref[0, j * GRID_W:(j + 1) * GRID_W, :] = o


def _na_bias_table(rpb):
    c = jnp.arange(GRID_W)
    dc = jnp.clip(c[None, :] - c[:, None] + (NA_KW - 1), 0, 2 * NA_KW - 2)
    col_start = jnp.clip(c - NA_KW // 2, 0, GRID_W - NA_KW)
    ok = (c[None, :] >= col_start[:, None]) & (c[None, :] < col_start[:, None] + NA_KW)
    e = jnp.where(ok[None, None], rpb.astype(F32)[:, :, dc], NEG_BIG)
    t = jnp.stack([e[:, d0:d0 + NA_KH] for d0 in range(NA_KH)], axis=1)
    t = t.transpose(1, 0, 3, 2, 4).reshape(NA_KH, NA_HEADS * GRID_W, NA_KH * GRID_W)
    return jnp.where(t > 0.5 * NEG_BIG, t * math.log2(math.e), NEG_BIG)


def _neighbourhood_attention(pa, bias):
    B, L, _ = pa.shape
    rows = L // GRID_W
    blk = NA_RB * GRID_W
    nblk = rows // NA_RB
    spec = lambda col, shift: pl.BlockSpec(
        (1, blk, NA_WIDTH), lambda b, i: (b, jnp.clip(i + shift, 0, nblk - 1), col))
    return pl.pallas_call(
        functools.partial(_na_kernel, rows=rows),
        grid=(B, nblk),
        in_specs=[spec(0, 0), spec(1, -1), spec(1, 0), spec(1, 1), spec(2, -1), spec(2, 0), spec(2, 1),
                  pl.BlockSpec(bias.shape, lambda b, i: (0, 0, 0))],
        out_specs=pl.BlockSpec((1, blk, NA_WIDTH), lambda b, i: (b, i, 0)),
        out_shape=jax.ShapeDtypeStruct((B, L, NA_WIDTH), F32),
        scratch_shapes=[pltpu.VMEM((3 * blk, NA_WIDTH), BF16), pltpu.VMEM((3 * blk, NA_WIDTH), BF16)],
        compiler_params=pltpu.CompilerParams(dimension_semantics=("parallel", "parallel")),
        name="neighbourhood_attention",
    )(pa, pa, pa, pa, pa, pa, pa, bias)


def _rope_table_kernel(inv_ref, cos_ref, sin_ref):
    t = cos_ref.shape[0]
    pos = (pl.program_id(0) * t + lax.broadcasted_iota(jnp.int32, (t, MLA_HEAD_PAD), 0)).astype(F32)
    ang = pos * inv_ref[...]
    cos_ref[...] = jnp.cos(ang)
    sin_ref[...] = jnp.sin(ang)


def _rope_tables(L):
    half = MLA_ROPE // 2
    inv = ROPE_THETA ** (-jnp.arange(half, dtype=F32) / half)
    inv_row = jnp.zeros((1, MLA_HEAD_PAD), F32).at[0, MLA_NOPE:MLA_NOPE + MLA_ROPE].set(jnp.tile(inv, 2))
    t = min(L, 1024)
    return pl.pallas_call(
        _rope_table_kernel,
        grid=(L // t,),
        in_specs=[pl.BlockSpec((1, MLA_HEAD_PAD), lambda i: (0, 0))],
        out_specs=[pl.BlockSpec((t, MLA_HEAD_PAD), lambda i: (i, 0))] * 2,
        out_shape=[jax.ShapeDtypeStruct((L, MLA_HEAD_PAD), F32)] * 2,
        name="rope_tables",
    )(inv_row)


def _mla_prep_kernel(pd_ref, cos_ref, sin_ref, qn_ref, kvn_ref, wq, wqr, wk, wv, we, wer,
                     q_out, k_out, v_out):
    pd = pd_ref[0]
    cos = cos_ref[...]
    sin = sin_ref[...]
    nq = _rms(pd[:, :MLA_Q_LORA], qn_ref[...]).astype(BF16)
    nkv = _rms(pd[:, MLA_Q_LORA:MLA_Q_LORA + MLA_KV_LORA], kvn_ref[...]).astype(BF16)
    kr = pd[:, MLA_Q_LORA + MLA_KV_LORA:].astype(BF16)
    q = _dot(nq, wq[...])
    q_rot = _dot(nq, wqr[...])
    k = _dot(nkv, wk[...]) + _dot(kr, we[...])
    k_rot = _dot(kr, wer[...])
    scale = (MLA_NOPE + MLA_ROPE) ** -0.5 * math.log2(math.e)
    for h in range(MLA_HEADS):
        hs = slice(h * MLA_HEAD_PAD, (h + 1) * MLA_HEAD_PAD)
        q_out[0, :, hs] = ((q[:, hs] * cos + q_rot[:, hs] * sin) * scale).astype(BF16)
        k_out[0, :, hs] = (k[:, hs] * cos + k_rot[:, hs] * sin).astype(BF16)
    v_out[0] = _dot(nkv, wv[...]).astype(BF16)


def _mla_weights(w_uq, w_ukv):
    half = MLA_ROPE // 2
    P = jnp.zeros((MLA_ROPE, MLA_ROPE), F32)
    P = P.at[jnp.arange(half) + half, jnp.arange(half)].set(-1.0)
    P = P.at[jnp.arange(half), jnp.arange(half) + half].set(1.0)
    HP = MLA_HEAD_PAD
    wq = jnp.zeros((MLA_Q_LORA, MLA_HEADS * HP), F32)
    wqr = jnp.zeros_like(wq)
    wk = jnp.zeros((MLA_KV_LORA, MLA_HEADS * HP), F32)
    wv = jnp.zeros((MLA_KV_LORA, MLA_HEADS * MLA_V), F32)
    we = jnp.zeros((MLA_COLS_PAD - MLA_Q_LORA - MLA_KV_LORA, MLA_HEADS * HP), F32)
    wer = jnp.zeros_like(we)
    eye = jnp.eye(MLA_ROPE, dtype=F32)
    for h in range(MLA_HEADS):
        qh = w_uq[:, h * (MLA_NOPE + MLA_ROPE):(h + 1) * (MLA_NOPE + MLA_ROPE)]
        wq = wq.at[:, h * HP:h * HP + MLA_NOPE + MLA_ROPE].set(qh)
        wqr = wqr.at[:, h * HP + MLA_NOPE:h * HP + MLA_NOPE + MLA_ROPE].set(qh[:, MLA_NOPE:] @ P)
        kvh = w_ukv[:, h * (MLA_NOPE + MLA_V):(h + 1) * (MLA_NOPE + MLA_V)]
        wk = wk.at[:, h * HP:h * HP + MLA_NOPE].set(kvh[:, :MLA_NOPE])
        wv = wv.at[:, h * MLA_V:(h + 1) * MLA_V].set(kvh[:, MLA_NOPE:])
        we = we.at[:MLA_ROPE, h * HP + MLA_NOPE:h * HP + MLA_NOPE + MLA_ROPE].set(eye)
        wer = wer.at[:MLA_ROPE, h * HP + MLA_NOPE:h * HP + MLA_NOPE + MLA_ROPE].set(P)
    return tuple(w.astype(BF16) for w in (wq, wqr, wk, wv, we, wer))


def _mla_prep(pd, cos, sin, q_norm, kv_norm, weights, tm=512):
    B, L, _ = pd.shape
    tm = min(tm, L)
    full = lambda a: pl.BlockSpec(a.shape, lambda b, i: (0, 0))
    HP = MLA_HEADS * MLA_HEAD_PAD
    return pl.pallas_call(
        _mla_prep_kernel,
        grid=(B, L // tm),
        in_specs=[
            pl.BlockSpec((1, tm, MLA_COLS_PAD), lambda b, i: (b, i, 0)),
            pl.BlockSpec((tm, MLA_HEAD_PAD), lambda b, i: (i, 0)),
            pl.BlockSpec((tm, MLA_HEAD_PAD), lambda b, i: (i, 0)),
            full(q_norm), full(kv_norm),
        ] + [full(w) for w in weights],
        out_specs=[pl.BlockSpec((1, tm, HP), lambda b, i: (b, i, 0)),
                   pl.BlockSpec((1, tm, HP), lambda b, i: (b, i, 0)),
                   pl.BlockSpec((1, tm, MLA_WIDTH), lambda b, i: (b, i, 0))],
        out_shape=[jax.ShapeDtypeStruct((B, L, HP), BF16), jax.ShapeDtypeStruct((B, L, HP), BF16),
                   jax.ShapeDtypeStruct((B, L, MLA_WIDTH), BF16)],
        compiler_params=pltpu.CompilerParams(dimension_semantics=("parallel", "parallel")),
        name="mla_prep",
    )(pd, cos, sin, q_norm, kv_norm, *weights)


FLASH_ROWS = 16
FLASH_HEADS = 4


def _flash_kernel(q_ref, k_ref, v_ref, o_ref, m_sc, acc_sc, p_sc):
    kv = pl.program_id(3)
    _, tq, tk = p_sc.shape
    RB = FLASH_ROWS

    @pl.when(kv == 0)
    def _():
        m_sc[...] = jnp.full_like(m_sc, -jnp.inf)
        acc_sc[...] = jnp.zeros_like(acc_sc)

    first_head = lax.broadcasted_iota(jnp.int32, (1, LANES), 1) < MLA_V
    for h in range(FLASH_HEADS):
        q = q_ref[0, :, h * MLA_HEAD_PAD:(h + 1) * MLA_HEAD_PAD]
        k = k_ref[0, :, h * MLA_HEAD_PAD:(h + 1) * MLA_HEAD_PAD]
        s = _dot_nt(q, k)
        alphas = []
        for r in range(tq // RB):
            rows = slice(r * RB, (r + 1) * RB)
            x = s[rows, 0:LANES]
            for c in range(1, tk // LANES):
                x = jnp.maximum(x, s[rows, c * LANES:(c + 1) * LANES])
            m_prev = m_sc[h, rows, :]
            m_new = jnp.maximum(m_prev, jnp.max(x, axis=-1, keepdims=True))
            alphas.append(jnp.exp2(m_prev - m_new))
            m_sc[h, rows, :] = m_new
            for c in range(tk // LANES):
                cols = slice(c * LANES, (c + 1) * LANES)
                p_sc[h, rows, cols] = jnp.exp2((s[rows, cols] - m_new).astype(BF16))
        v = v_ref[0, :, (h // 2) * LANES:(h // 2 + 1) * LANES]
        ones = jnp.ones_like(v)
        v_aug = jnp.where(first_head, v, ones) if h % 2 == 0 else jnp.where(first_head, ones, v)
        acc_sc[h] = jnp.concatenate(alphas, axis=0) * acc_sc[h] + _dot(p_sc[h], v_aug)

    @pl.when(kv == pl.num_programs(3) - 1)
    def _():
        for g in range(FLASH_HEADS // 2):
            a0 = acc_sc[2 * g]
            a1 = acc_sc[2 * g + 1]
            num = jnp.where(first_head, a0, a1)
            den = jnp.where(first_head, pltpu.roll(a0, MLA_V, 1), pltpu.roll(a1, MLA_V, 1))
            o_ref[0, :, g * LANES:(g + 1) * LANES] = num / den


def _mla_flash(q, k, v, tq=512, tk=2048):
    B, L, _ = q.shape
    tq = min(tq, L)
    tk = min(tk, L)
    nh = FLASH_HEADS
    stat = pltpu.VMEM((nh, tq, LANES), F32)
    return pl.pallas_call(
        _flash_kernel,
        grid=(B, MLA_HEADS // nh, L // tq, L // tk),
        in_specs=[
            pl.BlockSpec((1, tq, nh * MLA_HEAD_PAD), lambda b, h, i, j: (b, i, h)),
            pl.BlockSpec((1, tk, nh * MLA_HEAD_PAD), lambda b, h, i, j: (b, j, h)),
            pl.BlockSpec((1, tk, nh * MLA_V), lambda b, h, i, j: (b, j, h)),
        ],
        out_specs=pl.BlockSpec((1, tq, nh * MLA_V), lambda b, h, i, j: (b, i, h)),
        out_shape=jax.ShapeDtypeStruct((B, L, MLA_WIDTH), F32),
        scratch_shapes=[stat, stat, pltpu.VMEM((nh, tq, tk), BF16)],
        compiler_params=pltpu.CompilerParams(
            dimension_semantics=("parallel", "parallel", "parallel", "arbitrary")),
        name="mla_flash",
    )(q, k, v)


HG_T = 32
HG_TB = 2048


def _hgrn2_kernel(qf_ref, qb_ref, ff_ref, fb_ref, if_ref, ib_ref, lb_ref, tri_ref, sel_ref, bd_ref,
                  of_ref, ob_ref, st_ref, g_ref, *, nsub):
    T = HG_T
    ngrp = T // SUB

    @pl.when(pl.program_id(1) == 0)
    def _():
        st_ref[...] = jnp.zeros_like(st_ref)

    sel = sel_ref[...]
    row = lax.broadcasted_iota(jnp.int32, (SUB, HG_WIDTH), 0)

    def step(c, backward):
        z = 1 if backward else 0
        q_ref, f_ref, i_ref, o_ref = (qb_ref, fb_ref, ib_ref, ob_ref) if backward else (qf_ref, ff_ref, if_ref, of_ref)
        lb = lb_ref[z]
        r0 = pl.multiple_of(c * T, T)
        qs = q_ref[0, pl.ds(r0, T), :]
        qs = qs * jax.nn.sigmoid(qs)
        fg = lb + (1.0 - lb) * jax.nn.sigmoid(f_ref[0, pl.ds(r0, T), :])
        kk = 1.0 - fg
        vv = i_ref[0, pl.ds(r0, T), :]
        b = _split_dot_rhs(tri_ref[z], jnp.log(fg) * math.log2(math.e))
        edge = b[0:1, :] if backward else b[T - 1:T, :]
        st = st_ref[z]
        o_inter = _dot_nt((qs * jnp.exp2(b)).astype(BF16), st.astype(BF16))
        off = 0
        offs = {}
        for s in range(T):
            gs = s // SUB
            groups = range(0, gs + 1) if backward else range(gs, ngrp)
            bs = b[s:s + 1, :]
            ks = kk[s:s + 1, :]
            for gidx in groups:
                rs = slice(gidx * SUB, (gidx + 1) * SUB)
                gval = qs[rs] * jnp.exp2(b[rs] - bs) * ks
                if gidx == gs:
                    keep = (row <= s - gs * SUB) if backward else (row >= s - gs * SUB)
                    gval = jnp.where(keep, gval, 0.0)
                g_ref[z, off:off + SUB, :] = gval
                offs[(s, gidx)] = off
                off += SUB
        red = _dot(g_ref[z].astype(BF16), sel)
        outs = []
        for gidx in range(ngrp):
            acc = o_inter[gidx * SUB:(gidx + 1) * SUB]
            for s in range(T):
                if (s, gidx) in offs:
                    o0 = offs[(s, gidx)]
                    acc = acc + red[o0:o0 + SUB] * vv[s:s + 1, :]
            outs.append(acc)
        o_ref[0, pl.ds(r0, T), :] = jnp.concatenate(outs, axis=0)
        kt = (kk * jnp.exp2(edge - b)).astype(BF16)
        upd = lax.dot_general(vv.astype(BF16), kt, (((0,), (0,)), ((), ())), preferred_element_type=F32)
        st_ref[z] = st * jnp.exp2(edge) + upd * bd_ref[...]

    def both(c, carry):
        step(c, False)
        step(nsub - 1 - c, True)
        return carry

    lax.fori_loop(0, nsub, both, 0, unroll=4)


def _split_dot_rhs(tri, x):
    tb = tri.astype(BF16)
    acc = None
    for _ in range(3):
        hi = x.astype(BF16)
        part = _dot(tb, hi)
        acc = part if acc is None else acc + part
        x = x - hi.astype(F32)
    return acc


def _hgrn2_tables():
    T = HG_T
    r = jnp.arange(T)
    tri = jnp.stack([(r[None, :] <= r[:, None]), (r[None, :] >= r[:, None])]).astype(F32)
    head = jnp.arange(HG_WIDTH) // (HG_WIDTH // HG_HEADS)
    same = (head[:, None] == head[None, :])
    return tri, same.astype(BF16), same.astype(F32)


def _hgrn2_groups():
    T, ngrp = HG_T, HG_T // SUB
    return sum(ngrp - s // SUB for s in range(T))


def _hgrn2_scan(pb, lb):
    B, L, _ = pb.shape
    tb = min(HG_TB, L)
    nblk = L // tb
    tri, sel, bd = _hgrn2_tables()
    W = HG_WIDTH
    fwd = lambda col: pl.BlockSpec((1, tb, W), lambda b, j: (b, j, col))
    bwd = lambda col: pl.BlockSpec((1, tb, W), lambda b, j: (b, nblk - 1 - j, col))
    full = lambda a: pl.BlockSpec(a.shape, lambda b, j: (0,) * a.ndim)
    lb3 = lb.reshape(2, 1, W)
    return pl.pallas_call(
        functools.partial(_hgrn2_kernel, nsub=tb // HG_T),
        grid=(B, nblk),
        in_specs=[fwd(0), bwd(0), fwd(1), bwd(2), fwd(3), bwd(3), full(lb3), full(tri), full(sel), full(bd)],
        out_specs=[fwd(0), bwd(0)],
        out_shape=[jax.ShapeDtypeStruct((B, L, W), F32)] * 2,
        scratch_shapes=[pltpu.VMEM((2, W, W), F32), pltpu.VMEM((2, _hgrn2_groups() * SUB, W), F32)],
        compiler_params=pltpu.CompilerParams(dimension_semantics=("parallel", "arbitrary")),
        name="hgrn2_scan",
    )(pb, pb, pb, pb, pb, pb, lb3, tri, sel, bd)


HY_N2 = 128
HY_K1_TILE = 8
HY_FILT_ROWS = 512
HY_KRON_MAX_N1 = 32


def _dot_hi(a, b):
    return jnp.dot(a, b, precision=lax.Precision.HIGHEST, preferred_element_type=F32)


def _dot_3x(a, b):
    a_hi = a.astype(BF16)
    b_hi = b.astype(BF16)
    a_lo = (a - a_hi.astype(F32)).astype(BF16)
    b_lo = (b - b_hi.astype(F32)).astype(BF16)
    return _dot(a_hi, b_hi) + _dot(a_hi, b_lo) + _dot(a_lo, b_hi)


HY_FILT_PACK = LANES // HY_BANDS


def _hy_filter_kernel(w1t_ref, w1c_ref, w1s_ref, b1_ref, fr_ref, w2_ref, b2_ref, w3_ref, bands_ref,
                      dl_ref, k_ref, sum_ref, *, L):
    tr = k_ref.shape[0]
    P, Hd = HY_FILT_PACK, HY_HIDDEN
    rq = tr // P
    r0 = pl.program_id(0) * tr

    def position(r):
        return jnp.where(r < L, r, 2 * L - r).astype(F32)

    def packed_rows(width, per):
        row = lax.broadcasted_iota(jnp.int32, (rq, width), 0)
        q = lax.broadcasted_iota(jnp.int32, (rq, width), 1) // per
        return r0 + q * rq + row

    bw = (position(packed_rows(LANES, HY_BANDS)) * (2.0 * math.pi / L)) * bands_ref[...]
    t_h = position(packed_rows(P * Hd, Hd)) * (1.0 / (L - 1))
    pre = t_h * w1t_ref[...] + _dot_hi(jnp.cos(bw), w1c_ref[...]) - _dot_hi(jnp.sin(bw), w1s_ref[...])
    fr = fr_ref[...]
    a = jnp.sin(fr * (pre + b1_ref[...]))
    a = jnp.sin(fr * (_dot_hi(a, w2_ref[...]) + b2_ref[...]))
    W2 = HY_ORDER * HY_WIDTH
    a_rows = jnp.concatenate([a[:, q * Hd:(q + 1) * Hd] for q in range(P)], axis=0)
    h = _dot_3x(a_rows, w3_ref[...])
    r = r0 + lax.broadcasted_iota(jnp.int32, (tr, 1), 0)
    decay = jnp.exp(-(position(r) * (1.0 / (L - 1))) * dl_ref[...])
    hf = h[:, :W2] * decay * (r < L).astype(F32)
    hb = h[:, W2:] * decay * ((r > L) | (r == 0)).astype(F32)
    k_ref[...] = hf + hb
    total = jnp.sum(jnp.abs(hf) + jnp.abs(hb), axis=0, keepdims=True)

    @pl.when(pl.program_id(0) == 0)
    def _():
        sum_ref[...] = jnp.zeros_like(sum_ref)

    sum_ref[...] += total


def _hy_filter(L, w1, b1, freq, w2, b2, w3):
    tr = min(HY_FILT_ROWS, L)
    W2 = HY_ORDER * HY_WIDTH
    P = HY_FILT_PACK
    eye = jnp.eye(P, dtype=F32)
    tile = lambda v: jnp.tile(v.reshape(1, -1), (1, P))
    bands = tile(jnp.linspace(1e-4, HY_BANDS - 1, HY_BANDS, dtype=F32))
    w1c = jnp.kron(eye, w1[1:1 + HY_BANDS])
    w1s = jnp.kron(eye, w1[1 + HY_BANDS:])
    w2b = jnp.kron(eye, w2)
    deltas = jnp.abs(jnp.linspace(math.log(HY_DECAY_TARGET) / HY_SLOW_PCT,
                                  math.log(HY_DECAY_TARGET) / HY_FAST_PCT, HY_WIDTH, dtype=F32))
    dl = jnp.tile(deltas, HY_ORDER).reshape(1, W2)
    args = (tile(w1[0]), w1c, w1s, tile(b1), tile(freq), w2b, tile(b2), w3, bands, dl)
    full = lambda a: pl.BlockSpec(a.shape, lambda i: (0, 0))
    return pl.pallas_call(
        functools.partial(_hy_filter_kernel, L=L),
        grid=(2 * L // tr,),
        in_specs=[full(a) for a in args],
        out_specs=[pl.BlockSpec((tr, W2), lambda i: (i, 0)), pl.BlockSpec((1, W2), lambda i: (0, 0))],
        out_shape=[jax.ShapeDtypeStruct((2 * L, W2), F32), jax.ShapeDtypeStruct((1, W2), F32)],
        compiler_params=pltpu.CompilerParams(dimension_semantics=("arbitrary",)),
        name="hyena_filter",
    )(*args)


def _hy_dft_tables(L):
    N = 2 * L
    N2 = HY_N2
    N1 = N // N2
    def cs(n, rows, cols):
        ang = (2.0 * math.pi / n) * ((jnp.arange(rows)[:, None] * jnp.arange(cols)[None, :]) % n).astype(F32)
        return jnp.cos(ang), jnp.sin(ang)
    ca, sa = cs(N1, N1, N1)
    fa_full = jnp.concatenate([ca, -sa], axis=0)
    fa_inv = jnp.concatenate([ca[:N1 // 2], -sa[:N1 // 2]], axis=1) * (1.0 / N)
    cb, sb = cs(N2, N2, N2)
    fb = jnp.concatenate([jnp.concatenate([cb, sb], axis=1), jnp.concatenate([-sb, cb], axis=1)], axis=0)
    tc, ts = cs(N, N1, N2)
    tw = jnp.stack([tc, ts])[..., None] * jnp.ones((1, 1, 1, 128), F32)
    tabs = dict(N1=N1, fa_half=fa_full[:, :N1 // 2].astype(BF16), fa_full=fa_full.astype(BF16),
                fa_inv=fa_inv.astype(BF16), fb=fb.astype(BF16), fb_t=fb.T.astype(BF16), tw=tw)
    if N1 <= HY_KRON_MAX_N1:
        eye = jnp.eye(SUB, dtype=F32)
        tabs['fa_half_kron'] = jnp.kron(fa_full[:, :N1 // 2], eye).astype(BF16)
        tabs['fa_inv_kron'] = jnp.kron(fa_inv, eye).astype(BF16)
    return tabs


def _hy_n2_rows(N1):
    return SUB * max(1, min(HY_N2 // SUB, 256 // N1))


def _hy_dft_a_kernel(m_ref, x_ref, o_ref):
    N1, C = o_ref.shape[2], o_ref.shape[4]
    for g0 in range(0, x_ref.shape[2], SUB):
        x = jnp.concatenate([x_ref[0, :, g0 + j, :] for j in range(SUB)], axis=1).astype(BF16)
        y = _dot(m_ref[...], x)
        for j in range(SUB):
            o_ref[0, 0, :, g0 + j, :] = y[:N1, j * C:(j + 1) * C]
            o_ref[0, 1, :, g0 + j, :] = y[N1:, j * C:(j + 1) * C]


def _hy_dft_a(m, x):
    B, R, N2, C = x.shape
    N1 = m.shape[0] // 2
    W = HY_WIDTH
    G = _hy_n2_rows(N1)
    return pl.pallas_call(
        _hy_dft_a_kernel,
        grid=(B, N2 // G, C // W),
        in_specs=[pl.BlockSpec(m.shape, lambda b, g, c: (0, 0)),
                  pl.BlockSpec((1, R, G, W), lambda b, g, c: (b, 0, g, c))],
        out_specs=pl.BlockSpec((1, 2, N1, G, W), lambda b, g, c: (b, 0, 0, g, c)),
        out_shape=jax.ShapeDtypeStruct((B, 2, N1, N2, C), F32),
        compiler_params=pltpu.CompilerParams(dimension_semantics=("parallel", "parallel", "parallel")),
        name="hyena_dft_a",
    )(m, x)


def _hy_gate_kernel(m_ref, y_ref, z_ref, g_ref, s_ref, o_ref):
    C = o_ref.shape[3]
    for g0 in range(0, o_ref.shape[2], SUB):
        y = jnp.concatenate(
            [jnp.concatenate([y_ref[0, 0, :, g0 + j, :], y_ref[0, 1, :, g0 + j, :]], axis=0) for j in range(SUB)],
            axis=1)
        zc = _dot(m_ref[...], y.astype(BF16))
        for j in range(SUB):
            o_ref[0, :, g0 + j, :] = g_ref[0, :, g0 + j, :] * (
                zc[:, j * C:(j + 1) * C] + s_ref[...] * z_ref[0, :, g0 + j, :])


def _hy_inverse_gate(m, y, z, gate, skip_row):
    B, _, N1, N2, C = y.shape
    G = _hy_n2_rows(N1)
    tok = pl.BlockSpec((1, N1 // 2, G, C), lambda b, g: (b, 0, g, 0))
    return pl.pallas_call(
        _hy_gate_kernel,
        grid=(B, N2 // G),
        in_specs=[pl.BlockSpec(m.shape, lambda b, g: (0, 0)),
                  pl.BlockSpec((1, 2, N1, G, C), lambda b, g: (b, 0, 0, g, 0)),
                  tok, tok,
                  pl.BlockSpec((1, C), lambda b, g: (0, 0))],
        out_specs=tok,
        out_shape=jax.ShapeDtypeStruct((B, N1 // 2, N2, C), F32),
        compiler_params=pltpu.CompilerParams(dimension_semantics=("parallel", "parallel")),
        name="hyena_idft_a_gate",
    )(m, y, z, gate, skip_row)


def _cmul(ar, ai, br, bi):
    return ar * br - ai * bi, ar * bi + ai * br


def _hy_stage_b_one(ar, ai, tcos, tsin, fb, fbt, kr, ki):
    N2 = HY_N2
    tc = jnp.concatenate([tcos, tcos], axis=1)
    ts = jnp.concatenate([tsin, tsin], axis=1)
    br, bi = _cmul(ar, ai, tc, -ts)
    x = _dot(fb, jnp.concatenate([br, bi], axis=0).astype(BF16))
    yr, yi = _cmul(x[:N2], x[N2:], kr, ki)
    y = _dot(fbt, jnp.concatenate([yr, yi], axis=0).astype(BF16))
    return _cmul(y[:N2], y[N2:], tc, ts)


def _hy_fused_kernel(ma_ref, mi_ref, tw_ref, fb_ref, fbt_ref, k_ref, z_ref, g_ref, s_ref, o_ref, w_sc):
    R, N2, C = z_ref.shape[1:]
    N1 = k_ref.shape[1]
    for g0 in range(0, N2, SUB):
        x = z_ref[0, :, g0:g0 + SUB, :].reshape(R * SUB, C).astype(BF16)
        w_sc[:, :, g0:g0 + SUB, :] = _dot(ma_ref[...], x).reshape(2, N1, SUB, C)

    def k1_step(j, carry):
        yr, yi = _hy_stage_b_one(w_sc[0, j], w_sc[1, j], tw_ref[0, j], tw_ref[1, j], fb_ref[...], fbt_ref[...],
                                 k_ref[0, j], k_ref[1, j])
        w_sc[0, j] = yr
        w_sc[1, j] = yi
        return carry

    lax.fori_loop(0, N1, k1_step, 0, unroll=2)
    for g0 in range(0, N2, SUB):
        y = w_sc[:, :, g0:g0 + SUB, :].reshape(2 * N1 * SUB, C).astype(BF16)
        zc = _dot(mi_ref[...], y).reshape(R, SUB, C)
        rows = slice(g0, g0 + SUB)
        o_ref[0, :, rows, :] = g_ref[0, :, rows, :] * (zc + s_ref[...] * z_ref[0, :, rows, :])


def _hy_fused_conv(tabs, kspec, order, z, gate, skip_row):
    B, R, N2, C = z.shape
    N1 = tabs['N1']
    once = lambda a: pl.BlockSpec(a.shape, lambda b: (0,) * a.ndim, pipeline_mode=pl.Buffered(1))
    tok = pl.BlockSpec((1, R, N2, C), lambda b: (b, 0, 0, 0))
    ma, mi = tabs['fa_half_kron'], tabs['fa_inv_kron']
    return pl.pallas_call(
        _hy_fused_kernel,
        grid=(B,),
        in_specs=[once(ma), once(mi), once(tabs['tw']), once(tabs['fb']), once(tabs['fb_t']),
                  pl.BlockSpec((2, N1, N2, C), lambda b: (0, 0, 0, order), pipeline_mode=pl.Buffered(1)),
                  tok, tok, pl.BlockSpec((1, C), lambda b: (0, 0))],
        out_specs=tok,
        out_shape=jax.ShapeDtypeStruct(z.shape, F32),
        scratch_shapes=[pltpu.VMEM((2, N1, N2, C), F32)],
        compiler_params=pltpu.CompilerParams(dimension_semantics=("parallel",)),
        name="hyena_conv_fused",
    )(ma, mi, tabs['tw'], tabs['fb'], tabs['fb_t'], kspec, z, gate, skip_row)


def _hy_stage_b_kernel(a_ref, tw_ref, fb_ref, fbt_ref, k_ref, o_ref, *, conv):
    N2 = HY_N2
    for j in range(a_ref.shape[2]):
        if conv:
            xr, xi = _hy_stage_b_one(a_ref[0, 0, j], a_ref[0, 1, j], tw_ref[0, j], tw_ref[1, j], fb_ref[...],
                                     fbt_ref[...], k_ref[0, j], k_ref[1, j])
        else:
            tc = jnp.concatenate([tw_ref[0, j], tw_ref[0, j]], axis=1)
            ts = jnp.concatenate([tw_ref[1, j], tw_ref[1, j]], axis=1)
            br, bi = _cmul(a_ref[0, 0, j], a_ref[0, 1, j], tc, -ts)
            x = _dot(fb_ref[...], jnp.concatenate([br, bi], axis=0).astype(BF16))
            xr, xi = x[:N2] * k_ref[...], x[N2:] * k_ref[...]
        o_ref[0, 0, j] = xr
        o_ref[0, 1, j] = xi


def _hy_stage_b(a, tabs, kspec=None, order=0, scale=None):
    B, _, N1, N2, C = a.shape
    t1 = HY_K1_TILE
    conv = kspec is not None
    blk = pl.BlockSpec((1, 2, t1, N2, HY_WIDTH), lambda i, b, c: (b, 0, i, 0, c))
    if conv:
        k_arr = kspec
        k_spec = pl.BlockSpec((2, t1, N2, HY_WIDTH), lambda i, b, c: (0, i, 0, order))
    else:
        k_arr = scale
        k_spec = pl.BlockSpec((1, HY_WIDTH), lambda i, b, c: (0, c))
    return pl.pallas_call(
        functools.partial(_hy_stage_b_kernel, conv=conv),
        grid=(N1 // t1, B, C // HY_WIDTH),
        in_specs=[blk,
                  pl.BlockSpec((2, t1, N2, 128), lambda i, b, c: (0, i, 0, 0)),
                  pl.BlockSpec(tabs['fb'].shape, lambda i, b, c: (0, 0)),
                  pl.BlockSpec(tabs['fb_t'].shape, lambda i, b, c: (0, 0)),
                  k_spec],
        out_specs=blk,
        out_shape=jax.ShapeDtypeStruct(a.shape, F32),
        compiler_params=pltpu.CompilerParams(dimension_semantics=("parallel", "parallel", "parallel")),
        name="hyena_stage_b_conv" if conv else "hyena_stage_b_spectrum",
    )(a, tabs['tw'], tabs['fb'], tabs['fb_t'], k_arr)


def _hy_filter_spectrum(L, tabs, w1, b1, freq, w2, b2, w3):
    N1, N2 = tabs['N1'], HY_N2
    W2 = HY_ORDER * HY_WIDTH
    k, total = _hy_filter(L, w1, b1, freq, w2, b2, w3)
    a = _hy_dft_a(tabs['fa_full'], k.reshape(1, N1, N2, W2))
    return _hy_stage_b(a, tabs, scale=1.0 / total)[0]


def _hyena(v, x1, x2, skip, tabs, kspec):
    B, L, _ = v.shape
    N1, N2, C = tabs['N1'], HY_N2, HY_WIDTH
    grid = lambda t: t.reshape(B, N1 // 2, N2, C)
    z = grid(v)
    for n, gate in enumerate((x1, x2)):
        if 'fa_half_kron' in tabs:
            z = _hy_fused_conv(tabs, kspec, n, z, grid(gate), skip[n].reshape(1, C))
        else:
            a = _hy_dft_a(tabs['fa_half'], z)
            y = _hy_stage_b(a, tabs, kspec, order=n)
            z = _hy_inverse_gate(tabs['fa_inv'], y, z, grid(gate), skip[n].reshape(1, C))
    return z.reshape(B, L, C)


def _prep_layer_weights(l, w_in, w_out, na_rpb, mla_w_uq, mla_w_ukv, ffn_w_up, ffn_conv, ffn_w_down):
    wi = w_in[l]
    s0, s1, s2 = NA_COLS, NA_COLS + HG_COLS, NA_COLS + HG_COLS + HY_COLS
    w_ml = jnp.pad(wi[:, s2:], ((0, 0), (0, MLA_COLS_PAD - MLA_COLS)))
    return dict(
        w_na=wi[:, :s0].astype(BF16), w_hg=wi[:, s0:s1].astype(BF16), w_hy=wi[:, s1:s2].astype(BF16),
        w_ml=w_ml.astype(BF16), w_out=w_out[l].astype(BF16), na_bias=_na_bias_table(na_rpb[l]),
        mla=_mla_weights(mla_w_uq[l], mla_w_ukv[l]),
        ffn=_ffn_weight_tiles(ffn_w_up[l], ffn_conv[l], ffn_w_down[l]),
    )


def _trunk_layer(x, mod, lb, g, lw, p, cos, sin, hy_tabs, hy_spec):
    pa, pb, hv, hx1, hx2, pd = _in_proj(x, mod, g, lw['w_na'], lw['w_hg'], lw['w_hy'], lw['w_ml'], p['hy_short'])
    o_na = _neighbourhood_attention(pa, lw['na_bias'])
    hg_dirs = _hgrn2_scan(pb, lb)
    o_hy = _hyena(hv, hx1, hx2, p['hy_skip'], hy_tabs, hy_spec)
    L = x.shape[1]
    q, k, v = _mla_prep(pd, cos[:L], sin[:L], p['mla_q_norm'], p['mla_kv_norm'], lw['mla'])
    o_ml = _mla_flash(q, k, v)
    x1, h2 = _out_proj(x, mod, g, o_na, hg_dirs, pb, p['hg_norm'], o_hy, o_ml, lw['w_out'])
    return _conv_ffn(h2, x1, mod, g, *lw['ffn'])


def kernel(x_prompt, x_sample, c_prompt, c_sample, ada_w, ada_b, norm_g, w_in, w_out, na_rpb, hg_lb,
           hg_norm, hy_short, hy_w1, hy_b1, hy_freq, hy_w2, hy_b2, hy_w3, hy_skip, mla_q_norm,
           mla_kv_norm, mla_w_uq, mla_w_ukv, ffn_w_up, ffn_conv, ffn_w_down):
    Bp, Bs = x_prompt.shape[0], x_sample.shape[0]
    Lp, Ls = x_prompt.shape[1], x_sample.shape[1]
    lb_soft = jax.nn.softmax(hg_lb.astype(F32), axis=0)
    lower_bounds = jnp.cumsum(lb_soft, axis=0) - lb_soft[:1]

    R = -(-(Bp + Bs) // 8) * 8
    cond = jnp.zeros((R, D_MODEL), F32).at[:Bp].set(c_prompt).at[Bp:Bp + Bs].set(c_sample)
    mod = _ada_modulation(cond, ada_w, ada_b).reshape(DEPTH, R, 6, D_MODEL)
    mod = jnp.pad(mod, ((0, 0), (0, 0), (0, 2), (0, 0)))

    cos, sin = _rope_tables(max(Lp, Ls))
    tabs_p, tabs_s = _hy_dft_tables(Lp), _hy_dft_tables(Ls)
    y_prompt, y_sample = x_prompt, x_sample
    for l in range(DEPTH):
        lw = _prep_layer_weights(l, w_in, w_out, na_rpb, mla_w_uq, mla_w_ukv, ffn_w_up, ffn_conv, ffn_w_down)
        p = dict(hg_norm=hg_norm[l], hy_short=hy_short[l], hy_skip=hy_skip[l],
                 mla_q_norm=mla_q_norm[l].reshape(1, -1), mla_kv_norm=mla_kv_norm[l].reshape(1, -1))
        filt = (hy_w1[l], hy_b1[l], hy_freq[l], hy_w2[l], hy_b2[l], hy_w3[l])
        spec_p = _hy_filter_spectrum(Lp, tabs_p, *filt)
        spec_s = _hy_filter_spectrum(Ls, tabs_s, *filt)
        y_prompt = _trunk_layer(y_prompt, mod[l, :Bp], lower_bounds[l], norm_g[l], lw, p, cos, sin,
                                tabs_p, spec_p)
        y_sample = _trunk_layer(y_sample, mod[l, Bp:Bp + Bs], lower_bounds[l], norm_g[l], lw, p, cos, sin,
                                tabs_s, spec_s)
    return (y_prompt, y_sample)
```

```python
import functools
import math

import jax
import jax.numpy as jnp
from jax import lax
from jax.experimental import pallas as pl
from jax.experimental.pallas import tpu as pltpu

F32 = jnp.float32
BF16 = jnp.bfloat16

D_MODEL = 1024
DEPTH = 4
GRID_W = 64
HEAD_DIM = 64
NA_WIDTH = 256
NA_HEADS = 4
NA_KH = 8
NA_KW = 16
HG_WIDTH = 256
HG_HEADS = 4
HY_WIDTH = 256
HY_ORDER = 2
HY_BANDS = 16
HY_HIDDEN = 64
HY_DECAY_TARGET = 1e-2
HY_FAST_PCT = 0.3
HY_SLOW_PCT = 1.5
MLA_WIDTH = 256
MLA_HEADS = 4
MLA_NOPE = 64
MLA_ROPE = 32
MLA_V = 64
MLA_Q_LORA = 256
MLA_KV_LORA = 128
MLA_HEAD_PAD = 128
ROPE_THETA = 10000.0
D_FF = 2816
EPS = 1e-6
NA_COLS = 3 * NA_WIDTH
HG_COLS = 5 * HG_WIDTH
HY_COLS = 3 * HY_WIDTH
MLA_COLS = MLA_Q_LORA + MLA_KV_LORA + MLA_ROPE
MLA_COLS_PAD = 512
NEG_BIG = -1e30
SUB = 8
LANES = 128


def _dot(a, b):
    return jnp.dot(a, b, preferred_element_type=F32)


def _dot_nt(a, b):
    return lax.dot_general(a, b, (((1,), (1,)), ((), ())), preferred_element_type=F32)


def _rms(x, g):
    return x * lax.rsqrt(jnp.mean(x * x, axis=-1, keepdims=True) + EPS) * g


def _ada_kernel(c_ref, w_ref, b_ref, o_ref):
    c = c_ref[...]
    s = c * jax.nn.sigmoid(c)
    o_ref[0] = _dot(s.astype(BF16), w_ref[0].astype(BF16)) + b_ref[0]


def _ada_modulation(cond, ada_w, ada_b):
    R = cond.shape[0]
    tn = 1536
    return pl.pallas_call(
        _ada_kernel,
        grid=(DEPTH, 6 * D_MODEL // tn),
        in_specs=[
            pl.BlockSpec((R, D_MODEL), lambda l, n: (0, 0)),
            pl.BlockSpec((1, D_MODEL, tn), lambda l, n: (l, 0, n)),
            pl.BlockSpec((1, 1, tn), lambda l, n: (l, 0, n)),
        ],
        out_specs=pl.BlockSpec((1, R, tn), lambda l, n: (l, 0, n)),
        out_shape=jax.ShapeDtypeStruct((DEPTH, R, 6 * D_MODEL), F32),
        name="ada_modulation",
    )(cond, ada_w, ada_b.reshape(DEPTH, 1, 6 * D_MODEL))


IN_HALO = 16


def _in_proj_kernel(x_ref, xp_ref, xn_ref, mod_ref, g_ref, wna, whg, why, wml, sw_ref,
                    cos_ref, sin_ref, qn_ref, kvn_ref, wq, wqr, wk, wv, we, wer,
                    ona, ohg, ov, ox1, ox2, oq, ok, ovv, hext_ref):
    i = pl.program_id(1)
    tm = x_ref.shape[1]
    H = IN_HALO
    g = g_ref[0:1, :]
    scale = 1.0 + mod_ref[0, 1:2, :]
    shift = mod_ref[0, 0:1, :]
    prep = lambda x: (_rms(x, g) * scale + shift).astype(BF16)
    hext_ref[0:H, :] = prep(xp_ref[0])
    hext_ref[H:H + tm, :] = prep(x_ref[0])
    hext_ref[H + tm:2 * H + tm, :] = prep(xn_ref[0])
    hb = hext_ref[H:H + tm, :]
    ona[0] = _dot(hb, wna[...])
    ohg[0] = _dot(hb, whg[...])
    _mla_qkv_tile(_dot(hb, wml[...]), cos_ref[...], sin_ref[...], qn_ref[...], kvn_ref[...], wq[...], wqr[...],
                  wk[...], wv[...], we[...], wer[...], oq, ok, ovv)
    ue = _dot(hext_ref[...], why[...])
    u = ue[H:H + tm]
    u_prev = ue[H - 1:H] * (i > 0).astype(F32)
    u_next = ue[H + tm:H + tm + 1] * (i < pl.num_programs(1) - 1).astype(F32)
    sub = lax.broadcasted_iota(jnp.int32, (SUB, 1), 0)
    down = pltpu.roll(u, 1, 0)
    up = pltpu.roll(u, tm - 1, 0)
    down = jnp.concatenate([jnp.where(sub == 0, u_prev, down[:SUB]), down[SUB:]], axis=0)
    up = jnp.concatenate([up[:tm - SUB], jnp.where(sub == SUB - 1, u_next, up[tm - SUB:])], axis=0)
    y = sw_ref[0:1, :] * down + sw_ref[1:2, :] * u + sw_ref[2:3, :] * up
    W = HY_WIDTH
    ov[0] = y[:, 0 * W:1 * W]
    ox1[0] = y[:, 1 * W:2 * W]
    ox2[0] = y[:, 2 * W:3 * W]


def _in_proj(x, mod, g, w_na, w_hg, w_hy, w_ml, short_w, cos, sin, q_norm, kv_norm, mla_w, tm=512):
    B, L, D = x.shape
    tm = min(tm, L)
    nh = tm // IN_HALO
    last = L // IN_HALO - 1
    full = lambda a: pl.BlockSpec(a.shape, lambda b, i: (0, 0))
    rope = pl.BlockSpec((tm, MLA_HEAD_PAD), lambda b, i: (i, 0))
    HP = MLA_HEADS * MLA_HEAD_PAD
    outs = [(NA_COLS, F32), (HG_COLS, F32), (HY_WIDTH, F32), (HY_WIDTH, F32), (HY_WIDTH, F32),
            (HP, BF16), (HP, BF16), (MLA_WIDTH, BF16)]
    return pl.pallas_call(
        _in_proj_kernel,
        grid=(B, L // tm),
        in_specs=[
            pl.BlockSpec((1, tm, D), lambda b, i: (b, i, 0)),
            pl.BlockSpec((1, IN_HALO, D), lambda b, i: (b, jnp.maximum(i * nh - 1, 0), 0)),
            pl.BlockSpec((1, IN_HALO, D), lambda b, i: (b, jnp.minimum((i + 1) * nh, last), 0)),
            pl.BlockSpec((1, 8, D), lambda b, i: (b, 0, 0)),
            full(g), full(w_na), full(w_hg), full(w_hy), full(w_ml), full(short_w),
            rope, rope, full(q_norm), full(kv_norm),
        ] + [full(w) for w in mla_w],
        out_specs=[pl.BlockSpec((1, tm, n), lambda b, i: (b, i, 0)) for n, _ in outs],
        out_shape=[jax.ShapeDtypeStruct((B, L, n), dt) for n, dt in outs],
        scratch_shapes=[pltpu.VMEM((tm + 2 * IN_HALO, D), BF16)],
        compiler_params=pltpu.CompilerParams(dimension_semantics=("parallel", "parallel")),
        name="in_proj",
    )(x, x, x, mod, g, w_na, w_hg, w_hy, w_ml, short_w, cos, sin, q_norm, kv_norm, *mla_w)


def _head_mean_sq(o, sel):
    sq = o * o
    hi = sq.astype(BF16)
    lo = (sq - hi.astype(F32)).astype(BF16)
    return (_dot(hi, sel) + _dot(lo, sel)) * (1.0 / (HG_WIDTH // HG_HEADS))


def _out_proj_kernel(x_ref, mod_ref, g_ref, na, hgf, hgb, hgg, hgn, sel, hy, ml, w_ref, x1_ref, h2_ref):
    W = NA_WIDTH
    o = hgf[0] + hgb[0]
    gate = hgg[0]
    hg = o * lax.rsqrt(_head_mean_sq(o, sel[...]) + EPS) * hgn[...] * (gate * jax.nn.sigmoid(gate))
    mix = _dot(na[0], w_ref[0 * W:1 * W, :])
    mix += _dot(hg.astype(BF16), w_ref[1 * W:2 * W, :])
    mix += _dot(hy[0].astype(BF16), w_ref[2 * W:3 * W, :])
    mix += _dot(ml[0], w_ref[3 * W:4 * W, :])
    x1 = x_ref[0] + mod_ref[0, 2:3, :] * _rms(mix, g_ref[1:2, :])
    x1_ref[0] = x1
    h2 = _rms(x1, g_ref[2:3, :]) * (1.0 + mod_ref[0, 4:5, :]) + mod_ref[0, 3:4, :]
    h2_ref[0] = h2.astype(BF16)


def _out_proj(x, mod, g, o_na, hg_dirs, pb, hg_norm, o_hy, o_ml, w_out, tm=512):
    B, L, D = x.shape
    tm = min(tm, L)
    tok = lambda n: pl.BlockSpec((1, tm, n), lambda b, i: (b, i, 0))
    sel = _hgrn2_tables()[1]
    return pl.pallas_call(
        _out_proj_kernel,
        grid=(B, L // tm),
        in_specs=[
            tok(D),
            pl.BlockSpec((1, 8, D), lambda b, i: (b, 0, 0)),
            pl.BlockSpec(g.shape, lambda b, i: (0, 0)),
            tok(NA_WIDTH),
            tok(HG_WIDTH), tok(HG_WIDTH),
            pl.BlockSpec((1, tm, HG_WIDTH), lambda b, i: (b, i, 4)),
            pl.BlockSpec((1, HG_WIDTH), lambda b, i: (0, 0)),
            pl.BlockSpec(sel.shape, lambda b, i: (0, 0)),
            tok(HY_WIDTH), tok(MLA_WIDTH),
            pl.BlockSpec(w_out.shape, lambda b, i: (0, 0)),
        ],
        out_specs=[tok(D), tok(D)],
        out_shape=[jax.ShapeDtypeStruct((B, L, D), F32), jax.ShapeDtypeStruct((B, L, D), BF16)],
        compiler_params=pltpu.CompilerParams(dimension_semantics=("parallel", "parallel")),
        name="out_proj",
    )(x, mod, g, o_na, hg_dirs[0], hg_dirs[1], pb, hg_norm.reshape(1, HG_WIDTH), sel, o_hy, o_ml, w_out)


FFN_HALO = 16


def _ffn_kernel(hm_ref, hp_ref, hn_ref, x_ref, mod_ref, g_ref, wu_ref, cw_ref, wd_ref, o_ref, acc_ref,
                hext_ref):
    i = pl.program_id(1)
    tm = hm_ref.shape[1]
    nf = wd_ref.shape[0]
    not_first = (i > 0).astype(F32)
    not_last = (i < pl.num_programs(1) - 1).astype(F32)
    sub = lax.broadcasted_iota(jnp.int32, (SUB, 1), 0)

    H = FFN_HALO
    hext_ref[0:H, :] = hp_ref[0]
    hext_ref[H:H + tm, :] = hm_ref[0]
    hext_ref[H + tm:2 * H + tm, :] = hn_ref[0]

    def conv_branch(t):
        c = cw_ref[t]
        ue = _dot(hext_ref[...], wu_ref[t])
        u = ue[H:H + tm]
        u_prev = ue[H - 1:H] * not_first
        u_next = ue[H + tm:H + tm + 1] * not_last
        down = pltpu.roll(u, 1, 0)
        up = pltpu.roll(u, tm - 1, 0)
        down = jnp.concatenate([jnp.where(sub == 0, u_prev, down[:SUB]), down[SUB:]], axis=0)
        up = jnp.concatenate([up[:tm - SUB], jnp.where(sub == SUB - 1, u_next, up[tm - SUB:])], axis=0)
        return c[0:1, :] * down + c[1:2, :] * u + c[2:3, :] * up

    def tile(f, carry):
        a = conv_branch(f)
        b = conv_branch(nf + f)
        act = (jax.nn.gelu(a, approximate=True) * b).astype(BF16)
        acc_ref[...] += _dot(act, wd_ref[f])
        return carry

    acc_ref[...] = jnp.zeros_like(acc_ref)
    lax.fori_loop(0, nf, tile, 0, unroll=FFN_UNROLL)
    o_ref[0] = x_ref[0] + mod_ref[0, 5:6, :] * _rms(acc_ref[...], g_ref[3:4, :])


FFN_TF = 256
FFN_UNROLL = 2


def _conv_ffn(h2, x1, mod, g, w_up, conv_w, w_down, tm=1024):
    B, L, D = x1.shape
    tm = min(tm, L)
    nh = tm // FFN_HALO
    last_halo = L // FFN_HALO - 1
    once = lambda a: pl.BlockSpec(a.shape, lambda b, i: (0,) * a.ndim, pipeline_mode=pl.Buffered(1))
    return pl.pallas_call(
        _ffn_kernel,
        grid=(B, L // tm),
        in_specs=[
            pl.BlockSpec((1, tm, D), lambda b, i: (b, i, 0)),
            pl.BlockSpec((1, FFN_HALO, D), lambda b, i: (b, jnp.maximum(i * nh - 1, 0), 0)),
            pl.BlockSpec((1, FFN_HALO, D), lambda b, i: (b, jnp.minimum((i + 1) * nh, last_halo), 0)),
            pl.BlockSpec((1, tm, D), lambda b, i: (b, i, 0)),
            pl.BlockSpec((1, 8, D), lambda b, i: (b, 0, 0)),
            pl.BlockSpec(g.shape, lambda b, i: (0, 0)),
            once(w_up), once(conv_w), once(w_down),
        ],
        out_specs=pl.BlockSpec((1, tm, D), lambda b, i: (b, i, 0)),
        out_shape=jax.ShapeDtypeStruct((B, L, D), F32),
        scratch_shapes=[pltpu.VMEM((tm, D), F32), pltpu.VMEM((tm + 2 * FFN_HALO, D), BF16)],
        compiler_params=pltpu.CompilerParams(dimension_semantics=("parallel", "parallel")),
        name="conv_ffn",
    )(h2, h2, h2, x1, mod, g, w_up, conv_w, w_down)


def _ffn_weight_tiles(w_up, conv_w, w_down):
    D = w_up.shape[0]
    n2 = 2 * D_FF // FFN_TF
    wu = w_up.reshape(D, n2, FFN_TF).transpose(1, 0, 2).astype(BF16)
    cw = conv_w.reshape(3, n2, FFN_TF).transpose(1, 0, 2)
    wd = w_down.reshape(D_FF // FFN_TF, FFN_TF, D).astype(BF16)
    return wu, cw, wd


NA_RB = 8


def _na_kernel(q_ref, kp_ref, kc_ref, kn_ref, vp_ref, vc_ref, vn_ref, bias_ref, o_ref,
               kbuf, vbuf, *, rows):
    i = pl.program_id(1)
    blk = NA_RB * GRID_W
    win = NA_KH * GRID_W
    kbuf[0 * blk:1 * blk, :] = kp_ref[0].astype(BF16)
    kbuf[1 * blk:2 * blk, :] = kc_ref[0].astype(BF16)
    kbuf[2 * blk:3 * blk, :] = kn_ref[0].astype(BF16)
    vbuf[0 * blk:1 * blk, :] = vp_ref[0].astype(BF16)
    vbuf[1 * blk:2 * blk, :] = vc_ref[0].astype(BF16)
    vbuf[2 * blk:3 * blk, :] = vn_ref[0].astype(BF16)
    scale = HEAD_DIM ** -0.5 * math.log2(math.e)
    lane_head = lax.broadcasted_iota(jnp.int32, (1, NA_WIDTH), 1) // HEAD_DIM
    for j in range(NA_RB):
        r = i * NA_RB + j
        start = jnp.clip(r - NA_KH // 2, 0, rows - NA_KH)
        loc = start - (i - 1) * NA_RB
        dr0 = start - r + (NA_KH - 1)
        off = pl.multiple_of(loc * GRID_W, GRID_W)
        kw = kbuf[pl.ds(off, win), :]
        vw = vbuf[pl.ds(off, win), :]
        qj = (q_ref[0, j * GRID_W:(j + 1) * GRID_W, :] * scale).astype(BF16)
        qbd = jnp.concatenate([jnp.where(lane_head == h, qj, jnp.zeros_like(qj)) for h in range(NA_HEADS)],
                              axis=0)
        s = _dot_nt(qbd, kw) + bias_ref[dr0]
        m = jnp.max(s, axis=-1, keepdims=True)
        p = jnp.exp2(s - m)
        l = jnp.sum(p, axis=-1, keepdims=True)
        pv = _dot(p.astype(BF16), vw) / l
        o = None
        for h in range(NA_HEADS):
            part = jnp.where(lane_head == h, pv[h * GRID_W:(h + 1) * GRID_W], 0.0)
            o = part if o is None else o + part
        o_ref[0, j * GRID_W:(j + 1) * GRID_W, :] = o.astype(o_ref.dtype)


def _na_bias_table(rpb):
    c = jnp.arange(GRID_W)
    dc = jnp.clip(c[None, :] - c[:, None] + (NA_KW - 1), 0, 2 * NA_KW - 2)
    col_start = jnp.clip(c - NA_KW // 2, 0, GRID_W - NA_KW)
    ok = (c[None, :] >= col_start[:, None]) & (c[None, :] < col_start[:, None] + NA_KW)
    e = jnp.where(ok[None, None], rpb.astype(F32)[:, :, dc], NEG_BIG)
    t = jnp.stack([e[:, d0:d0 + NA_KH] for d0 in range(NA_KH)], axis=1)
    t = t.transpose(1, 0, 3, 2, 4).reshape(NA_KH, NA_HEADS * GRID_W, NA_KH * GRID_W)
    return jnp.where(t > 0.5 * NEG_BIG, t * math.log2(math.e), NEG_BIG)


def _neighbourhood_attention(pa, bias):
    B, L, _ = pa.shape
    rows = L // GRID_W
    blk = NA_RB * GRID_W
    nblk = rows // NA_RB
    spec = lambda col, shift: pl.BlockSpec(
        (1, blk, NA_WIDTH), lambda b, i: (b, jnp.clip(i + shift, 0, nblk - 1), col))
    return pl.pallas_call(
        functools.partial(_na_kernel, rows=rows),
        grid=(B, nblk),
        in_specs=[spec(0, 0), spec(1, -1), spec(1, 0), spec(1, 1), spec(2, -1), spec(2, 0), spec(2, 1),
                  pl.BlockSpec(bias.shape, lambda b, i: (0, 0, 0))],
        out_specs=pl.BlockSpec((1, blk, NA_WIDTH), lambda b, i: (b, i, 0)),
        out_shape=jax.ShapeDtypeStruct((B, L, NA_WIDTH), BF16),
        scratch_shapes=[pltpu.VMEM((3 * blk, NA_WIDTH), BF16), pltpu.VMEM((3 * blk, NA_WIDTH), BF16)],
        compiler_params=pltpu.CompilerParams(dimension_semantics=("parallel", "parallel")),
        name="neighbourhood_attention",
    )(pa, pa, pa, pa, pa, pa, pa, bias)


def _rope_table_kernel(inv_ref, cos_ref, sin_ref):
    t = cos_ref.shape[0]
    pos = (pl.program_id(0) * t + lax.broadcasted_iota(jnp.int32, (t, MLA_HEAD_PAD), 0)).astype(F32)
    ang = pos * inv_ref[...]
    cos_ref[...] = jnp.cos(ang)
    sin_ref[...] = jnp.sin(ang)


def _rope_tables(L):
    half = MLA_ROPE // 2
    inv = ROPE_THETA ** (-jnp.arange(half, dtype=F32) / half)
    inv_row = jnp.zeros((1, MLA_HEAD_PAD), F32).at[0, MLA_NOPE:MLA_NOPE + MLA_ROPE].set(jnp.tile(inv, 2))
    t = min(L, 1024)
    return pl.pallas_call(
        _rope_table_kernel,
        grid=(L // t,),
        in_specs=[pl.BlockSpec((1, MLA_HEAD_PAD), lambda i: (0, 0))],
        out_specs=[pl.BlockSpec((t, MLA_HEAD_PAD), lambda i: (i, 0))] * 2,
        out_shape=[jax.ShapeDtypeStruct((L, MLA_HEAD_PAD), F32)] * 2,
        name="rope_tables",
    )(inv_row)


def _mla_qkv_tile(pd, cos, sin, q_norm, kv_norm, wq, wqr, wk, wv, we, wer, q_out, k_out, v_out):
    nq = _rms(pd[:, :MLA_Q_LORA], q_norm).astype(BF16)
    nkv = _rms(pd[:, MLA_Q_LORA:MLA_Q_LORA + MLA_KV_LORA], kv_norm).astype(BF16)
    kr = pd[:, MLA_Q_LORA + MLA_KV_LORA:].astype(BF16)
    q = _dot(nq, wq)
    q_rot = _dot(nq, wqr)
    k = _dot(nkv, wk) + _dot(kr, we)
    k_rot = _dot(kr, wer)
    scale = (MLA_NOPE + MLA_ROPE) ** -0.5 * math.log2(math.e)
    for h in range(MLA_HEADS):
        hs = slice(h * MLA_HEAD_PAD, (h + 1) * MLA_HEAD_PAD)
        q_out[0, :, hs] = ((q[:, hs] * cos + q_rot[:, hs] * sin) * scale).astype(BF16)
        k_out[0, :, hs] = (k[:, hs] * cos + k_rot[:, hs] * sin).astype(BF16)
    v_out[0] = _dot(nkv, wv).astype(BF16)


def _mla_weights(w_uq, w_ukv):
    half = MLA_ROPE // 2
    P = jnp.zeros((MLA_ROPE, MLA_ROPE), F32)
    P = P.at[jnp.arange(half) + half, jnp.arange(half)].set(-1.0)
    P = P.at[jnp.arange(half), jnp.arange(half) + half].set(1.0)
    HP = MLA_HEAD_PAD
    wq = jnp.zeros((MLA_Q_LORA, MLA_HEADS * HP), F32)
    wqr = jnp.zeros_like(wq)
    wk = jnp.zeros((MLA_KV_LORA, MLA_HEADS * HP), F32)
    wv = jnp.zeros((MLA_KV_LORA, MLA_HEADS * MLA_V), F32)
    we = jnp.zeros((MLA_COLS_PAD - MLA_Q_LORA - MLA_KV_LORA, MLA_HEADS * HP), F32)
    wer = jnp.zeros_like(we)
    eye = jnp.eye(MLA_ROPE, dtype=F32)
    for h in range(MLA_HEADS):
        qh = w_uq[:, h * (MLA_NOPE + MLA_ROPE):(h + 1) * (MLA_NOPE + MLA_ROPE)]
        wq = wq.at[:, h * HP:h * HP + MLA_NOPE + MLA_ROPE].set(qh)
        wqr = wqr.at[:, h * HP + MLA_NOPE:h * HP + MLA_NOPE + MLA_ROPE].set(qh[:, MLA_NOPE:] @ P)
        kvh = w_ukv[:, h * (MLA_NOPE + MLA_V):(h + 1) * (MLA_NOPE + MLA_V)]
        wk = wk.at[:, h * HP:h * HP + MLA_NOPE].set(kvh[:, :MLA_NOPE])
        wv = wv.at[:, h * MLA_V:(h + 1) * MLA_V].set(kvh[:, MLA_NOPE:])
        we = we.at[:MLA_ROPE, h * HP + MLA_NOPE:h * HP + MLA_NOPE + MLA_ROPE].set(eye)
        wer = wer.at[:MLA_ROPE, h * HP + MLA_NOPE:h * HP + MLA_NOPE + MLA_ROPE].set(P)
    return tuple(w.astype(BF16) for w in (wq, wqr, wk, wv, we, wer))


FLASH_ROWS = 16
FLASH_HEADS = 4


def _flash_kernel(q_ref, k_ref, v_ref, o_ref, m_sc, acc_sc, p_sc):
    kv = pl.program_id(3)
    _, tq, tk = p_sc.shape
    RB = FLASH_ROWS

    @pl.when(kv == 0)
    def _():
        m_sc[...] = jnp.full_like(m_sc, -jnp.inf)
        acc_sc[...] = jnp.zeros_like(acc_sc)

    first_head = lax.broadcasted_iota(jnp.int32, (1, LANES), 1) < MLA_V
    for h in range(FLASH_HEADS):
        q = q_ref[0, :, h * MLA_HEAD_PAD:(h + 1) * MLA_HEAD_PAD]
        k = k_ref[0, :, h * MLA_HEAD_PAD:(h + 1) * MLA_HEAD_PAD]
        s = _dot_nt(q, k)
        alphas = []
        for r in range(tq // RB):
            rows = slice(r * RB, (r + 1) * RB)
            x = s[rows, 0:LANES]
            for c in range(1, tk // LANES):
                x = jnp.maximum(x, s[rows, c * LANES:(c + 1) * LANES])
            m_prev = m_sc[h, rows, :]
            m_new = jnp.maximum(m_prev, jnp.max(x, axis=-1, keepdims=True))
            alphas.append(jnp.exp2(m_prev - m_new))
            m_sc[h, rows, :] = m_new
            for c in range(tk // LANES):
                cols = slice(c * LANES, (c + 1) * LANES)
                p_sc[h, rows, cols] = jnp.exp2((s[rows, cols] - m_new).astype(BF16))
        v = v_ref[0, :, (h // 2) * LANES:(h // 2 + 1) * LANES]
        ones = jnp.ones_like(v)
        v_aug = jnp.where(first_head, v, ones) if h % 2 == 0 else jnp.where(first_head, ones, v)
        acc_sc[h] = jnp.concatenate(alphas, axis=0) * acc_sc[h] + _dot(p_sc[h], v_aug)

    @pl.when(kv == pl.num_programs(3) - 1)
    def _():
        for g in range(FLASH_HEADS // 2):
            a0 = acc_sc[2 * g]
            a1 = acc_sc[2 * g + 1]
            num = jnp.where(first_head, a0, a1)
            den = jnp.where(first_head, pltpu.roll(a0, MLA_V, 1), pltpu.roll(a1, MLA_V, 1))
            o_ref[0, :, g * LANES:(g + 1) * LANES] = (num / den).astype(o_ref.dtype)


def _mla_flash(q, k, v, tq=512, tk=2048):
    B, L, _ = q.shape
    tq = min(tq, L)
    tk = min(tk, L)
    nh = FLASH_HEADS
    stat = pltpu.VMEM((nh, tq, LANES), F32)
    return pl.pallas_call(
        _flash_kernel,
        grid=(B, MLA_HEADS // nh, L // tq, L // tk),
        in_specs=[
            pl.BlockSpec((1, tq, nh * MLA_HEAD_PAD), lambda b, h, i, j: (b, i, h)),
            pl.BlockSpec((1, tk, nh * MLA_HEAD_PAD), lambda b, h, i, j: (b, j, h)),
            pl.BlockSpec((1, tk, nh * MLA_V), lambda b, h, i, j: (b, j, h)),
        ],
        out_specs=pl.BlockSpec((1, tq, nh * MLA_V), lambda b, h, i, j: (b, i, h)),
        out_shape=jax.ShapeDtypeStruct((B, L, MLA_WIDTH), BF16),
        scratch_shapes=[stat, stat, pltpu.VMEM((nh, tq, tk), BF16)],
        compiler_params=pltpu.CompilerParams(
            dimension_semantics=("parallel", "parallel", "parallel", "arbitrary")),
        name="mla_flash",
    )(q, k, v)


HG_T = 32
HG_TB = 2048


def _hgrn2_kernel(qf_ref, qb_ref, ff_ref, fb_ref, if_ref, ib_ref, lb_ref, tri_ref, sel_ref, bd_ref,
                  of_ref, ob_ref, st_ref, g_ref, *, nsub):
    T = HG_T
    ngrp = T // SUB

    @pl.when(pl.program_id(1) == 0)
    def _():
        st_ref[...] = jnp.zeros_like(st_ref)

    sel = sel_ref[...]
    row = lax.broadcasted_iota(jnp.int32, (SUB, HG_WIDTH), 0)

    def step(c, backward):
        z = 1 if backward else 0
        q_ref, f_ref, i_ref, o_ref = (qb_ref, fb_ref, ib_ref, ob_ref) if backward else (qf_ref, ff_ref, if_ref, of_ref)
        lb = lb_ref[z]
        r0 = pl.multiple_of(c * T, T)
        qs = q_ref[0, pl.ds(r0, T), :]
        qs = qs * jax.nn.sigmoid(qs)
        fg = lb + (1.0 - lb) * jax.nn.sigmoid(f_ref[0, pl.ds(r0, T), :])
        kk = 1.0 - fg
        vv = i_ref[0, pl.ds(r0, T), :]
        b = _split_dot_rhs(tri_ref[z], jnp.log(fg) * math.log2(math.e))
        edge = b[0:1, :] if backward else b[T - 1:T, :]
        st = st_ref[z]
        o_inter = _dot_nt((qs * jnp.exp2(b)).astype(BF16), st.astype(BF16))
        off = 0
        offs = {}
        for s in range(T):
            gs = s // SUB
            groups = range(0, gs + 1) if backward else range(gs, ngrp)
            bs = b[s:s + 1, :]
            ks = kk[s:s + 1, :]
            for gidx in groups:
                rs = slice(gidx * SUB, (gidx + 1) * SUB)
                gval = qs[rs] * jnp.exp2(b[rs] - bs) * ks
                if gidx == gs:
                    keep = (row <= s - gs * SUB) if backward else (row >= s - gs * SUB)
                    gval = jnp.where(keep, gval, 0.0)
                g_ref[z, off:off + SUB, :] = gval
                offs[(s, gidx)] = off
                off += SUB
        red = _dot(g_ref[z].astype(BF16), sel)
        outs = []
        for gidx in range(ngrp):
            acc = o_inter[gidx * SUB:(gidx + 1) * SUB]
            for s in range(T):
                if (s, gidx) in offs:
                    o0 = offs[(s, gidx)]
                    acc = acc + red[o0:o0 + SUB] * vv[s:s + 1, :]
            outs.append(acc)
        o_ref[0, pl.ds(r0, T), :] = jnp.concatenate(outs, axis=0)
        kt = (kk * jnp.exp2(edge - b)).astype(BF16)
        upd = lax.dot_general(vv.astype(BF16), kt, (((0,), (0,)), ((), ())), preferred_element_type=F32)
        st_ref[z] = st * jnp.exp2(edge) + upd * bd_ref[...]

    def both(c, carry):
        step(c, False)
        step(nsub - 1 - c, True)
        return carry

    lax.fori_loop(0, nsub, both, 0, unroll=4)


def _split_dot_rhs(tri, x):
    tb = tri.astype(BF16)
    acc = None
    for _ in range(3):
        hi = x.astype(BF16)
        part = _dot(tb, hi)
        acc = part if acc is None else acc + part
        x = x - hi.astype(F32)
    return acc


def _hgrn2_tables():
    T = HG_T
    r = jnp.arange(T)
    tri = jnp.stack([(r[None, :] <= r[:, None]), (r[None, :] >= r[:, None])]).astype(F32)
    head = jnp.arange(HG_WIDTH) // (HG_WIDTH // HG_HEADS)
    same = (head[:, None] == head[None, :])
    return tri, same.astype(BF16), same.astype(F32)


def _hgrn2_groups():
    T, ngrp = HG_T, HG_T // SUB
    return sum(ngrp - s // SUB for s in range(T))


def _hgrn2_scan(pb, lb):
    B, L, _ = pb.shape
    tb = min(HG_TB, L)
    nblk = L // tb
    tri, sel, bd = _hgrn2_tables()
    W = HG_WIDTH
    fwd = lambda col: pl.BlockSpec((1, tb, W), lambda b, j: (b, j, col))
    bwd = lambda col: pl.BlockSpec((1, tb, W), lambda b, j: (b, nblk - 1 - j, col))
    full = lambda a: pl.BlockSpec(a.shape, lambda b, j: (0,) * a.ndim)
    lb3 = lb.reshape(2, 1, W)
    return pl.pallas_call(
        functools.partial(_hgrn2_kernel, nsub=tb // HG_T),
        grid=(B, nblk),
        in_specs=[fwd(0), bwd(0), fwd(1), bwd(2), fwd(3), bwd(3), full(lb3), full(tri), full(sel), full(bd)],
        out_specs=[fwd(0), bwd(0)],
        out_shape=[jax.ShapeDtypeStruct((B, L, W), F32)] * 2,
        scratch_shapes=[pltpu.VMEM((2, W, W), F32), pltpu.VMEM((2, _hgrn2_groups() * SUB, W), F32)],
        compiler_params=pltpu.CompilerParams(dimension_semantics=("parallel", "arbitrary")),
        name="hgrn2_scan",
    )(pb, pb, pb, pb, pb, pb, lb3, tri, sel, bd)


HY_N2 = 128
HY_K1_TILE = 8
HY_FILT_ROWS = 512
HY_KRON_MAX_N1 = 32


def _dot_hi(a, b):
    return jnp.dot(a, b, precision=lax.Precision.HIGHEST, preferred_element_type=F32)


def _dot_3x(a, b):
    a_hi = a.astype(BF16)
    b_hi = b.astype(BF16)
    a_lo = (a - a_hi.astype(F32)).astype(BF16)
    b_lo = (b - b_hi.astype(F32)).astype(BF16)
    return _dot(a_hi, b_hi) + _dot(a_hi, b_lo) + _dot(a_lo, b_hi)


HY_FILT_PACK = LANES // HY_BANDS


def _hy_filter_kernel(w1t_ref, w1c_ref, w1s_ref, b1_ref, fr_ref, w2_ref, b2_ref, w3_ref, bands_ref,
                      dl_ref, k_ref, sum_ref, *, L):
    tr = k_ref.shape[0]
    P, Hd = HY_FILT_PACK, HY_HIDDEN
    rq = tr // P
    r0 = pl.program_id(0) * tr

    def position(r):
        return jnp.where(r < L, r, 2 * L - r).astype(F32)

    def packed_rows(width, per):
        row = lax.broadcasted_iota(jnp.int32, (rq, width), 0)
        q = lax.broadcasted_iota(jnp.int32, (rq, width), 1) // per
        return r0 + q * rq + row

    bw = (position(packed_rows(LANES, HY_BANDS)) * (2.0 * math.pi / L)) * bands_ref[...]
    t_h = position(packed_rows(P * Hd, Hd)) * (1.0 / (L - 1))
    pre = t_h * w1t_ref[...] + _dot_hi(jnp.cos(bw), w1c_ref[...]) - _dot_hi(jnp.sin(bw), w1s_ref[...])
    fr = fr_ref[...]
    a = jnp.sin(fr * (pre + b1_ref[...]))
    a = jnp.sin(fr * (_dot_hi(a, w2_ref[...]) + b2_ref[...]))
    W2 = HY_ORDER * HY_WIDTH
    a_rows = jnp.concatenate([a[:, q * Hd:(q + 1) * Hd] for q in range(P)], axis=0)
    h = _dot_3x(a_rows, w3_ref[...])
    r = r0 + lax.broadcasted_iota(jnp.int32, (tr, 1), 0)
    decay = jnp.exp(-(position(r) * (1.0 / (L - 1))) * dl_ref[...])
    hf = h[:, :W2] * decay * (r < L).astype(F32)
    hb = h[:, W2:] * decay * ((r > L) | (r == 0)).astype(F32)
    k_ref[...] = hf + hb
    total = jnp.sum(jnp.abs(hf) + jnp.abs(hb), axis=0, keepdims=True)

    @pl.when(pl.program_id(0) == 0)
    def _():
        sum_ref[...] = jnp.zeros_like(sum_ref)

    sum_ref[...] += total


def _hy_filter(L, w1, b1, freq, w2, b2, w3):
    tr = min(HY_FILT_ROWS, L)
    W2 = HY_ORDER * HY_WIDTH
    P = HY_FILT_PACK
    eye = jnp.eye(P, dtype=F32)
    tile = lambda v: jnp.tile(v.reshape(1, -1), (1, P))
    bands = tile(jnp.linspace(1e-4, HY_BANDS - 1, HY_BANDS, dtype=F32))
    w1c = jnp.kron(eye, w1[1:1 + HY_BANDS])
    w1s = jnp.kron(eye, w1[1 + HY_BANDS:])
    w2b = jnp.kron(eye, w2)
    deltas = jnp.abs(jnp.linspace(math.log(HY_DECAY_TARGET) / HY_SLOW_PCT,
                                  math.log(HY_DECAY_TARGET) / HY_FAST_PCT, HY_WIDTH, dtype=F32))
    dl = jnp.tile(deltas, HY_ORDER).reshape(1, W2)
    args = (tile(w1[0]), w1c, w1s, tile(b1), tile(freq), w2b, tile(b2), w3, bands, dl)
    full = lambda a: pl.BlockSpec(a.shape, lambda i: (0, 0))
    return pl.pallas_call(
        functools.partial(_hy_filter_kernel, L=L),
        grid=(2 * L // tr,),
        in_specs=[full(a) for a in args],
        out_specs=[pl.BlockSpec((tr, W2), lambda i: (i, 0)), pl.BlockSpec((1, W2), lambda i: (0, 0))],
        out_shape=[jax.ShapeDtypeStruct((2 * L, W2), F32), jax.ShapeDtypeStruct((1, W2), F32)],
        compiler_params=pltpu.CompilerParams(dimension_semantics=("arbitrary",)),
        name="hyena_filter",
    )(*args)


def _hy_dft_tables(L):
    N = 2 * L
    N2 = HY_N2
    N1 = N // N2
    def cs(n, rows, cols):
        ang = (2.0 * math.pi / n) * ((jnp.arange(rows)[:, None] * jnp.arange(cols)[None, :]) % n).astype(F32)
        return jnp.cos(ang), jnp.sin(ang)
    ca, sa = cs(N1, N1, N1)
    fa_full = jnp.concatenate([ca, -sa], axis=0)
    fa_inv = jnp.concatenate([ca[:N1 // 2], -sa[:N1 // 2]], axis=1) * (1.0 / N)
    cb, sb = cs(N2, N2, N2)
    fb = jnp.concatenate([jnp.concatenate([cb, sb], axis=1), jnp.concatenate([-sb, cb], axis=1)], axis=0)
    tc, ts = cs(N, N1, N2)
    tw = jnp.stack([tc, ts])[..., None] * jnp.ones((1, 1, 1, 128), F32)
    tabs = dict(N1=N1, fa_half=fa_full[:, :N1 // 2].astype(BF16), fa_full=fa_full.astype(BF16),
                fa_inv=fa_inv.astype(BF16), fb=fb.astype(BF16), fb_t=fb.T.astype(BF16), tw=tw)
    if N1 <= HY_KRON_MAX_N1:
        eye = jnp.eye(SUB, dtype=F32)
        tabs['fa_half_kron'] = jnp.kron(fa_full[:, :N1 // 2], eye).astype(BF16)
        tabs['fa_inv_kron'] = jnp.kron(fa_inv, eye).astype(BF16)
    return tabs


def _hy_n2_rows(N1):
    return SUB * max(1, min(HY_N2 // SUB, 256 // N1))


def _hy_dft_a_kernel(m_ref, x_ref, o_ref):
    N1, C = o_ref.shape[2], o_ref.shape[4]
    for g0 in range(0, x_ref.shape[2], SUB):
        x = jnp.concatenate([x_ref[0, :, g0 + j, :] for j in range(SUB)], axis=1).astype(BF16)
        y = _dot(m_ref[...], x)
        for j in range(SUB):
            o_ref[0, 0, :, g0 + j, :] = y[:N1, j * C:(j + 1) * C]
            o_ref[0, 1, :, g0 + j, :] = y[N1:, j * C:(j + 1) * C]


def _hy_dft_a(m, x):
    B, R, N2, C = x.shape
    N1 = m.shape[0] // 2
    W = HY_WIDTH
    G = _hy_n2_rows(N1)
    return pl.pallas_call(
        _hy_dft_a_kernel,
        grid=(B, N2 // G, C // W),
        in_specs=[pl.BlockSpec(m.shape, lambda b, g, c: (0, 0)),
                  pl.BlockSpec((1, R, G, W), lambda b, g, c: (b, 0, g, c))],
        out_specs=pl.BlockSpec((1, 2, N1, G, W), lambda b, g, c: (b, 0, 0, g, c)),
        out_shape=jax.ShapeDtypeStruct((B, 2, N1, N2, C), F32),
        compiler_params=pltpu.CompilerParams(dimension_semantics=("parallel", "parallel", "parallel")),
        name="hyena_dft_a",
    )(m, x)


def _hy_gate_kernel(m_ref, y_ref, z_ref, g_ref, s_ref, o_ref):
    C = o_ref.shape[3]
    for g0 in range(0, o_ref.shape[2], SUB):
        y = jnp.concatenate(
            [jnp.concatenate([y_ref[0, 0, :, g0 + j, :], y_ref[0, 1, :, g0 + j, :]], axis=0) for j in range(SUB)],
            axis=1)
        zc = _dot(m_ref[...], y.astype(BF16))
        for j in range(SUB):
            o_ref[0, :, g0 + j, :] = g_ref[0, :, g0 + j, :] * (
                zc[:, j * C:(j + 1) * C] + s_ref[...] * z_ref[0, :, g0 + j, :])


def _hy_inverse_gate(m, y, z, gate, skip_row):
    B, _, N1, N2, C = y.shape
    G = _hy_n2_rows(N1)
    tok = pl.BlockSpec((1, N1 // 2, G, C), lambda b, g: (b, 0, g, 0))
    return pl.pallas_call(
        _hy_gate_kernel,
        grid=(B, N2 // G),
        in_specs=[pl.BlockSpec(m.shape, lambda b, g: (0, 0)),
                  pl.BlockSpec((1, 2, N1, G, C), lambda b, g: (b, 0, 0, g, 0)),
                  tok, tok,
                  pl.BlockSpec((1, C), lambda b, g: (0, 0))],
        out_specs=tok,
        out_shape=jax.ShapeDtypeStruct((B, N1 // 2, N2, C), F32),
        compiler_params=pltpu.CompilerParams(dimension_semantics=("parallel", "parallel")),
        name="hyena_idft_a_gate",
    )(m, y, z, gate, skip_row)


def _cmul(ar, ai, br, bi):
    return ar * br - ai * bi, ar * bi + ai * br


def _hy_stage_b_one(ar, ai, tcos, tsin, fb, fbt, kr, ki):
    N2 = HY_N2
    tc = jnp.concatenate([tcos, tcos], axis=1)
    ts = jnp.concatenate([tsin, tsin], axis=1)
    br, bi = _cmul(ar, ai, tc, -ts)
    x = _dot(fb, jnp.concatenate([br, bi], axis=0).astype(BF16))
    yr, yi = _cmul(x[:N2], x[N2:], kr, ki)
    y = _dot(fbt, jnp.concatenate([yr, yi], axis=0).astype(BF16))
    return _cmul(y[:N2], y[N2:], tc, ts)


def _hy_fused_kernel(ma_ref, mi_ref, tw_ref, fb_ref, fbt_ref, k_ref, z_ref, g_ref, s_ref, o_ref, w_sc):
    R, N2, C = z_ref.shape[1:]
    N1 = k_ref.shape[1]
    for g0 in range(0, N2, SUB):
        x = z_ref[0, :, g0:g0 + SUB, :].reshape(R * SUB, C).astype(BF16)
        w_sc[:, :, g0:g0 + SUB, :] = _dot(ma_ref[...], x).reshape(2, N1, SUB, C)

    def k1_step(j, carry):
        yr, yi = _hy_stage_b_one(w_sc[0, j], w_sc[1, j], tw_ref[0, j], tw_ref[1, j], fb_ref[...], fbt_ref[...],
                                 k_ref[0, j], k_ref[1, j])
        w_sc[0, j] = yr
        w_sc[1, j] = yi
        return carry

    lax.fori_loop(0, N1, k1_step, 0, unroll=2)
    for g0 in range(0, N2, SUB):
        y = w_sc[:, :, g0:g0 + SUB, :].reshape(2 * N1 * SUB, C).astype(BF16)
        zc = _dot(mi_ref[...], y).reshape(R, SUB, C)
        rows = slice(g0, g0 + SUB)
        o_ref[0, :, rows, :] = g_ref[0, :, rows, :] * (zc + s_ref[...] * z_ref[0, :, rows, :])


def _hy_fused_conv(tabs, kspec, order, z, gate, skip_row):
    B, R, N2, C = z.shape
    N1 = tabs['N1']
    once = lambda a: pl.BlockSpec(a.shape, lambda b: (0,) * a.ndim, pipeline_mode=pl.Buffered(1))
    tok = pl.BlockSpec((1, R, N2, C), lambda b: (b, 0, 0, 0))
    ma, mi = tabs['fa_half_kron'], tabs['fa_inv_kron']
    return pl.pallas_call(
        _hy_fused_kernel,
        grid=(B,),
        in_specs=[once(ma), once(mi), once(tabs['tw']), once(tabs['fb']), once(tabs['fb_t']),
                  pl.BlockSpec((2, N1, N2, C), lambda b: (0, 0, 0, order), pipeline_mode=pl.Buffered(1)),
                  tok, tok, pl.BlockSpec((1, C), lambda b: (0, 0))],
        out_specs=tok,
        out_shape=jax.ShapeDtypeStruct(z.shape, F32),
        scratch_shapes=[pltpu.VMEM((2, N1, N2, C), F32)],
        compiler_params=pltpu.CompilerParams(dimension_semantics=("parallel",)),
        name="hyena_conv_fused",
    )(ma, mi, tabs['tw'], tabs['fb'], tabs['fb_t'], kspec, z, gate, skip_row)


def _hy_stage_b_kernel(a_ref, tw_ref, fb_ref, fbt_ref, k_ref, o_ref, *, conv):
    N2 = HY_N2
    for j in range(a_ref.shape[2]):
        if conv:
            xr, xi = _hy_stage_b_one(a_ref[0, 0, j], a_ref[0, 1, j], tw_ref[0, j], tw_ref[1, j], fb_ref[...],
                                     fbt_ref[...], k_ref[0, j], k_ref[1, j])
        else:
            tc = jnp.concatenate([tw_ref[0, j], tw_ref[0, j]], axis=1)
            ts = jnp.concatenate([tw_ref[1, j], tw_ref[1, j]], axis=1)
            br, bi = _cmul(a_ref[0, 0, j], a_ref[0, 1, j], tc, -ts)
            x = _dot(fb_ref[...], jnp.concatenate([br, bi], axis=0).astype(BF16))
            xr, xi = x[:N2] * k_ref[...], x[N2:] * k_ref[...]
        o_ref[0, 0, j] = xr
        o_ref[0, 1, j] = xi


def _hy_stage_b(a, tabs, kspec=None, order=0, scale=None):
    B, _, N1, N2, C = a.shape
    t1 = HY_K1_TILE
    conv = kspec is not None
    blk = pl.BlockSpec((1, 2, t1, N2, HY_WIDTH), lambda i, b, c: (b, 0, i, 0, c))
    if conv:
        k_arr = kspec
        k_spec = pl.BlockSpec((2, t1, N2, HY_WIDTH), lambda i, b, c: (0, i, 0, order))
    else:
        k_arr = scale
        k_spec = pl.BlockSpec((1, HY_WIDTH), lambda i, b, c: (0, c))
    return pl.pallas_call(
        functools.partial(_hy_stage_b_kernel, conv=conv),
        grid=(N1 // t1, B, C // HY_WIDTH),
        in_specs=[blk,
                  pl.BlockSpec((2, t1, N2, 128), lambda i, b, c: (0, i, 0, 0)),
                  pl.BlockSpec(tabs['fb'].shape, lambda i, b, c: (0, 0)),
                  pl.BlockSpec(tabs['fb_t'].shape, lambda i, b, c: (0, 0)),
                  k_spec],
        out_specs=blk,
        out_shape=jax.ShapeDtypeStruct(a.shape, F32),
        compiler_params=pltpu.CompilerParams(dimension_semantics=("parallel", "parallel", "parallel")),
        name="hyena_stage_b_conv" if conv else "hyena_stage_b_spectrum",
    )(a, tabs['tw'], tabs['fb'], tabs['fb_t'], k_arr)


def _hy_filter_spectrum(L, tabs, w1, b1, freq, w2, b2, w3):
    N1, N2 = tabs['N1'], HY_N2
    W2 = HY_ORDER * HY_WIDTH
    k, total = _hy_filter(L, w1, b1, freq, w2, b2, w3)
    a = _hy_dft_a(tabs['fa_full'], k.reshape(1, N1, N2, W2))
    return _hy_stage_b(a, tabs, scale=1.0 / total)[0]


def _hyena(v, x1, x2, skip, tabs, kspec):
    B, L, _ = v.shape
    N1, N2, C = tabs['N1'], HY_N2, HY_WIDTH
    grid = lambda t: t.reshape(B, N1 // 2, N2, C)
    z = grid(v)
    for n, gate in enumerate((x1, x2)):
        if 'fa_half_kron' in tabs:
            z = _hy_fused_conv(tabs, kspec, n, z, grid(gate), skip[n].reshape(1, C))
        else:
            a = _hy_dft_a(tabs['fa_half'], z)
            y = _hy_stage_b(a, tabs, kspec, order=n)
            z = _hy_inverse_gate(tabs['fa_inv'], y, z, grid(gate), skip[n].reshape(1, C))
    return z.reshape(B, L, C)


def _prep_layer_weights(l, w_in, w_out, na_rpb, mla_w_uq, mla_w_ukv, ffn_w_up, ffn_conv, ffn_w_down):
    wi = w_in[l]
    s0, s1, s2 = NA_COLS, NA_COLS + HG_COLS, NA_COLS + HG_COLS + HY_COLS
    w_ml = jnp.pad(wi[:, s2:], ((0, 0), (0, MLA_COLS_PAD - MLA_COLS)))
    return dict(
        w_na=wi[:, :s0].astype(BF16), w_hg=wi[:, s0:s1].astype(BF16), w_hy=wi[:, s1:s2].astype(BF16),
        w_ml=w_ml.astype(BF16), w_out=w_out[l].astype(BF16), na_bias=_na_bias_table(na_rpb[l]),
        mla=_mla_weights(mla_w_uq[l], mla_w_ukv[l]),
        ffn=_ffn_weight_tiles(ffn_w_up[l], ffn_conv[l], ffn_w_down[l]),
    )


def _trunk_layer(x, mod, lb, g, lw, p, cos, sin, hy_tabs, hy_spec):
    L = x.shape[1]
    pa, pb, hv, hx1, hx2, q, k, v = _in_proj(x, mod, g, lw['w_na'], lw['w_hg'], lw['w_hy'], lw['w_ml'], p['hy_short'],
                                             cos[:L], sin[:L], p['mla_q_norm'], p['mla_kv_norm'], lw['mla'])
    o_na = _neighbourhood_attention(pa, lw['na_bias'])
    hg_dirs = _hgrn2_scan(pb, lb)
    o_hy = _hyena(hv, hx1, hx2, p['hy_skip'], hy_tabs, hy_spec)
    o_ml = _mla_flash(q, k, v)
    x1, h2 = _out_proj(x, mod, g, o_na, hg_dirs, pb, p['hg_norm'], o_hy, o_ml, lw['w_out'])
    return _conv_ffn(h2, x1, mod, g, *lw['ffn'])


def kernel(x_prompt, x_sample, c_prompt, c_sample, ada_w, ada_b, norm_g, w_in, w_out, na_rpb, hg_lb,
           hg_norm, hy_short, hy_w1, hy_b1, hy_freq, hy_w2, hy_b2, hy_w3, hy_skip, mla_q_norm,
           mla_kv_norm, mla_w_uq, mla_w_ukv, ffn_w_up, ffn_conv, ffn_w_down):
    Bp, Bs = x_prompt.shape[0], x_sample.shape[0]
    Lp, Ls = x_prompt.shape[1], x_sample.shape[1]
    lb_soft = jax.nn.softmax(hg_lb.astype(F32), axis=0)
    lower_bounds = jnp.cumsum(lb_soft, axis=0) - lb_soft[:1]

    R = -(-(Bp + Bs) // 8) * 8
    cond = jnp.zeros((R, D_MODEL), F32).at[:Bp].set(c_prompt).at[Bp:Bp + Bs].set(c_sample)
    mod = _ada_modulation(cond, ada_w, ada_b).reshape(DEPTH, R, 6, D_MODEL)
    mod = jnp.pad(mod, ((0, 0), (0, 0), (0, 2), (0, 0)))

    cos, sin = _rope_tables(max(Lp, Ls))
    tabs_p, tabs_s = _hy_dft_tables(Lp), _hy_dft_tables(Ls)
    y_prompt, y_sample = x_prompt, x_sample
    for l in range(DEPTH):
        lw = _prep_layer_weights(l, w_in, w_out, na_rpb, mla_w_uq, mla_w_ukv, ffn_w_up, ffn_conv, ffn_w_down)
        p = dict(hg_norm=hg_norm[l], hy_short=hy_short[l], hy_skip=hy_skip[l],
                 mla_q_norm=mla_q_norm[l].reshape(1, -1), mla_kv_norm=mla_kv_norm[l].reshape(1, -1))
        filt = (hy_w1[l], hy_b1[l], hy_freq[l], hy_w2[l], hy_b2[l], hy_w3[l])
        spec_p = _hy_filter_spectrum(Lp, tabs_p, *filt)
        spec_s = _hy_filter_spectrum(Ls, tabs_s, *filt)
        y_prompt = _trunk_layer(y_prompt, mod[l, :Bp], lower_bounds[l], norm_g[l], lw, p, cos, sin,
                                tabs_p, spec_p)
        y_sample = _trunk_layer(y_sample, mod[l, Bp:Bp + Bs], lower_bounds[l], norm_g[l], lw, p, cos, sin,
                                tabs_s, spec_s)
    return (y_prompt, y_sample)
```

```python
import functools
import math

import jax
import jax.numpy as jnp
from jax import lax
from jax.experimental import pallas as pl
from jax.experimental.pallas import tpu as pltpu

F32 = jnp.float32
BF16 = jnp.bfloat16

D_MODEL = 1024
DEPTH = 4
GRID_W = 64
HEAD_DIM = 64
NA_WIDTH = 256
NA_HEADS = 4
NA_KH = 8
NA_KW = 16
HG_WIDTH = 256
HG_HEADS = 4
HY_WIDTH = 256
HY_ORDER = 2
HY_BANDS = 16
HY_HIDDEN = 64
HY_DECAY_TARGET = 1e-2
HY_FAST_PCT = 0.3
HY_SLOW_PCT = 1.5
MLA_WIDTH = 256
MLA_HEADS = 4
MLA_NOPE = 64
MLA_ROPE = 32
MLA_V = 64
MLA_Q_LORA = 256
MLA_KV_LORA = 128
MLA_HEAD_PAD = 128
ROPE_THETA = 10000.0
D_FF = 2816
EPS = 1e-6
NA_COLS = 3 * NA_WIDTH
HG_COLS = 5 * HG_WIDTH
HY_COLS = 3 * HY_WIDTH
MLA_COLS = MLA_Q_LORA + MLA_KV_LORA + MLA_ROPE
MLA_COLS_PAD = 512
NEG_BIG = -1e30
SUB = 8
LANES = 128


def _dot(a, b):
    return jnp.dot(a, b, preferred_element_type=F32)


def _dot_nt(a, b):
    return lax.dot_general(a, b, (((1,), (1,)), ((), ())), preferred_element_type=F32)


def _rms(x, g):
    return x * lax.rsqrt(jnp.mean(x * x, axis=-1, keepdims=True) + EPS) * g


def _ada_kernel(c_ref, w_ref, b_ref, o_ref):
    c = c_ref[...]
    s = c * jax.nn.sigmoid(c)
    o_ref[0] = _dot(s.astype(BF16), w_ref[0].astype(BF16)) + b_ref[0]


def _ada_modulation(cond, ada_w, ada_b):
    R = cond.shape[0]
    tn = 1536
    return pl.pallas_call(
        _ada_kernel,
        grid=(DEPTH, 6 * D_MODEL // tn),
        in_specs=[
            pl.BlockSpec((R, D_MODEL), lambda l, n: (0, 0)),
            pl.BlockSpec((1, D_MODEL, tn), lambda l, n: (l, 0, n)),
            pl.BlockSpec((1, 1, tn), lambda l, n: (l, 0, n)),
        ],
        out_specs=pl.BlockSpec((1, R, tn), lambda l, n: (l, 0, n)),
        out_shape=jax.ShapeDtypeStruct((DEPTH, R, 6 * D_MODEL), F32),
        name="ada_modulation",
    )(cond, ada_w, ada_b.reshape(DEPTH, 1, 6 * D_MODEL))


IN_HALO = 16


def _in_proj_kernel(x_ref, xp_ref, xn_ref, mod_ref, g_ref, wna, whg, why, wml, sw_ref,
                    cos_ref, sin_ref, qn_ref, kvn_ref, wq, wqr, wk, wv, we, wer,
                    ona, ohg, ov, ox1, ox2, oq, ok, ovv, hext_ref):
    i = pl.program_id(1)
    tm = x_ref.shape[1]
    H = IN_HALO
    g = g_ref[0:1, :]
    scale = 1.0 + mod_ref[0, 1:2, :]
    shift = mod_ref[0, 0:1, :]
    prep = lambda x: (_rms(x, g) * scale + shift).astype(BF16)
    hext_ref[0:H, :] = prep(xp_ref[0])
    hext_ref[H:H + tm, :] = prep(x_ref[0])
    hext_ref[H + tm:2 * H + tm, :] = prep(xn_ref[0])
    hb = hext_ref[H:H + tm, :]
    ona[0] = _dot(hb, wna[...])
    ohg[0] = _dot(hb, whg[...])
    _mla_qkv_tile(_dot(hb, wml[...]), cos_ref[...], sin_ref[...], qn_ref[...], kvn_ref[...], wq[...], wqr[...],
                  wk[...], wv[...], we[...], wer[...], oq, ok, ovv)
    ue = _dot(hext_ref[...], why[...])
    u = ue[H:H + tm]
    u_prev = ue[H - 1:H] * (i > 0).astype(F32)
    u_next = ue[H + tm:H + tm + 1] * (i < pl.num_programs(1) - 1).astype(F32)
    sub = lax.broadcasted_iota(jnp.int32, (SUB, 1), 0)
    down = pltpu.roll(u, 1, 0)
    up = pltpu.roll(u, tm - 1, 0)
    down = jnp.concatenate([jnp.where(sub == 0, u_prev, down[:SUB]), down[SUB:]], axis=0)
    up = jnp.concatenate([up[:tm - SUB], jnp.where(sub == SUB - 1, u_next, up[tm - SUB:])], axis=0)
    y = sw_ref[0:1, :] * down + sw_ref[1:2, :] * u + sw_ref[2:3, :] * up
    W = HY_WIDTH
    ov[0] = y[:, 0 * W:1 * W]
    ox1[0] = y[:, 1 * W:2 * W]
    ox2[0] = y[:, 2 * W:3 * W]


def _in_proj(x, mod, g, w_na, w_hg, w_hy, w_ml, short_w, cos, sin, q_norm, kv_norm, mla_w, tm=512):
    B, L, D = x.shape
    tm = min(tm, L)
    nh = tm // IN_HALO
    last = L // IN_HALO - 1
    full = lambda a: pl.BlockSpec(a.shape, lambda b, i: (0, 0))
    rope = pl.BlockSpec((tm, MLA_HEAD_PAD), lambda b, i: (i, 0))
    HP = MLA_HEADS * MLA_HEAD_PAD
    outs = [(NA_COLS, F32), (HG_COLS, F32), (HY_WIDTH, F32), (HY_WIDTH, F32), (HY_WIDTH, F32),
            (HP, BF16), (HP, BF16), (MLA_WIDTH, BF16)]
    return pl.pallas_call(
        _in_proj_kernel,
        grid=(B, L // tm),
        in_specs=[
            pl.BlockSpec((1, tm, D), lambda b, i: (b, i, 0)),
            pl.BlockSpec((1, IN_HALO, D), lambda b, i: (b, jnp.maximum(i * nh - 1, 0), 0)),
            pl.BlockSpec((1, IN_HALO, D), lambda b, i: (b, jnp.minimum((i + 1) * nh, last), 0)),
            pl.BlockSpec((1, 8, D), lambda b, i: (b, 0, 0)),
            full(g), full(w_na), full(w_hg), full(w_hy), full(w_ml), full(short_w),
            rope, rope, full(q_norm), full(kv_norm),
        ] + [full(w) for w in mla_w],
        out_specs=[pl.BlockSpec((1, tm, n), lambda b, i: (b, i, 0)) for n, _ in outs],
        out_shape=[jax.ShapeDtypeStruct((B, L, n), dt) for n, dt in outs],
        scratch_shapes=[pltpu.VMEM((tm + 2 * IN_HALO, D), BF16)],
        compiler_params=pltpu.CompilerParams(dimension_semantics=("parallel", "parallel")),
        name="in_proj",
    )(x, x, x, mod, g, w_na, w_hg, w_hy, w_ml, short_w, cos, sin, q_norm, kv_norm, *mla_w)


def _head_mean_sq(o, sel):
    sq = o * o
    hi = sq.astype(BF16)
    lo = (sq - hi.astype(F32)).astype(BF16)
    return (_dot(hi, sel) + _dot(lo, sel)) * (1.0 / (HG_WIDTH // HG_HEADS))


def _out_proj_kernel(x_ref, mod_ref, g_ref, na, hgf, hgb, hgg, hgn, sel, hy, ml, w_ref, x1_ref, h2_ref):
    W = NA_WIDTH
    o = hgf[0] + hgb[0]
    gate = hgg[0]
    hg = o * lax.rsqrt(_head_mean_sq(o, sel[...]) + EPS) * hgn[...] * (gate * jax.nn.sigmoid(gate))
    mix = _dot(na[0], w_ref[0 * W:1 * W, :])
    mix += _dot(hg.astype(BF16), w_ref[1 * W:2 * W, :])
    mix += _dot(hy[0].astype(BF16), w_ref[2 * W:3 * W, :])
    mix += _dot(ml[0], w_ref[3 * W:4 * W, :])
    x1 = x_ref[0] + mod_ref[0, 2:3, :] * _rms(mix, g_ref[1:2, :])
    x1_ref[0] = x1
    h2 = _rms(x1, g_ref[2:3, :]) * (1.0 + mod_ref[0, 4:5, :]) + mod_ref[0, 3:4, :]
    h2_ref[0] = h2.astype(BF16)


def _out_proj(x, mod, g, o_na, hg_dirs, pb, hg_norm, o_hy, o_ml, w_out, tm=512):
    B, L, D = x.shape
    tm = min(tm, L)
    tok = lambda n: pl.BlockSpec((1, tm, n), lambda b, i: (b, i, 0))
    sel = _hgrn2_tables()[1]
    return pl.pallas_call(
        _out_proj_kernel,
        grid=(B, L // tm),
        in_specs=[
            tok(D),
            pl.BlockSpec((1, 8, D), lambda b, i: (b, 0, 0)),
            pl.BlockSpec(g.shape, lambda b, i: (0, 0)),
            tok(NA_WIDTH),
            tok(HG_WIDTH), tok(HG_WIDTH),
            pl.BlockSpec((1, tm, HG_WIDTH), lambda b, i: (b, i, 4)),
            pl.BlockSpec((1, HG_WIDTH), lambda b, i: (0, 0)),
            pl.BlockSpec(sel.shape, lambda b, i: (0, 0)),
            tok(HY_WIDTH), tok(MLA_WIDTH),
            pl.BlockSpec(w_out.shape, lambda b, i: (0, 0)),
        ],
        out_specs=[tok(D), tok(D)],
        out_shape=[jax.ShapeDtypeStruct((B, L, D), F32), jax.ShapeDtypeStruct((B, L, D), BF16)],
        compiler_params=pltpu.CompilerParams(dimension_semantics=("parallel", "parallel")),
        name="out_proj",
    )(x, mod, g, o_na, hg_dirs[0], hg_dirs[1], pb, hg_norm.reshape(1, HG_WIDTH), sel, o_hy, o_ml, w_out)


FFN_HALO = 16


def _ffn_kernel(hm_ref, hp_ref, hn_ref, x_ref, mod_ref, g_ref, wu_ref, cw_ref, wd_ref, o_ref, acc_ref,
                hext_ref):
    i = pl.program_id(1)
    tm = hm_ref.shape[1]
    nf = wd_ref.shape[0]
    not_first = (i > 0).astype(F32)
    not_last = (i < pl.num_programs(1) - 1).astype(F32)
    sub = lax.broadcasted_iota(jnp.int32, (SUB, 1), 0)

    H = FFN_HALO
    hext_ref[0:H, :] = hp_ref[0]
    hext_ref[H:H + tm, :] = hm_ref[0]
    hext_ref[H + tm:2 * H + tm, :] = hn_ref[0]

    def conv_branch(t):
        c = cw_ref[t]
        ue = _dot(hext_ref[...], wu_ref[t])
        u = ue[H:H + tm]
        u_prev = ue[H - 1:H] * not_first
        u_next = ue[H + tm:H + tm + 1] * not_last
        down = pltpu.roll(u, 1, 0)
        up = pltpu.roll(u, tm - 1, 0)
        down = jnp.concatenate([jnp.where(sub == 0, u_prev, down[:SUB]), down[SUB:]], axis=0)
        up = jnp.concatenate([up[:tm - SUB], jnp.where(sub == SUB - 1, u_next, up[tm - SUB:])], axis=0)
        return c[0:1, :] * down + c[1:2, :] * u + c[2:3, :] * up

    def tile(f, carry):
        a = conv_branch(f)
        b = conv_branch(nf + f)
        act = (jax.nn.gelu(a, approximate=True) * b).astype(BF16)
        acc_ref[...] += _dot(act, wd_ref[f])
        return carry

    acc_ref[...] = jnp.zeros_like(acc_ref)
    lax.fori_loop(0, nf, tile, 0, unroll=FFN_UNROLL)
    o_ref[0] = x_ref[0] + mod_ref[0, 5:6, :] * _rms(acc_ref[...], g_ref[3:4, :])


FFN_TF = 256
FFN_UNROLL = 2


def _conv_ffn(h2, x1, mod, g, w_up, conv_w, w_down, tm=1024):
    B, L, D = x1.shape
    tm = min(tm, L)
    nh = tm // FFN_HALO
    last_halo = L // FFN_HALO - 1
    once = lambda a: pl.BlockSpec(a.shape, lambda b, i: (0,) * a.ndim, pipeline_mode=pl.Buffered(1))
    return pl.pallas_call(
        _ffn_kernel,
        grid=(B, L // tm),
        in_specs=[
            pl.BlockSpec((1, tm, D), lambda b, i: (b, i, 0)),
            pl.BlockSpec((1, FFN_HALO, D), lambda b, i: (b, jnp.maximum(i * nh - 1, 0), 0)),
            pl.BlockSpec((1, FFN_HALO, D), lambda b, i: (b, jnp.minimum((i + 1) * nh, last_halo), 0)),
            pl.BlockSpec((1, tm, D), lambda b, i: (b, i, 0)),
            pl.BlockSpec((1, 8, D), lambda b, i: (b, 0, 0)),
            pl.BlockSpec(g.shape, lambda b, i: (0, 0)),
            once(w_up), once(conv_w), once(w_down),
        ],
        out_specs=pl.BlockSpec((1, tm, D), lambda b, i: (b, i, 0)),
        out_shape=jax.ShapeDtypeStruct((B, L, D), F32),
        scratch_shapes=[pltpu.VMEM((tm, D), F32), pltpu.VMEM((tm + 2 * FFN_HALO, D), BF16)],
        compiler_params=pltpu.CompilerParams(dimension_semantics=("parallel", "parallel")),
        name="conv_ffn",
    )(h2, h2, h2, x1, mod, g, w_up, conv_w, w_down)


def _ffn_weight_tiles(w_up, conv_w, w_down):
    D = w_up.shape[0]
    n2 = 2 * D_FF // FFN_TF
    wu = w_up.reshape(D, n2, FFN_TF).transpose(1, 0, 2).astype(BF16)
    cw = conv_w.reshape(3, n2, FFN_TF).transpose(1, 0, 2)
    wd = w_down.reshape(D_FF // FFN_TF, FFN_TF, D).astype(BF16)
    return wu, cw, wd


NA_RB = 8


def _na_kernel(q_ref, kp_ref, kc_ref, kn_ref, vp_ref, vc_ref, vn_ref, bias_ref, o_ref,
               kbuf, vbuf, *, rows):
    i = pl.program_id(1)
    blk = NA_RB * GRID_W
    win = NA_KH * GRID_W
    kbuf[0 * blk:1 * blk, :] = kp_ref[0].astype(BF16)
    kbuf[1 * blk:2 * blk, :] = kc_ref[0].astype(BF16)
    kbuf[2 * blk:3 * blk, :] = kn_ref[0].astype(BF16)
    vbuf[0 * blk:1 * blk, :] = vp_ref[0].astype(BF16)
    vbuf[1 * blk:2 * blk, :] = vc_ref[0].astype(BF16)
    vbuf[2 * blk:3 * blk, :] = vn_ref[0].astype(BF16)
    scale = HEAD_DIM ** -0.5 * math.log2(math.e)
    lane_head = lax.broadcasted_iota(jnp.int32, (1, NA_WIDTH), 1) // HEAD_DIM
    for j in range(NA_RB):
        r = i * NA_RB + j
        start = jnp.clip(r - NA_KH // 2, 0, rows - NA_KH)
        loc = start - (i - 1) * NA_RB
        dr0 = start - r + (NA_KH - 1)
        off = pl.multiple_of(loc * GRID_W, GRID_W)
        kw = kbuf[pl.ds(off, win), :]
        vw = vbuf[pl.ds(off, win), :]
        qj = (q_ref[0, j * GRID_W:(j + 1) * GRID_W, :] * scale).astype(BF16)
        qbd = jnp.concatenate([jnp.where(lane_head == h, qj, jnp.zeros_like(qj)) for h in range(NA_HEADS)],
                              axis=0)
        s = _dot_nt(qbd, kw) + bias_ref[dr0]
        m = jnp.max(s, axis=-1, keepdims=True)
        p = jnp.exp2(s - m)
        l = jnp.sum(p, axis=-1, keepdims=True)
        pv = _dot(p.astype(BF16), vw) / l
        o = None
        for h in range(NA_HEADS):
            part = jnp.where(lane_head == h, pv[h * GRID_W:(h + 1) * GRID_W], 0.0)
            o = part if o is None else o + part
        o_ref[0, j * GRID_W:(j + 1) * GRID_W, :] = o.astype(o_ref.dtype)


def _na_bias_table(rpb):
    c = jnp.arange(GRID_W)
    dc = jnp.clip(c[None, :] - c[:, None] + (NA_KW - 1), 0, 2 * NA_KW - 2)
    col_start = jnp.clip(c - NA_KW // 2, 0, GRID_W - NA_KW)
    ok = (c[None, :] >= col_start[:, None]) & (c[None, :] < col_start[:, None] + NA_KW)
    e = jnp.where(ok[None, None], rpb.astype(F32)[:, :, dc], NEG_BIG)
    t = jnp.stack([e[:, d0:d0 + NA_KH] for d0 in range(NA_KH)], axis=1)
    t = t.transpose(1, 0, 3, 2, 4).reshape(NA_KH, NA_HEADS * GRID_W, NA_KH * GRID_W)
    return jnp.where(t > 0.5 * NEG_BIG, t * math.log2(math.e), NEG_BIG)


def _neighbourhood_attention(pa, bias):
    B, L, _ = pa.shape
    rows = L // GRID_W
    blk = NA_RB * GRID_W
    nblk = rows // NA_RB
    spec = lambda col, shift: pl.BlockSpec(
        (1, blk, NA_WIDTH), lambda b, i: (b, jnp.clip(i + shift, 0, nblk - 1), col))
    return pl.pallas_call(
        functools.partial(_na_kernel, rows=rows),
        grid=(B, nblk),
        in_specs=[spec(0, 0), spec(1, -1), spec(1, 0), spec(1, 1), spec(2, -1), spec(2, 0), spec(2, 1),
                  pl.BlockSpec(bias.shape, lambda b, i: (0, 0, 0))],
        out_specs=pl.BlockSpec((1, blk, NA_WIDTH), lambda b, i: (b, i, 0)),
        out_shape=jax.ShapeDtypeStruct((B, L, NA_WIDTH), BF16),
        scratch_shapes=[pltpu.VMEM((3 * blk, NA_WIDTH), BF16), pltpu.VMEM((3 * blk, NA_WIDTH), BF16)],
        compiler_params=pltpu.CompilerParams(dimension_semantics=("parallel", "parallel")),
        name="neighbourhood_attention",
    )(pa, pa, pa, pa, pa, pa, pa, bias)


def _rope_table_kernel(inv_ref, cos_ref, sin_ref):
    t = cos_ref.shape[0]
    pos = (pl.program_id(0) * t + lax.broadcasted_iota(jnp.int32, (t, MLA_HEAD_PAD), 0)).astype(F32)
    ang = pos * inv_ref[...]
    cos_ref[...] = jnp.cos(ang)
    sin_ref[...] = jnp.sin(ang)


def _rope_tables(L):
    half = MLA_ROPE // 2
    inv = ROPE_THETA ** (-jnp.arange(half, dtype=F32) / half)
    inv_row = jnp.zeros((1, MLA_HEAD_PAD), F32).at[0, MLA_NOPE:MLA_NOPE + MLA_ROPE].set(jnp.tile(inv, 2))
    t = min(L, 1024)
    return pl.pallas_call(
        _rope_table_kernel,
        grid=(L // t,),
        in_specs=[pl.BlockSpec((1, MLA_HEAD_PAD), lambda i: (0, 0))],
        out_specs=[pl.BlockSpec((t, MLA_HEAD_PAD), lambda i: (i, 0))] * 2,
        out_shape=[jax.ShapeDtypeStruct((L, MLA_HEAD_PAD), F32)] * 2,
        name="rope_tables",
    )(inv_row)


def _mla_qkv_tile(pd, cos, sin, q_norm, kv_norm, wq, wqr, wk, wv, we, wer, q_out, k_out, v_out):
    nq = _rms(pd[:, :MLA_Q_LORA], q_norm).astype(BF16)
    nkv = _rms(pd[:, MLA_Q_LORA:MLA_Q_LORA + MLA_KV_LORA], kv_norm).astype(BF16)
    kr = pd[:, MLA_Q_LORA + MLA_KV_LORA:].astype(BF16)
    q = _dot(nq, wq)
    q_rot = _dot(nq, wqr)
    k = _dot(nkv, wk) + _dot(kr, we)
    k_rot = _dot(kr, wer)
    scale = (MLA_NOPE + MLA_ROPE) ** -0.5 * math.log2(math.e)
    for h in range(MLA_HEADS):
        hs = slice(h * MLA_HEAD_PAD, (h + 1) * MLA_HEAD_PAD)
        q_out[0, :, hs] = ((q[:, hs] * cos + q_rot[:, hs] * sin) * scale).astype(BF16)
        k_out[0, :, hs] = (k[:, hs] * cos + k_rot[:, hs] * sin).astype(BF16)
    v_out[0] = _dot(nkv, wv).astype(BF16)


def _mla_weights(w_uq, w_ukv):
    half = MLA_ROPE // 2
    P = jnp.zeros((MLA_ROPE, MLA_ROPE), F32)
    P = P.at[jnp.arange(half) + half, jnp.arange(half)].set(-1.0)
    P = P.at[jnp.arange(half), jnp.arange(half) + half].set(1.0)
    HP = MLA_HEAD_PAD
    wq = jnp.zeros((MLA_Q_LORA, MLA_HEADS * HP), F32)
    wqr = jnp.zeros_like(wq)
    wk = jnp.zeros((MLA_KV_LORA, MLA_HEADS * HP), F32)
    wv = jnp.zeros((MLA_KV_LORA, MLA_HEADS * MLA_V), F32)
    we = jnp.zeros((MLA_COLS_PAD - MLA_Q_LORA - MLA_KV_LORA, MLA_HEADS * HP), F32)
    wer = jnp.zeros_like(we)
    eye = jnp.eye(MLA_ROPE, dtype=F32)
    for h in range(MLA_HEADS):
        qh = w_uq[:, h * (MLA_NOPE + MLA_ROPE):(h + 1) * (MLA_NOPE + MLA_ROPE)]
        wq = wq.at[:, h * HP:h * HP + MLA_NOPE + MLA_ROPE].set(qh)
        wqr = wqr.at[:, h * HP + MLA_NOPE:h * HP + MLA_NOPE + MLA_ROPE].set(qh[:, MLA_NOPE:] @ P)
        kvh = w_ukv[:, h * (MLA_NOPE + MLA_V):(h + 1) * (MLA_NOPE + MLA_V)]
        wk = wk.at[:, h * HP:h * HP + MLA_NOPE].set(kvh[:, :MLA_NOPE])
        wv = wv.at[:, h * MLA_V:(h + 1) * MLA_V].set(kvh[:, MLA_NOPE:])
        we = we.at[:MLA_ROPE, h * HP + MLA_NOPE:h * HP + MLA_NOPE + MLA_ROPE].set(eye)
        wer = wer.at[:MLA_ROPE, h * HP + MLA_NOPE:h * HP + MLA_NOPE + MLA_ROPE].set(P)
    return tuple(w.astype(BF16) for w in (wq, wqr, wk, wv, we, wer))


FLASH_ROWS = 16
FLASH_HEADS = 4


def _flash_kernel(q_ref, k_ref, v_ref, o_ref, m_sc, acc_sc, p_sc):
    kv = pl.program_id(3)
    _, tq, tk = p_sc.shape
    RB = FLASH_ROWS

    @pl.when(kv == 0)
    def _():
        m_sc[...] = jnp.full_like(m_sc, -jnp.inf)
        acc_sc[...] = jnp.zeros_like(acc_sc)

    first_head = lax.broadcasted_iota(jnp.int32, (1, LANES), 1) < MLA_V
    for h in range(FLASH_HEADS):
        q = q_ref[0, :, h * MLA_HEAD_PAD:(h + 1) * MLA_HEAD_PAD]
        k = k_ref[0, :, h * MLA_HEAD_PAD:(h + 1) * MLA_HEAD_PAD]
        s = _dot_nt(q, k)
        alphas = []
        for r in range(tq // RB):
            rows = slice(r * RB, (r + 1) * RB)
            x = s[rows, 0:LANES]
            for c in range(1, tk // LANES):
                x = jnp.maximum(x, s[rows, c * LANES:(c + 1) * LANES])
            m_prev = m_sc[h, rows, :]
            m_new = jnp.maximum(m_prev, jnp.max(x, axis=-1, keepdims=True))
            alphas.append(jnp.exp2(m_prev - m_new))
            m_sc[h, rows, :] = m_new
            for c in range(tk // LANES):
                cols = slice(c * LANES, (c + 1) * LANES)
                p_sc[h, rows, cols] = jnp.exp2((s[rows, cols] - m_new).astype(BF16))
        v = v_ref[0, :, (h // 2) * LANES:(h // 2 + 1) * LANES]
        ones = jnp.ones_like(v)
        v_aug = jnp.where(first_head, v, ones) if h % 2 == 0 else jnp.where(first_head, ones, v)
        acc_sc[h] = jnp.concatenate(alphas, axis=0) * acc_sc[h] + _dot(p_sc[h], v_aug)

    @pl.when(kv == pl.num_programs(3) - 1)
    def _():
        for g in range(FLASH_HEADS // 2):
            a0 = acc_sc[2 * g]
            a1 = acc_sc[2 * g + 1]
            num = jnp.where(first_head, a0, a1)
            den = jnp.where(first_head, pltpu.roll(a0, MLA_V, 1), pltpu.roll(a1, MLA_V, 1))
            o_ref[0, :, g * LANES:(g + 1) * LANES] = (num / den).astype(o_ref.dtype)


def _mla_flash(q, k, v, tq=512, tk=2048):
    B, L, _ = q.shape
    tq = min(tq, L)
    tk = min(tk, L)
    nh = FLASH_HEADS
    stat = pltpu.VMEM((nh, tq, LANES), F32)
    return pl.pallas_call(
        _flash_kernel,
        grid=(B, MLA_HEADS // nh, L // tq, L // tk),
        in_specs=[
            pl.BlockSpec((1, tq, nh * MLA_HEAD_PAD), lambda b, h, i, j: (b, i, h)),
            pl.BlockSpec((1, tk, nh * MLA_HEAD_PAD), lambda b, h, i, j: (b, j, h)),
            pl.BlockSpec((1, tk, nh * MLA_V), lambda b, h, i, j: (b, j, h)),
        ],
        out_specs=pl.BlockSpec((1, tq, nh * MLA_V), lambda b, h, i, j: (b, i, h)),
        out_shape=jax.ShapeDtypeStruct((B, L, MLA_WIDTH), BF16),
        scratch_shapes=[stat, stat, pltpu.VMEM((nh, tq, tk), BF16)],
        compiler_params=pltpu.CompilerParams(
            dimension_semantics=("parallel", "parallel", "parallel", "arbitrary")),
        name="mla_flash",
    )(q, k, v)


HG_T = 32
HG_TB = 2048


def _hgrn2_kernel(qf_ref, qb_ref, ff_ref, fb_ref, if_ref, ib_ref, lb_ref, tri_ref, sel_ref, bd_ref,
                  of_ref, ob_ref, st_ref, g_ref, *, nsub):
    T = HG_T
    ngrp = T // SUB

    @pl.when(pl.program_id(1) == 0)
    def _():
        st_ref[...] = jnp.zeros_like(st_ref)

    sel = sel_ref[...]
    row = lax.broadcasted_iota(jnp.int32, (SUB, HG_WIDTH), 0)

    def step(c, backward):
        z = 1 if backward else 0
        q_ref, f_ref, i_ref, o_ref = (qb_ref, fb_ref, ib_ref, ob_ref) if backward else (qf_ref, ff_ref, if_ref, of_ref)
        lb = lb_ref[z]
        r0 = pl.multiple_of(c * T, T)
        qs = q_ref[0, pl.ds(r0, T), :]
        qs = qs * jax.nn.sigmoid(qs)
        fg = lb + (1.0 - lb) * jax.nn.sigmoid(f_ref[0, pl.ds(r0, T), :])
        kk = 1.0 - fg
        vv = i_ref[0, pl.ds(r0, T), :]
        b = _split_dot_rhs(tri_ref[z], jnp.log(fg) * math.log2(math.e))
        edge = b[0:1, :] if backward else b[T - 1:T, :]
        st = st_ref[z]
        o_inter = _dot_nt((qs * jnp.exp2(b)).astype(BF16), st.astype(BF16))
        off = 0
        offs = {}
        for s in range(T):
            gs = s // SUB
            groups = range(0, gs + 1) if backward else range(gs, ngrp)
            bs = b[s:s + 1, :]
            ks = kk[s:s + 1, :]
            for gidx in groups:
                rs = slice(gidx * SUB, (gidx + 1) * SUB)
                gval = qs[rs] * jnp.exp2(b[rs] - bs) * ks
                if gidx == gs:
                    keep = (row <= s - gs * SUB) if backward else (row >= s - gs * SUB)
                    gval = jnp.where(keep, gval, 0.0)
                g_ref[z, off:off + SUB, :] = gval
                offs[(s, gidx)] = off
                off += SUB
        red = _dot(g_ref[z].astype(BF16), sel)
        outs = []
        for gidx in range(ngrp):
            acc = o_inter[gidx * SUB:(gidx + 1) * SUB]
            for s in range(T):
                if (s, gidx) in offs:
                    o0 = offs[(s, gidx)]
                    acc = acc + red[o0:o0 + SUB] * vv[s:s + 1, :]
            outs.append(acc)
        o_ref[0, pl.ds(r0, T), :] = jnp.concatenate(outs, axis=0)
        kt = (kk * jnp.exp2(edge - b)).astype(BF16)
        upd = lax.dot_general(vv.astype(BF16), kt, (((0,), (0,)), ((), ())), preferred_element_type=F32)
        st_ref[z] = st * jnp.exp2(edge) + upd * bd_ref[...]

    def both(c, carry):
        step(c, False)
        step(nsub - 1 - c, True)
        return carry

    lax.fori_loop(0, nsub, both, 0, unroll=4)


def _split_dot_rhs(tri, x):
    tb = tri.astype(BF16)
    acc = None
    for _ in range(3):
        hi = x.astype(BF16)
        part = _dot(tb, hi)
        acc = part if acc is None else acc + part
        x = x - hi.astype(F32)
    return acc


def _hgrn2_tables():
    T = HG_T
    r = jnp.arange(T)
    tri = jnp.stack([(r[None, :] <= r[:, None]), (r[None, :] >= r[:, None])]).astype(F32)
    head = jnp.arange(HG_WIDTH) // (HG_WIDTH // HG_HEADS)
    same = (head[:, None] == head[None, :])
    return tri, same.astype(BF16), same.astype(F32)


def _hgrn2_groups():
    T, ngrp = HG_T, HG_T // SUB
    return sum(ngrp - s // SUB for s in range(T))


def _hgrn2_scan(pb, lb):
    B, L, _ = pb.shape
    tb = min(HG_TB, L)
    nblk = L // tb
    tri, sel, bd = _hgrn2_tables()
    W = HG_WIDTH
    fwd = lambda col: pl.BlockSpec((1, tb, W), lambda b, j: (b, j, col))
    bwd = lambda col: pl.BlockSpec((1, tb, W), lambda b, j: (b, nblk - 1 - j, col))
    full = lambda a: pl.BlockSpec(a.shape, lambda b, j: (0,) * a.ndim)
    lb3 = lb.reshape(2, 1, W)
    return pl.pallas_call(
        functools.partial(_hgrn2_kernel, nsub=tb // HG_T),
        grid=(B, nblk),
        in_specs=[fwd(0), bwd(0), fwd(1), bwd(2), fwd(3), bwd(3), full(lb3), full(tri), full(sel), full(bd)],
        out_specs=[fwd(0), bwd(0)],
        out_shape=[jax.ShapeDtypeStruct((B, L, W), F32)] * 2,
        scratch_shapes=[pltpu.VMEM((2, W, W), F32), pltpu.VMEM((2, _hgrn2_groups() * SUB, W), F32)],
        compiler_params=pltpu.CompilerParams(dimension_semantics=("parallel", "arbitrary")),
        name="hgrn2_scan",
    )(pb, pb, pb, pb, pb, pb, lb3, tri, sel, bd)


HY_N2 = 128
HY_K1_TILE = 8
HY_FILT_ROWS = 512
HY_KRON_MAX_N1 = 32


def _dot_hi(a, b):
    return jnp.dot(a, b, precision=lax.Precision.HIGHEST, preferred_element_type=F32)


def _dot_3x(a, b):
    a_hi = a.astype(BF16)
    b_hi = b.astype(BF16)
    a_lo = (a - a_hi.astype(F32)).astype(BF16)
    b_lo = (b - b_hi.astype(F32)).astype(BF16)
    return _dot(a_hi, b_hi) + _dot(a_hi, b_lo) + _dot(a_lo, b_hi)


HY_FILT_PACK = LANES // HY_BANDS


def _hy_filter_kernel(w1t_ref, w1c_ref, w1s_ref, b1_ref, fr_ref, w2_ref, b2_ref, w3_ref, bands_ref,
                      dl_ref, k_ref, sum_ref, *, L):
    tr = k_ref.shape[0]
    P, Hd = HY_FILT_PACK, HY_HIDDEN
    rq = tr // P
    r0 = pl.program_id(0) * tr

    def position(r):
        return jnp.where(r < L, r, 2 * L - r).astype(F32)

    def packed_rows(width, per):
        row = lax.broadcasted_iota(jnp.int32, (rq, width), 0)
        q = lax.broadcasted_iota(jnp.int32, (rq, width), 1) // per
        return r0 + q * rq + row

    bw = (position(packed_rows(LANES, HY_BANDS)) * (2.0 * math.pi / L)) * bands_ref[...]
    t_h = position(packed_rows(P * Hd, Hd)) * (1.0 / (L - 1))
    pre = t_h * w1t_ref[...] + _dot_hi(jnp.cos(bw), w1c_ref[...]) - _dot_hi(jnp.sin(bw), w1s_ref[...])
    fr = fr_ref[...]
    a = jnp.sin(fr * (pre + b1_ref[...]))
    a = jnp.sin(fr * (_dot_hi(a, w2_ref[...]) + b2_ref[...]))
    W2 = HY_ORDER * HY_WIDTH
    a_rows = jnp.concatenate([a[:, q * Hd:(q + 1) * Hd] for q in range(P)], axis=0)
    h = _dot_3x(a_rows, w3_ref[...])
    r = r0 + lax.broadcasted_iota(jnp.int32, (tr, 1), 0)
    decay = jnp.exp(-(position(r) * (1.0 / (L - 1))) * dl_ref[...])
    hf = h[:, :W2] * decay * (r < L).astype(F32)
    hb = h[:, W2:] * decay * ((r > L) | (r == 0)).astype(F32)
    k_ref[...] = hf + hb
    total = jnp.sum(jnp.abs(hf) + jnp.abs(hb), axis=0, keepdims=True)

    @pl.when(pl.program_id(0) == 0)
    def _():
        sum_ref[...] = jnp.zeros_like(sum_ref)

    sum_ref[...] += total


def _hy_filter(L, w1, b1, freq, w2, b2, w3):
    tr = min(HY_FILT_ROWS, L)
    W2 = HY_ORDER * HY_WIDTH
    P = HY_FILT_PACK
    eye = jnp.eye(P, dtype=F32)
    tile = lambda v: jnp.tile(v.reshape(1, -1), (1, P))
    bands = tile(jnp.linspace(1e-4, HY_BANDS - 1, HY_BANDS, dtype=F32))
    w1c = jnp.kron(eye, w1[1:1 + HY_BANDS])
    w1s = jnp.kron(eye, w1[1 + HY_BANDS:])
    w2b = jnp.kron(eye, w2)
    deltas = jnp.abs(jnp.linspace(math.log(HY_DECAY_TARGET) / HY_SLOW_PCT,
                                  math.log(HY_DECAY_TARGET) / HY_FAST_PCT, HY_WIDTH, dtype=F32))
    dl = jnp.tile(deltas, HY_ORDER).reshape(1, W2)
    args = (tile(w1[0]), w1c, w1s, tile(b1), tile(freq), w2b, tile(b2), w3, bands, dl)
    full = lambda a: pl.BlockSpec(a.shape, lambda i: (0, 0))
    return pl.pallas_call(
        functools.partial(_hy_filter_kernel, L=L),
        grid=(2 * L // tr,),
        in_specs=[full(a) for a in args],
        out_specs=[pl.BlockSpec((tr, W2), lambda i: (i, 0)), pl.BlockSpec((1, W2), lambda i: (0, 0))],
        out_shape=[jax.ShapeDtypeStruct((2 * L, W2), F32), jax.ShapeDtypeStruct((1, W2), F32)],
        compiler_params=pltpu.CompilerParams(dimension_semantics=("arbitrary",)),
        name="hyena_filter",
    )(*args)


def _hy_dft_tables(L):
    N = 2 * L
    N2 = HY_N2
    N1 = N // N2
    def cs(n, rows, cols):
        ang = (2.0 * math.pi / n) * ((jnp.arange(rows)[:, None] * jnp.arange(cols)[None, :]) % n).astype(F32)
        return jnp.cos(ang), jnp.sin(ang)
    ca, sa = cs(N1, N1, N1)
    fa_full = jnp.concatenate([ca, -sa], axis=0)
    fa_inv = jnp.concatenate([ca[:N1 // 2], -sa[:N1 // 2]], axis=1) * (1.0 / N)
    cb, sb = cs(N2, N2, N2)
    fb = jnp.concatenate([jnp.concatenate([cb, sb], axis=1), jnp.concatenate([-sb, cb], axis=1)], axis=0)
    tc, ts = cs(N, N1, N2)
    tw = jnp.stack([tc, ts])[..., None] * jnp.ones((1, 1, 1, 128), F32)
    tabs = dict(N1=N1, fa_half=fa_full[:, :N1 // 2].astype(BF16), fa_full=fa_full.astype(BF16),
                fa_inv=fa_inv.astype(BF16), fb=fb.astype(BF16), fb_t=fb.T.astype(BF16), tw=tw)
    if N1 <= HY_KRON_MAX_N1:
        eye = jnp.eye(SUB, dtype=F32)
        tabs['fa_half_kron'] = jnp.kron(fa_full[:, :N1 // 2], eye).astype(BF16)
        tabs['fa_inv_kron'] = jnp.kron(fa_inv, eye).astype(BF16)
    return tabs


def _hy_n2_rows(N1):
    return SUB * max(1, min(HY_N2 // SUB, 256 // N1))


def _hy_dft_a_kernel(m_ref, x_ref, o_ref):
    N1, C = o_ref.shape[2], o_ref.shape[4]
    for g0 in range(0, x_ref.shape[2], SUB):
        x = jnp.concatenate([x_ref[0, :, g0 + j, :] for j in range(SUB)], axis=1).astype(BF16)
        y = _dot(m_ref[...], x)
        for j in range(SUB):
            o_ref[0, 0, :, g0 + j, :] = y[:N1, j * C:(j + 1) * C]
            o_ref[0, 1, :, g0 + j, :] = y[N1:, j * C:(j + 1) * C]


def _hy_dft_a(m, x):
    B, R, N2, C = x.shape
    N1 = m.shape[0] // 2
    W = HY_WIDTH
    G = _hy_n2_rows(N1)
    return pl.pallas_call(
        _hy_dft_a_kernel,
        grid=(B, N2 // G, C // W),
        in_specs=[pl.BlockSpec(m.shape, lambda b, g, c: (0, 0)),
                  pl.BlockSpec((1, R, G, W), lambda b, g, c: (b, 0, g, c))],
        out_specs=pl.BlockSpec((1, 2, N1, G, W), lambda b, g, c: (b, 0, 0, g, c)),
        out_shape=jax.ShapeDtypeStruct((B, 2, N1, N2, C), F32),
        compiler_params=pltpu.CompilerParams(dimension_semantics=("parallel", "parallel", "parallel")),
        name="hyena_dft_a",
    )(m, x)


def _hy_gate_kernel(m_ref, y_ref, z_ref, g_ref, s_ref, o_ref):
    C = o_ref.shape[3]
    for g0 in range(0, o_ref.shape[2], SUB):
        y = jnp.concatenate(
            [jnp.concatenate([y_ref[0, 0, :, g0 + j, :], y_ref[0, 1, :, g0 + j, :]], axis=0) for j in range(SUB)],
            axis=1)
        zc = _dot(m_ref[...], y.astype(BF16))
        for j in range(SUB):
            o_ref[0, :, g0 + j, :] = g_ref[0, :, g0 + j, :] * (
                zc[:, j * C:(j + 1) * C] + s_ref[...] * z_ref[0, :, g0 + j, :])


def _hy_inverse_gate(m, y, z, gate, skip_row):
    B, _, N1, N2, C = y.shape
    G = _hy_n2_rows(N1)
    tok = pl.BlockSpec((1, N1 // 2, G, C), lambda b, g: (b, 0, g, 0))
    return pl.pallas_call(
        _hy_gate_kernel,
        grid=(B, N2 // G),
        in_specs=[pl.BlockSpec(m.shape, lambda b, g: (0, 0)),
                  pl.BlockSpec((1, 2, N1, G, C), lambda b, g: (b, 0, 0, g, 0)),
                  tok, tok,
                  pl.BlockSpec((1, C), lambda b, g: (0, 0))],
        out_specs=tok,
        out_shape=jax.ShapeDtypeStruct((B, N1 // 2, N2, C), F32),
        compiler_params=pltpu.CompilerParams(dimension_semantics=("parallel", "parallel")),
        name="hyena_idft_a_gate",
    )(m, y, z, gate, skip_row)


def _cmul(ar, ai, br, bi):
    return ar * br - ai * bi, ar * bi + ai * br


def _hy_stage_b_one(ar, ai, tcos, tsin, fb, fbt, kr, ki):
    N2 = HY_N2
    tc = jnp.concatenate([tcos, tcos], axis=1)
    ts = jnp.concatenate([tsin, tsin], axis=1)
    br, bi = _cmul(ar, ai, tc, -ts)
    x = _dot(fb, jnp.concatenate([br, bi], axis=0).astype(BF16))
    yr, yi = _cmul(x[:N2], x[N2:], kr, ki)
    y = _dot(fbt, jnp.concatenate([yr, yi], axis=0).astype(BF16))
    return _cmul(y[:N2], y[N2:], tc, ts)


def _hy_fused_kernel(ma_ref, mi_ref, tw_ref, fb_ref, fbt_ref, k_ref, z_ref, g_ref, s_ref, o_ref, w_sc):
    R, N2, C = z_ref.shape[1:]
    N1 = k_ref.shape[1]
    for g0 in range(0, N2, SUB):
        x = z_ref[0, :, g0:g0 + SUB, :].reshape(R * SUB, C).astype(BF16)
        w_sc[:, :, g0:g0 + SUB, :] = _dot(ma_ref[...], x).reshape(2, N1, SUB, C)

    def k1_step(j, carry):
        yr, yi = _hy_stage_b_one(w_sc[0, j], w_sc[1, j], tw_ref[0, j], tw_ref[1, j], fb_ref[...], fbt_ref[...],
                                 k_ref[0, j], k_ref[1, j])
        w_sc[0, j] = yr
        w_sc[1, j] = yi
        return carry

    lax.fori_loop(0, N1, k1_step, 0, unroll=8)
    for g0 in range(0, N2, SUB):
        y = w_sc[:, :, g0:g0 + SUB, :].reshape(2 * N1 * SUB, C).astype(BF16)
        zc = _dot(mi_ref[...], y).reshape(R, SUB, C)
        rows = slice(g0, g0 + SUB)
        o_ref[0, :, rows, :] = g_ref[0, :, rows, :] * (zc + s_ref[...] * z_ref[0, :, rows, :])


def _hy_fused_conv(tabs, kspec, order, z, gate, skip_row):
    B, R, N2, C = z.shape
    N1 = tabs['N1']
    once = lambda a: pl.BlockSpec(a.shape, lambda b: (0,) * a.ndim, pipeline_mode=pl.Buffered(1))
    tok = pl.BlockSpec((1, R, N2, C), lambda b: (b, 0, 0, 0))
    ma, mi = tabs['fa_half_kron'], tabs['fa_inv_kron']
    return pl.pallas_call(
        _hy_fused_kernel,
        grid=(B,),
        in_specs=[once(ma), once(mi), once(tabs['tw']), once(tabs['fb']), once(tabs['fb_t']),
                  pl.BlockSpec((2, N1, N2, C), lambda b: (0, 0, 0, order), pipeline_mode=pl.Buffered(1)),
                  tok, tok, pl.BlockSpec((1, C), lambda b: (0, 0))],
        out_specs=tok,
        out_shape=jax.ShapeDtypeStruct(z.shape, F32),
        scratch_shapes=[pltpu.VMEM((2, N1, N2, C), F32)],
        compiler_params=pltpu.CompilerParams(dimension_semantics=("parallel",)),
        name="hyena_conv_fused",
    )(ma, mi, tabs['tw'], tabs['fb'], tabs['fb_t'], kspec, z, gate, skip_row)


def _hy_stage_b_kernel(a_ref, tw_ref, fb_ref, fbt_ref, k_ref, o_ref, *, conv):
    N2 = HY_N2
    for j in range(a_ref.shape[2]):
        if conv:
            xr, xi = _hy_stage_b_one(a_ref[0, 0, j], a_ref[0, 1, j], tw_ref[0, j], tw_ref[1, j], fb_ref[...],
                                     fbt_ref[...], k_ref[0, j], k_ref[1, j])
        else:
            tc = jnp.concatenate([tw_ref[0, j], tw_ref[0, j]], axis=1)
            ts = jnp.concatenate([tw_ref[1, j], tw_ref[1, j]], axis=1)
            br, bi = _cmul(a_ref[0, 0, j], a_ref[0, 1, j], tc, -ts)
            x = _dot(fb_ref[...], jnp.concatenate([br, bi], axis=0).astype(BF16))
            xr, xi = x[:N2] * k_ref[...], x[N2:] * k_ref[...]
        o_ref[0, 0, j] = xr
        o_ref[0, 1, j] = xi


def _hy_stage_b(a, tabs, kspec=None, order=0, scale=None):
    B, _, N1, N2, C = a.shape
    t1 = HY_K1_TILE
    conv = kspec is not None
    blk = pl.BlockSpec((1, 2, t1, N2, HY_WIDTH), lambda i, b, c: (b, 0, i, 0, c))
    if conv:
        k_arr = kspec
        k_spec = pl.BlockSpec((2, t1, N2, HY_WIDTH), lambda i, b, c: (0, i, 0, order))
    else:
        k_arr = scale
        k_spec = pl.BlockSpec((1, HY_WIDTH), lambda i, b, c: (0, c))
    return pl.pallas_call(
        functools.partial(_hy_stage_b_kernel, conv=conv),
        grid=(N1 // t1, B, C // HY_WIDTH),
        in_specs=[blk,
                  pl.BlockSpec((2, t1, N2, 128), lambda i, b, c: (0, i, 0, 0)),
                  pl.BlockSpec(tabs['fb'].shape, lambda i, b, c: (0, 0)),
                  pl.BlockSpec(tabs['fb_t'].shape, lambda i, b, c: (0, 0)),
                  k_spec],
        out_specs=blk,
        out_shape=jax.ShapeDtypeStruct(a.shape, F32),
        compiler_params=pltpu.CompilerParams(dimension_semantics=("parallel", "parallel", "parallel")),
        name="hyena_stage_b_conv" if conv else "hyena_stage_b_spectrum",
    )(a, tabs['tw'], tabs['fb'], tabs['fb_t'], k_arr)


def _hy_filter_spectrum(L, tabs, w1, b1, freq, w2, b2, w3):
    N1, N2 = tabs['N1'], HY_N2
    W2 = HY_ORDER * HY_WIDTH
    k, total = _hy_filter(L, w1, b1, freq, w2, b2, w3)
    a = _hy_dft_a(tabs['fa_full'], k.reshape(1, N1, N2, W2))
    return _hy_stage_b(a, tabs, scale=1.0 / total)[0]


def _hyena(v, x1, x2, skip, tabs, kspec):
    B, L, _ = v.shape
    N1, N2, C = tabs['N1'], HY_N2, HY_WIDTH
    grid = lambda t: t.reshape(B, N1 // 2, N2, C)
    z = grid(v)
    for n, gate in enumerate((x1, x2)):
        if 'fa_half_kron' in tabs:
            z = _hy_fused_conv(tabs, kspec, n, z, grid(gate), skip[n].reshape(1, C))
        else:
            a = _hy_dft_a(tabs['fa_half'], z)
            y = _hy_stage_b(a, tabs, kspec, order=n)
            z = _hy_inverse_gate(tabs['fa_inv'], y, z, grid(gate), skip[n].reshape(1, C))
    return z.reshape(B, L, C)


def _prep_layer_weights(l, w_in, w_out, na_rpb, mla_w_uq, mla_w_ukv, ffn_w_up, ffn_conv, ffn_w_down):
    wi = w_in[l]
    s0, s1, s2 = NA_COLS, NA_COLS + HG_COLS, NA_COLS + HG_COLS + HY_COLS
    w_ml = jnp.pad(wi[:, s2:], ((0, 0), (0, MLA_COLS_PAD - MLA_COLS)))
    return dict(
        w_na=wi[:, :s0].astype(BF16), w_hg=wi[:, s0:s1].astype(BF16), w_hy=wi[:, s1:s2].astype(BF16),
        w_ml=w_ml.astype(BF16), w_out=w_out[l].astype(BF16), na_bias=_na_bias_table(na_rpb[l]),
        mla=_mla_weights(mla_w_uq[l], mla_w_ukv[l]),
        ffn=_ffn_weight_tiles(ffn_w_up[l], ffn_conv[l], ffn_w_down[l]),
    )


def _trunk_layer(x, mod, lb, g, lw, p, cos, sin, hy_tabs, hy_spec):
    L = x.shape[1]
    pa, pb, hv, hx1, hx2, q, k, v = _in_proj(x, mod, g, lw['w_na'], lw['w_hg'], lw['w_hy'], lw['w_ml'], p['hy_short'],
                                             cos[:L], sin[:L], p['mla_q_norm'], p['mla_kv_norm'], lw['mla'])
    o_na = _neighbourhood_attention(pa, lw['na_bias'])
    hg_dirs = _hgrn2_scan(pb, lb)
    o_hy = _hyena(hv, hx1, hx2, p['hy_skip'], hy_tabs, hy_spec)
    o_ml = _mla_flash(q, k, v)
    x1, h2 = _out_proj(x, mod, g, o_na, hg_dirs, pb, p['hg_norm'], o_hy, o_ml, lw['w_out'])
    return _conv_ffn(h2, x1, mod, g, *lw['ffn'])


def kernel(x_prompt, x_sample, c_prompt, c_sample, ada_w, ada_b, norm_g, w_in, w_out, na_rpb, hg_lb,
           hg_norm, hy_short, hy_w1, hy_b1, hy_freq, hy_w2, hy_b2, hy_w3, hy_skip, mla_q_norm,
           mla_kv_norm, mla_w_uq, mla_w_ukv, ffn_w_up, ffn_conv, ffn_w_down):
    Bp, Bs = x_prompt.shape[0], x_sample.shape[0]
    Lp, Ls = x_prompt.shape[1], x_sample.shape[1]
    lb_soft = jax.nn.softmax(hg_lb.astype(F32), axis=0)
    lower_bounds = jnp.cumsum(lb_soft, axis=0) - lb_soft[:1]

    R = -(-(Bp + Bs) // 8) * 8
    cond = jnp.zeros((R, D_MODEL), F32).at[:Bp].set(c_prompt).at[Bp:Bp + Bs].set(c_sample)
    mod = _ada_modulation(cond, ada_w, ada_b).reshape(DEPTH, R, 6, D_MODEL)
    mod = jnp.pad(mod, ((0, 0), (0, 0), (0, 2), (0, 0)))

    cos, sin = _rope_tables(max(Lp, Ls))
    tabs_p, tabs_s = _hy_dft_tables(Lp), _hy_dft_tables(Ls)
    y_prompt, y_sample = x_prompt, x_sample
    for l in range(DEPTH):
        lw = _prep_layer_weights(l, w_in, w_out, na_rpb, mla_w_uq, mla_w_ukv, ffn_w_up, ffn_conv, ffn_w_down)
        p = dict(hg_norm=hg_norm[l], hy_short=hy_short[l], hy_skip=hy_skip[l],
                 mla_q_norm=mla_q_norm[l].reshape(1, -1), mla_kv_norm=mla_kv_norm[l].reshape(1, -1))
        filt = (hy_w1[l], hy_b1[l], hy_freq[l], hy_w2[l], hy_b2[l], hy_w3[l])
        spec_p = _hy_filter_spectrum(Lp, tabs_p, *filt)
        spec_s = _hy_filter_spectrum(Ls, tabs_s, *filt)
        y_prompt = _trunk_layer(y_prompt, mod[l, :Bp], lower_bounds[l], norm_g[l], lw, p, cos, sin,
                                tabs_p, spec_p)
        y_sample = _trunk_layer(y_sample, mod[l, Bp:Bp + Bs], lower_bounds[l], norm_g[l], lw, p, cos, sin,
                                tabs_s, spec_s)
    return (y_prompt, y_sample)
```

```python
import functools
import math

import jax
import jax.numpy as jnp
from jax import lax
from jax.experimental import pallas as pl
from jax.experimental.pallas import tpu as pltpu

F32 = jnp.float32
BF16 = jnp.bfloat16

D_MODEL = 1024
DEPTH = 4
GRID_W = 64
HEAD_DIM = 64
NA_WIDTH = 256
NA_HEADS = 4
NA_KH = 8
NA_KW = 16
HG_WIDTH = 256
HG_HEADS = 4
HY_WIDTH = 256
HY_ORDER = 2
HY_BANDS = 16
HY_HIDDEN = 64
HY_DECAY_TARGET = 1e-2
HY_FAST_PCT = 0.3
HY_SLOW_PCT = 1.5
MLA_WIDTH = 256
MLA_HEADS = 4
MLA_NOPE = 64
MLA_ROPE = 32
MLA_V = 64
MLA_Q_LORA = 256
MLA_KV_LORA = 128
MLA_HEAD_PAD = 128
ROPE_THETA = 10000.0
D_FF = 2816
EPS = 1e-6
NA_COLS = 3 * NA_WIDTH
HG_COLS = 5 * HG_WIDTH
HY_COLS = 3 * HY_WIDTH
MLA_COLS = MLA_Q_LORA + MLA_KV_LORA + MLA_ROPE
MLA_COLS_PAD = 512
NEG_BIG = -1e30
SUB = 8
LANES = 128


def _dot(a, b):
    return jnp.dot(a, b, preferred_element_type=F32)


def _dot_nt(a, b):
    return lax.dot_general(a, b, (((1,), (1,)), ((), ())), preferred_element_type=F32)


def _rms(x, g):
    return x * lax.rsqrt(jnp.mean(x * x, axis=-1, keepdims=True) + EPS) * g


def _ada_kernel(c_ref, w_ref, b_ref, o_ref):
    c = c_ref[...]
    s = c * jax.nn.sigmoid(c)
    o_ref[0] = _dot(s.astype(BF16), w_ref[0].astype(BF16)) + b_ref[0]


def _ada_modulation(cond, ada_w, ada_b):
    R = cond.shape[0]
    tn = 1536
    return pl.pallas_call(
        _ada_kernel,
        grid=(DEPTH, 6 * D_MODEL // tn),
        in_specs=[
            pl.BlockSpec((R, D_MODEL), lambda l, n: (0, 0)),
            pl.BlockSpec((1, D_MODEL, tn), lambda l, n: (l, 0, n)),
            pl.BlockSpec((1, 1, tn), lambda l, n: (l, 0, n)),
        ],
        out_specs=pl.BlockSpec((1, R, tn), lambda l, n: (l, 0, n)),
        out_shape=jax.ShapeDtypeStruct((DEPTH, R, 6 * D_MODEL), F32),
        name="ada_modulation",
    )(cond, ada_w, ada_b.reshape(DEPTH, 1, 6 * D_MODEL))


IN_HALO = 16


def _in_proj_kernel(x_ref, xp_ref, xn_ref, mod_ref, g_ref, wna, whg, why, wml, sw_ref,
                    cos_ref, sin_ref, qn_ref, kvn_ref, wq, wqr, wk, wv, we, wer,
                    ona, ohg, ov, ox1, ox2, oq, ok, ovv, hext_ref):
    i = pl.program_id(1)
    tm = x_ref.shape[1]
    H = IN_HALO
    g = g_ref[0:1, :]
    scale = 1.0 + mod_ref[0, 1:2, :]
    shift = mod_ref[0, 0:1, :]
    prep = lambda x: (_rms(x, g) * scale + shift).astype(BF16)
    hext_ref[0:H, :] = prep(xp_ref[0])
    hext_ref[H:H + tm, :] = prep(x_ref[0])
    hext_ref[H + tm:2 * H + tm, :] = prep(xn_ref[0])
    hb = hext_ref[H:H + tm, :]
    ona[0] = _dot(hb, wna[...])
    ohg[0] = _dot(hb, whg[...])
    _mla_qkv_tile(_dot(hb, wml[...]), cos_ref[...], sin_ref[...], qn_ref[...], kvn_ref[...], wq[...], wqr[...],
                  wk[...], wv[...], we[...], wer[...], oq, ok, ovv)
    ue = _dot(hext_ref[...], why[...])
    u = ue[H:H + tm]
    u_prev = ue[H - 1:H] * (i > 0).astype(F32)
    u_next = ue[H + tm:H + tm + 1] * (i < pl.num_programs(1) - 1).astype(F32)
    sub = lax.broadcasted_iota(jnp.int32, (SUB, 1), 0)
    down = pltpu.roll(u, 1, 0)
    up = pltpu.roll(u, tm - 1, 0)
    down = jnp.concatenate([jnp.where(sub == 0, u_prev, down[:SUB]), down[SUB:]], axis=0)
    up = jnp.concatenate([up[:tm - SUB], jnp.where(sub == SUB - 1, u_next, up[tm - SUB:])], axis=0)
    y = sw_ref[0:1, :] * down + sw_ref[1:2, :] * u + sw_ref[2:3, :] * up
    W = HY_WIDTH
    ov[0] = y[:, 0 * W:1 * W]
    ox1[0] = y[:, 1 * W:2 * W]
    ox2[0] = y[:, 2 * W:3 * W]


def _in_proj(x, mod, g, w_na, w_hg, w_hy, w_ml, short_w, cos, sin, q_norm, kv_norm, mla_w, tm=512):
    B, L, D = x.shape
    tm = min(tm, L)
    nh = tm // IN_HALO
    last = L // IN_HALO - 1
    full = lambda a: pl.BlockSpec(a.shape, lambda b, i: (0, 0))
    rope = pl.BlockSpec((tm, MLA_HEAD_PAD), lambda b, i: (i, 0))
    HP = MLA_HEADS * MLA_HEAD_PAD
    outs = [(NA_COLS, F32), (HG_COLS, F32), (HY_WIDTH, F32), (HY_WIDTH, F32), (HY_WIDTH, F32),
            (HP, BF16), (HP, BF16), (MLA_WIDTH, BF16)]
    return pl.pallas_call(
        _in_proj_kernel,
        grid=(B, L // tm),
        in_specs=[
            pl.BlockSpec((1, tm, D), lambda b, i: (b, i, 0)),
            pl.BlockSpec((1, IN_HALO, D), lambda b, i: (b, jnp.maximum(i * nh - 1, 0), 0)),
            pl.BlockSpec((1, IN_HALO, D), lambda b, i: (b, jnp.minimum((i + 1) * nh, last), 0)),
            pl.BlockSpec((1, 8, D), lambda b, i: (b, 0, 0)),
            full(g), full(w_na), full(w_hg), full(w_hy), full(w_ml), full(short_w),
            rope, rope, full(q_norm), full(kv_norm),
        ] + [full(w) for w in mla_w],
        out_specs=[pl.BlockSpec((1, tm, n), lambda b, i: (b, i, 0)) for n, _ in outs],
        out_shape=[jax.ShapeDtypeStruct((B, L, n), dt) for n, dt in outs],
        scratch_shapes=[pltpu.VMEM((tm + 2 * IN_HALO, D), BF16)],
        compiler_params=pltpu.CompilerParams(dimension_semantics=("parallel", "parallel")),
        name="in_proj",
    )(x, x, x, mod, g, w_na, w_hg, w_hy, w_ml, short_w, cos, sin, q_norm, kv_norm, *mla_w)


def _head_mean_sq(o, sel):
    sq = o * o
    hi = sq.astype(BF16)
    lo = (sq - hi.astype(F32)).astype(BF16)
    return (_dot(hi, sel) + _dot(lo, sel)) * (1.0 / (HG_WIDTH // HG_HEADS))


def _out_proj_kernel(x_ref, mod_ref, g_ref, na, hgf, hgb, hgg, hgn, sel, hy, ml, w_ref, x1_ref, h2_ref):
    W = NA_WIDTH
    o = hgf[0] + hgb[0]
    gate = hgg[0]
    hg = o * lax.rsqrt(_head_mean_sq(o, sel[...]) + EPS) * hgn[...] * (gate * jax.nn.sigmoid(gate))
    mix = _dot(na[0], w_ref[0 * W:1 * W, :])
    mix += _dot(hg.astype(BF16), w_ref[1 * W:2 * W, :])
    mix += _dot(hy[0].astype(BF16), w_ref[2 * W:3 * W, :])
    mix += _dot(ml[0], w_ref[3 * W:4 * W, :])
    x1 = x_ref[0] + mod_ref[0, 2:3, :] * _rms(mix, g_ref[1:2, :])
    x1_ref[0] = x1
    h2 = _rms(x1, g_ref[2:3, :]) * (1.0 + mod_ref[0, 4:5, :]) + mod_ref[0, 3:4, :]
    h2_ref[0] = h2.astype(BF16)


def _out_proj(x, mod, g, o_na, hg_dirs, pb, hg_norm, o_hy, o_ml, w_out, tm=512):
    B, L, D = x.shape
    tm = min(tm, L)
    tok = lambda n: pl.BlockSpec((1, tm, n), lambda b, i: (b, i, 0))
    sel = _hgrn2_tables()[1]
    return pl.pallas_call(
        _out_proj_kernel,
        grid=(B, L // tm),
        in_specs=[
            tok(D),
            pl.BlockSpec((1, 8, D), lambda b, i: (b, 0, 0)),
            pl.BlockSpec(g.shape, lambda b, i: (0, 0)),
            tok(NA_WIDTH),
            tok(HG_WIDTH), tok(HG_WIDTH),
            pl.BlockSpec((1, tm, HG_WIDTH), lambda b, i: (b, i, 4)),
            pl.BlockSpec((1, HG_WIDTH), lambda b, i: (0, 0)),
            pl.BlockSpec(sel.shape, lambda b, i: (0, 0)),
            tok(HY_WIDTH), tok(MLA_WIDTH),
            pl.BlockSpec(w_out.shape, lambda b, i: (0, 0)),
        ],
        out_specs=[tok(D), tok(D)],
        out_shape=[jax.ShapeDtypeStruct((B, L, D), F32), jax.ShapeDtypeStruct((B, L, D), BF16)],
        compiler_params=pltpu.CompilerParams(dimension_semantics=("parallel", "parallel")),
        name="out_proj",
    )(x, mod, g, o_na, hg_dirs[0], hg_dirs[1], pb, hg_norm.reshape(1, HG_WIDTH), sel, o_hy, o_ml, w_out)


FFN_HALO = 16


def _ffn_kernel(hm_ref, hp_ref, hn_ref, x_ref, mod_ref, g_ref, wu_ref, cw_ref, wd_ref, o_ref, acc_ref,
                hext_ref):
    i = pl.program_id(1)
    tm = hm_ref.shape[1]
    nf = wd_ref.shape[0]
    not_first = (i > 0).astype(F32)
    not_last = (i < pl.num_programs(1) - 1).astype(F32)
    sub = lax.broadcasted_iota(jnp.int32, (SUB, 1), 0)

    H = FFN_HALO
    hext_ref[0:H, :] = hp_ref[0]
    hext_ref[H:H + tm, :] = hm_ref[0]
    hext_ref[H + tm:2 * H + tm, :] = hn_ref[0]

    def conv_branch(t):
        c = cw_ref[t]
        ue = _dot(hext_ref[...], wu_ref[t])
        u = ue[H:H + tm]
        u_prev = ue[H - 1:H] * not_first
        u_next = ue[H + tm:H + tm + 1] * not_last
        down = pltpu.roll(u, 1, 0)
        up = pltpu.roll(u, tm - 1, 0)
        down = jnp.concatenate([jnp.where(sub == 0, u_prev, down[:SUB]), down[SUB:]], axis=0)
        up = jnp.concatenate([up[:tm - SUB], jnp.where(sub == SUB - 1, u_next, up[tm - SUB:])], axis=0)
        return c[0:1, :] * down + c[1:2, :] * u + c[2:3, :] * up

    def tile(f, carry):
        a = conv_branch(f)
        b = conv_branch(nf + f)
        act = (jax.nn.gelu(a, approximate=True) * b).astype(BF16)
        acc_ref[...] += _dot(act, wd_ref[f])
        return carry

    acc_ref[...] = jnp.zeros_like(acc_ref)
    lax.fori_loop(0, nf, tile, 0, unroll=FFN_UNROLL)
    o_ref[0] = x_ref[0] + mod_ref[0, 5:6, :] * _rms(acc_ref[...], g_ref[3:4, :])


FFN_TF = 256
FFN_UNROLL = 2


def _conv_ffn(h2, x1, mod, g, w_up, conv_w, w_down, tm=1024):
    B, L, D = x1.shape
    tm = min(tm, L)
    nh = tm // FFN_HALO
    last_halo = L // FFN_HALO - 1
    once = lambda a: pl.BlockSpec(a.shape, lambda b, i: (0,) * a.ndim, pipeline_mode=pl.Buffered(1))
    return pl.pallas_call(
        _ffn_kernel,
        grid=(B, L // tm),
        in_specs=[
            pl.BlockSpec((1, tm, D), lambda b, i: (b, i, 0)),
            pl.BlockSpec((1, FFN_HALO, D), lambda b, i: (b, jnp.maximum(i * nh - 1, 0), 0)),
            pl.BlockSpec((1, FFN_HALO, D), lambda b, i: (b, jnp.minimum((i + 1) * nh, last_halo), 0)),
            pl.BlockSpec((1, tm, D), lambda b, i: (b, i, 0)),
            pl.BlockSpec((1, 8, D), lambda b, i: (b, 0, 0)),
            pl.BlockSpec(g.shape, lambda b, i: (0, 0)),
            once(w_up), once(conv_w), once(w_down),
        ],
        out_specs=pl.BlockSpec((1, tm, D), lambda b, i: (b, i, 0)),
        out_shape=jax.ShapeDtypeStruct((B, L, D), F32),
        scratch_shapes=[pltpu.VMEM((tm, D), F32), pltpu.VMEM((tm + 2 * FFN_HALO, D), BF16)],
        compiler_params=pltpu.CompilerParams(dimension_semantics=("parallel", "parallel")),
        name="conv_ffn",
    )(h2, h2, h2, x1, mod, g, w_up, conv_w, w_down)


def _ffn_weight_tiles(w_up, conv_w, w_down):
    D = w_up.shape[0]
    n2 = 2 * D_FF // FFN_TF
    wu = w_up.reshape(D, n2, FFN_TF).transpose(1, 0, 2).astype(BF16)
    cw = conv_w.reshape(3, n2, FFN_TF).transpose(1, 0, 2)
    wd = w_down.reshape(D_FF // FFN_TF, FFN_TF, D).astype(BF16)
    return wu, cw, wd


NA_RB = 8


def _na_kernel(q_ref, kp_ref, kc_ref, kn_ref, vp_ref, vc_ref, vn_ref, bias_ref, o_ref,
               kbuf, vbuf, *, rows):
    i = pl.program_id(1)
    blk = NA_RB * GRID_W
    win = NA_KH * GRID_W
    kbuf[0 * blk:1 * blk, :] = kp_ref[0].astype(BF16)
    kbuf[1 * blk:2 * blk, :] = kc_ref[0].astype(BF16)
    kbuf[2 * blk:3 * blk, :] = kn_ref[0].astype(BF16)
    vbuf[0 * blk:1 * blk, :] = vp_ref[0].astype(BF16)
    vbuf[1 * blk:2 * blk, :] = vc_ref[0].astype(BF16)
    vbuf[2 * blk:3 * blk, :] = vn_ref[0].astype(BF16)
    scale = HEAD_DIM ** -0.5 * math.log2(math.e)
    lane_head = lax.broadcasted_iota(jnp.int32, (1, NA_WIDTH), 1) // HEAD_DIM
    for j in range(NA_RB):
        r = i * NA_RB + j
        start = jnp.clip(r - NA_KH // 2, 0, rows - NA_KH)
        loc = start - (i - 1) * NA_RB
        dr0 = start - r + (NA_KH - 1)
        off = pl.multiple_of(loc * GRID_W, GRID_W)
        kw = kbuf[pl.ds(off, win), :]
        vw = vbuf[pl.ds(off, win), :]
        qj = (q_ref[0, j * GRID_W:(j + 1) * GRID_W, :] * scale).astype(BF16)
        qbd = jnp.concatenate([jnp.where(lane_head == h, qj, jnp.zeros_like(qj)) for h in range(NA_HEADS)],
                              axis=0)
        s = _dot_nt(qbd, kw) + bias_ref[dr0]
        m = jnp.max(s, axis=-1, keepdims=True)
        p = jnp.exp2(s - m)
        l = jnp.sum(p, axis=-1, keepdims=True)
        pv = _dot(p.astype(BF16), vw) / l
        o = None
        for h in range(NA_HEADS):
            part = jnp.where(lane_head == h, pv[h * GRID_W:(h + 1) * GRID_W], 0.0)
            o = part if o is None else o + part
        o_ref[0, j * GRID_W:(j + 1) * GRID_W, :] = o.astype(o_ref.dtype)


def _na_bias_table(rpb):
    c = jnp.arange(GRID_W)
    dc = jnp.clip(c[None, :] - c[:, None] + (NA_KW - 1), 0, 2 * NA_KW - 2)
    col_start = jnp.clip(c - NA_KW // 2, 0, GRID_W - NA_KW)
    ok = (c[None, :] >= col_start[:, None]) & (c[None, :] < col_start[:, None] + NA_KW)
    e = jnp.where(ok[None, None], rpb.astype(F32)[:, :, dc], NEG_BIG)
    t = jnp.stack([e[:, d0:d0 + NA_KH] for d0 in range(NA_KH)], axis=1)
    t = t.transpose(1, 0, 3, 2, 4).reshape(NA_KH, NA_HEADS * GRID_W, NA_KH * GRID_W)
    return jnp.where(t > 0.5 * NEG_BIG, t * math.log2(math.e), NEG_BIG)


def _neighbourhood_attention(pa, bias):
    B, L, _ = pa.shape
    rows = L // GRID_W
    blk = NA_RB * GRID_W
    nblk = rows // NA_RB
    spec = lambda col, shift: pl.BlockSpec(
        (1, blk, NA_WIDTH), lambda b, i: (b, jnp.clip(i + shift, 0, nblk - 1), col))
    return pl.pallas_call(
        functools.partial(_na_kernel, rows=rows),
        grid=(B, nblk),
        in_specs=[spec(0, 0), spec(1, -1), spec(1, 0), spec(1, 1), spec(2, -1), spec(2, 0), spec(2, 1),
                  pl.BlockSpec(bias.shape, lambda b, i: (0, 0, 0))],
        out_specs=pl.BlockSpec((1, blk, NA_WIDTH), lambda b, i: (b, i, 0)),
        out_shape=jax.ShapeDtypeStruct((B, L, NA_WIDTH), BF16),
        scratch_shapes=[pltpu.VMEM((3 * blk, NA_WIDTH), BF16), pltpu.VMEM((3 * blk, NA_WIDTH), BF16)],
        compiler_params=pltpu.CompilerParams(dimension_semantics=("parallel", "parallel")),
        name="neighbourhood_attention",
    )(pa, pa, pa, pa, pa, pa, pa, bias)


def _rope_table_kernel(inv_ref, cos_ref, sin_ref):
    t = cos_ref.shape[0]
    pos = (pl.program_id(0) * t + lax.broadcasted_iota(jnp.int32, (t, MLA_HEAD_PAD), 0)).astype(F32)
    ang = pos * inv_ref[...]
    cos_ref[...] = jnp.cos(ang)
    sin_ref[...] = jnp.sin(ang)


def _rope_tables(L):
    half = MLA_ROPE // 2
    inv = ROPE_THETA ** (-jnp.arange(half, dtype=F32) / half)
    inv_row = jnp.zeros((1, MLA_HEAD_PAD), F32).at[0, MLA_NOPE:MLA_NOPE + MLA_ROPE].set(jnp.tile(inv, 2))
    t = min(L, 1024)
    return pl.pallas_call(
        _rope_table_kernel,
        grid=(L // t,),
        in_specs=[pl.BlockSpec((1, MLA_HEAD_PAD), lambda i: (0, 0))],
        out_specs=[pl.BlockSpec((t, MLA_HEAD_PAD), lambda i: (i, 0))] * 2,
        out_shape=[jax.ShapeDtypeStruct((L, MLA_HEAD_PAD), F32)] * 2,
        name="rope_tables",
    )(inv_row)


def _mla_qkv_tile(pd, cos, sin, q_norm, kv_norm, wq, wqr, wk, wv, we, wer, q_out, k_out, v_out):
    nq = _rms(pd[:, :MLA_Q_LORA], q_norm).astype(BF16)
    nkv = _rms(pd[:, MLA_Q_LORA:MLA_Q_LORA + MLA_KV_LORA], kv_norm).astype(BF16)
    kr = pd[:, MLA_Q_LORA + MLA_KV_LORA:].astype(BF16)
    q = _dot(nq, wq)
    q_rot = _dot(nq, wqr)
    k = _dot(nkv, wk) + _dot(kr, we)
    k_rot = _dot(kr, wer)
    scale = (MLA_NOPE + MLA_ROPE) ** -0.5 * math.log2(math.e)
    for h in range(MLA_HEADS):
        hs = slice(h * MLA_HEAD_PAD, (h + 1) * MLA_HEAD_PAD)
        q_out[0, :, hs] = ((q[:, hs] * cos + q_rot[:, hs] * sin) * scale).astype(BF16)
        k_out[0, :, hs] = (k[:, hs] * cos + k_rot[:, hs] * sin).astype(BF16)
    v_out[0] = _dot(nkv, wv).astype(BF16)


def _mla_weights(w_uq, w_ukv):
    half = MLA_ROPE // 2
    P = jnp.zeros((MLA_ROPE, MLA_ROPE), F32)
    P = P.at[jnp.arange(half) + half, jnp.arange(half)].set(-1.0)
    P = P.at[jnp.arange(half), jnp.arange(half) + half].set(1.0)
    HP = MLA_HEAD_PAD
    wq = jnp.zeros((MLA_Q_LORA, MLA_HEADS * HP), F32)
    wqr = jnp.zeros_like(wq)
    wk = jnp.zeros((MLA_KV_LORA, MLA_HEADS * HP), F32)
    wv = jnp.zeros((MLA_KV_LORA, MLA_HEADS * MLA_V), F32)
    we = jnp.zeros((MLA_COLS_PAD - MLA_Q_LORA - MLA_KV_LORA, MLA_HEADS * HP), F32)
    wer = jnp.zeros_like(we)
    eye = jnp.eye(MLA_ROPE, dtype=F32)
    for h in range(MLA_HEADS):
        qh = w_uq[:, h * (MLA_NOPE + MLA_ROPE):(h + 1) * (MLA_NOPE + MLA_ROPE)]
        wq = wq.at[:, h * HP:h * HP + MLA_NOPE + MLA_ROPE].set(qh)
        wqr = wqr.at[:, h * HP + MLA_NOPE:h * HP + MLA_NOPE + MLA_ROPE].set(qh[:, MLA_NOPE:] @ P)
        kvh = w_ukv[:, h * (MLA_NOPE + MLA_V):(h + 1) * (MLA_NOPE + MLA_V)]
        wk = wk.at[:, h * HP:h * HP + MLA_NOPE].set(kvh[:, :MLA_NOPE])
        wv = wv.at[:, h * MLA_V:(h + 1) * MLA_V].set(kvh[:, MLA_NOPE:])
        we = we.at[:MLA_ROPE, h * HP + MLA_NOPE:h * HP + MLA_NOPE + MLA_ROPE].set(eye)
        wer = wer.at[:MLA_ROPE, h * HP + MLA_NOPE:h * HP + MLA_NOPE + MLA_ROPE].set(P)
    return tuple(w.astype(BF16) for w in (wq, wqr, wk, wv, we, wer))


FLASH_ROWS = 16
FLASH_HEADS = 4


def _flash_kernel(q_ref, k_ref, v_ref, o_ref, m_sc, acc_sc, p_sc):
    kv = pl.program_id(3)
    _, tq, tk = p_sc.shape
    RB = FLASH_ROWS

    @pl.when(kv == 0)
    def _():
        m_sc[...] = jnp.full_like(m_sc, -jnp.inf)
        acc_sc[...] = jnp.zeros_like(acc_sc)

    first_head = lax.broadcasted_iota(jnp.int32, (1, LANES), 1) < MLA_V
    for h in range(FLASH_HEADS):
        q = q_ref[0, :, h * MLA_HEAD_PAD:(h + 1) * MLA_HEAD_PAD]
        k = k_ref[0, :, h * MLA_HEAD_PAD:(h + 1) * MLA_HEAD_PAD]
        s = _dot_nt(q, k)
        alphas = []
        for r in range(tq // RB):
            rows = slice(r * RB, (r + 1) * RB)
            x = s[rows, 0:LANES]
            for c in range(1, tk // LANES):
                x = jnp.maximum(x, s[rows, c * LANES:(c + 1) * LANES])
            m_prev = m_sc[h, rows, :]
            m_new = jnp.maximum(m_prev, jnp.max(x, axis=-1, keepdims=True))
            alphas.append(jnp.exp2(m_prev - m_new))
            m_sc[h, rows, :] = m_new
            for c in range(tk // LANES):
                cols = slice(c * LANES, (c + 1) * LANES)
                p_sc[h, rows, cols] = jnp.exp2((s[rows, cols] - m_new).astype(BF16))
        v = v_ref[0, :, (h // 2) * LANES:(h // 2 + 1) * LANES]
        ones = jnp.ones_like(v)
        v_aug = jnp.where(first_head, v, ones) if h % 2 == 0 else jnp.where(first_head, ones, v)
        acc_sc[h] = jnp.concatenate(alphas, axis=0) * acc_sc[h] + _dot(p_sc[h], v_aug)

    @pl.when(kv == pl.num_programs(3) - 1)
    def _():
        for g in range(FLASH_HEADS // 2):
            a0 = acc_sc[2 * g]
            a1 = acc_sc[2 * g + 1]
            num = jnp.where(first_head, a0, a1)
            den = jnp.where(first_head, pltpu.roll(a0, MLA_V, 1), pltpu.roll(a1, MLA_V, 1))
            o_ref[0, :, g * LANES:(g + 1) * LANES] = (num / den).astype(o_ref.dtype)


def _mla_flash(q, k, v, tq=512, tk=2048):
    B, L, _ = q.shape
    tq = min(tq, L)
    tk = min(tk, L)
    nh = FLASH_HEADS
    stat = pltpu.VMEM((nh, tq, LANES), F32)
    return pl.pallas_call(
        _flash_kernel,
        grid=(B, MLA_HEADS // nh, L // tq, L // tk),
        in_specs=[
            pl.BlockSpec((1, tq, nh * MLA_HEAD_PAD), lambda b, h, i, j: (b, i, h)),
            pl.BlockSpec((1, tk, nh * MLA_HEAD_PAD), lambda b, h, i, j: (b, j, h)),
            pl.BlockSpec((1, tk, nh * MLA_V), lambda b, h, i, j: (b, j, h)),
        ],
        out_specs=pl.BlockSpec((1, tq, nh * MLA_V), lambda b, h, i, j: (b, i, h)),
        out_shape=jax.ShapeDtypeStruct((B, L, MLA_WIDTH), BF16),
        scratch_shapes=[stat, stat, pltpu.VMEM((nh, tq, tk), BF16)],
        compiler_params=pltpu.CompilerParams(
            dimension_semantics=("parallel", "parallel", "parallel", "arbitrary")),
        name="mla_flash",
    )(q, k, v)


HG_T = 32
HG_TB = 2048


def _hgrn2_kernel(qf_ref, qb_ref, ff_ref, fb_ref, if_ref, ib_ref, lb_ref, tri_ref, sel_ref, bd_ref,
                  of_ref, ob_ref, st_ref, g_ref, *, nsub):
    T = HG_T
    ngrp = T // SUB

    @pl.when(pl.program_id(1) == 0)
    def _():
        st_ref[...] = jnp.zeros_like(st_ref)

    sel = sel_ref[...]
    row = lax.broadcasted_iota(jnp.int32, (SUB, HG_WIDTH), 0)

    def step(c, backward):
        z = 1 if backward else 0
        q_ref, f_ref, i_ref, o_ref = (qb_ref, fb_ref, ib_ref, ob_ref) if backward else (qf_ref, ff_ref, if_ref, of_ref)
        lb = lb_ref[z]
        r0 = pl.multiple_of(c * T, T)
        qs = q_ref[0, pl.ds(r0, T), :]
        qs = qs * jax.nn.sigmoid(qs)
        fg = lb + (1.0 - lb) * jax.nn.sigmoid(f_ref[0, pl.ds(r0, T), :])
        kk = 1.0 - fg
        vv = i_ref[0, pl.ds(r0, T), :]
        b = _split_dot_rhs(tri_ref[z], jnp.log(fg) * math.log2(math.e))
        edge = b[0:1, :] if backward else b[T - 1:T, :]
        st = st_ref[z]
        o_inter = _dot_nt((qs * jnp.exp2(b)).astype(BF16), st.astype(BF16))
        off = 0
        offs = {}
        for s in range(T):
            gs = s // SUB
            groups = range(0, gs + 1) if backward else range(gs, ngrp)
            bs = b[s:s + 1, :]
            ks = kk[s:s + 1, :]
            for gidx in groups:
                rs = slice(gidx * SUB, (gidx + 1) * SUB)
                gval = qs[rs] * jnp.exp2(b[rs] - bs) * ks
                if gidx == gs:
                    keep = (row <= s - gs * SUB) if backward else (row >= s - gs * SUB)
                    gval = jnp.where(keep, gval, 0.0)
                g_ref[z, off:off + SUB, :] = gval
                offs[(s, gidx)] = off
                off += SUB
        red = _dot(g_ref[z].astype(BF16), sel)
        outs = []
        for gidx in range(ngrp):
            acc = o_inter[gidx * SUB:(gidx + 1) * SUB]
            for s in range(T):
                if (s, gidx) in offs:
                    o0 = offs[(s, gidx)]
                    acc = acc + red[o0:o0 + SUB] * vv[s:s + 1, :]
            outs.append(acc)
        o_ref[0, pl.ds(r0, T), :] = jnp.concatenate(outs, axis=0)
        kt = (kk * jnp.exp2(edge - b)).astype(BF16)
        upd = lax.dot_general(vv.astype(BF16), kt, (((0,), (0,)), ((), ())), preferred_element_type=F32)
        st_ref[z] = st * jnp.exp2(edge) + upd * bd_ref[...]

    def both(c, carry):
        step(c, False)
        step(nsub - 1 - c, True)
        return carry

    lax.fori_loop(0, nsub, both, 0, unroll=8)


def _split_dot_rhs(tri, x):
    tb = tri.astype(BF16)
    acc = None
    for _ in range(3):
        hi = x.astype(BF16)
        part = _dot(tb, hi)
        acc = part if acc is None else acc + part
        x = x - hi.astype(F32)
    return acc


def _hgrn2_tables():
    T = HG_T
    r = jnp.arange(T)
    tri = jnp.stack([(r[None, :] <= r[:, None]), (r[None, :] >= r[:, None])]).astype(F32)
    head = jnp.arange(HG_WIDTH) // (HG_WIDTH // HG_HEADS)
    same = (head[:, None] == head[None, :])
    return tri, same.astype(BF16), same.astype(F32)


def _hgrn2_groups():
    T, ngrp = HG_T, HG_T // SUB
    return sum(ngrp - s // SUB for s in range(T))


def _hgrn2_scan(pb, lb):
    B, L, _ = pb.shape
    tb = min(HG_TB, L)
    nblk = L // tb
    tri, sel, bd = _hgrn2_tables()
    W = HG_WIDTH
    fwd = lambda col: pl.BlockSpec((1, tb, W), lambda b, j: (b, j, col))
    bwd = lambda col: pl.BlockSpec((1, tb, W), lambda b, j: (b, nblk - 1 - j, col))
    full = lambda a: pl.BlockSpec(a.shape, lambda b, j: (0,) * a.ndim)
    lb3 = lb.reshape(2, 1, W)
    return pl.pallas_call(
        functools.partial(_hgrn2_kernel, nsub=tb // HG_T),
        grid=(B, nblk),
        in_specs=[fwd(0), bwd(0), fwd(1), bwd(2), fwd(3), bwd(3), full(lb3), full(tri), full(sel), full(bd)],
        out_specs=[fwd(0), bwd(0)],
        out_shape=[jax.ShapeDtypeStruct((B, L, W), F32)] * 2,
        scratch_shapes=[pltpu.VMEM((2, W, W), F32), pltpu.VMEM((2, _hgrn2_groups() * SUB, W), F32)],
        compiler_params=pltpu.CompilerParams(dimension_semantics=("parallel", "arbitrary")),
        name="hgrn2_scan",
    )(pb, pb, pb, pb, pb, pb, lb3, tri, sel, bd)


HY_N2 = 128
HY_K1_TILE = 8
HY_FILT_ROWS = 512
HY_KRON_MAX_N1 = 32


def _dot_hi(a, b):
    return jnp.dot(a, b, precision=lax.Precision.HIGHEST, preferred_element_type=F32)


def _dot_3x(a, b):
    a_hi = a.astype(BF16)
    b_hi = b.astype(BF16)
    a_lo = (a - a_hi.astype(F32)).astype(BF16)
    b_lo = (b - b_hi.astype(F32)).astype(BF16)
    return _dot(a_hi, b_hi) + _dot(a_hi, b_lo) + _dot(a_lo, b_hi)


HY_FILT_PACK = LANES // HY_BANDS


def _hy_filter_kernel(w1t_ref, w1c_ref, w1s_ref, b1_ref, fr_ref, w2_ref, b2_ref, w3_ref, bands_ref,
                      dl_ref, k_ref, sum_ref, *, L):
    tr = k_ref.shape[0]
    P, Hd = HY_FILT_PACK, HY_HIDDEN
    rq = tr // P
    r0 = pl.program_id(0) * tr

    def position(r):
        return jnp.where(r < L, r, 2 * L - r).astype(F32)

    def packed_rows(width, per):
        row = lax.broadcasted_iota(jnp.int32, (rq, width), 0)
        q = lax.broadcasted_iota(jnp.int32, (rq, width), 1) // per
        return r0 + q * rq + row

    bw = (position(packed_rows(LANES, HY_BANDS)) * (2.0 * math.pi / L)) * bands_ref[...]
    t_h = position(packed_rows(P * Hd, Hd)) * (1.0 / (L - 1))
    pre = t_h * w1t_ref[...] + _dot_hi(jnp.cos(bw), w1c_ref[...]) - _dot_hi(jnp.sin(bw), w1s_ref[...])
    fr = fr_ref[...]
    a = jnp.sin(fr * (pre + b1_ref[...]))
    a = jnp.sin(fr * (_dot_hi(a, w2_ref[...]) + b2_ref[...]))
    W2 = HY_ORDER * HY_WIDTH
    a_rows = jnp.concatenate([a[:, q * Hd:(q + 1) * Hd] for q in range(P)], axis=0)
    h = _dot_3x(a_rows, w3_ref[...])
    r = r0 + lax.broadcasted_iota(jnp.int32, (tr, 1), 0)
    decay = jnp.exp(-(position(r) * (1.0 / (L - 1))) * dl_ref[...])
    hf = h[:, :W2] * decay * (r < L).astype(F32)
    hb = h[:, W2:] * decay * ((r > L) | (r == 0)).astype(F32)
    k_ref[...] = hf + hb
    total = jnp.sum(jnp.abs(hf) + jnp.abs(hb), axis=0, keepdims=True)

    @pl.when(pl.program_id(0) == 0)
    def _():
        sum_ref[...] = jnp.zeros_like(sum_ref)

    sum_ref[...] += total


def _hy_filter(L, w1, b1, freq, w2, b2, w3):
    tr = min(HY_FILT_ROWS, L)
    W2 = HY_ORDER * HY_WIDTH
    P = HY_FILT_PACK
    eye = jnp.eye(P, dtype=F32)
    tile = lambda v: jnp.tile(v.reshape(1, -1), (1, P))
    bands = tile(jnp.linspace(1e-4, HY_BANDS - 1, HY_BANDS, dtype=F32))
    w1c = jnp.kron(eye, w1[1:1 + HY_BANDS])
    w1s = jnp.kron(eye, w1[1 + HY_BANDS:])
    w2b = jnp.kron(eye, w2)
    deltas = jnp.abs(jnp.linspace(math.log(HY_DECAY_TARGET) / HY_SLOW_PCT,
                                  math.log(HY_DECAY_TARGET) / HY_FAST_PCT, HY_WIDTH, dtype=F32))
    dl = jnp.tile(deltas, HY_ORDER).reshape(1, W2)
    args = (tile(w1[0]), w1c, w1s, tile(b1), tile(freq), w2b, tile(b2), w3, bands, dl)
    full = lambda a: pl.BlockSpec(a.shape, lambda i: (0, 0))
    return pl.pallas_call(
        functools.partial(_hy_filter_kernel, L=L),
        grid=(2 * L // tr,),
        in_specs=[full(a) for a in args],
        out_specs=[pl.BlockSpec((tr, W2), lambda i: (i, 0)), pl.BlockSpec((1, W2), lambda i: (0, 0))],
        out_shape=[jax.ShapeDtypeStruct((2 * L, W2), F32), jax.ShapeDtypeStruct((1, W2), F32)],
        compiler_params=pltpu.CompilerParams(dimension_semantics=("arbitrary",)),
        name="hyena_filter",
    )(*args)


def _hy_dft_tables(L):
    N = 2 * L
    N2 = HY_N2
    N1 = N // N2
    def cs(n, rows, cols):
        ang = (2.0 * math.pi / n) * ((jnp.arange(rows)[:, None] * jnp.arange(cols)[None, :]) % n).astype(F32)
        return jnp.cos(ang), jnp.sin(ang)
    ca, sa = cs(N1, N1, N1)
    fa_full = jnp.concatenate([ca, -sa], axis=0)
    fa_inv = jnp.concatenate([ca[:N1 // 2], -sa[:N1 // 2]], axis=1) * (1.0 / N)
    cb, sb = cs(N2, N2, N2)
    fb = jnp.concatenate([jnp.concatenate([cb, sb], axis=1), jnp.concatenate([-sb, cb], axis=1)], axis=0)
    tc, ts = cs(N, N1, N2)
    tw = jnp.stack([tc, ts])[..., None] * jnp.ones((1, 1, 1, 128), F32)
    tabs = dict(N1=N1, fa_half=fa_full[:, :N1 // 2].astype(BF16), fa_full=fa_full.astype(BF16),
                fa_inv=fa_inv.astype(BF16), fb=fb.astype(BF16), fb_t=fb.T.astype(BF16), tw=tw)
    if N1 <= HY_KRON_MAX_N1:
        eye = jnp.eye(SUB, dtype=F32)
        tabs['fa_half_kron'] = jnp.kron(fa_full[:, :N1 // 2], eye).astype(BF16)
        tabs['fa_inv_kron'] = jnp.kron(fa_inv, eye).astype(BF16)
    return tabs


def _hy_n2_rows(N1):
    return SUB * max(1, min(HY_N2 // SUB, 256 // N1))


def _hy_dft_a_kernel(m_ref, x_ref, o_ref):
    N1, C = o_ref.shape[2], o_ref.shape[4]
    for g0 in range(0, x_ref.shape[2], SUB):
        x = jnp.concatenate([x_ref[0, :, g0 + j, :] for j in range(SUB)], axis=1).astype(BF16)
        y = _dot(m_ref[...], x)
        for j in range(SUB):
            o_ref[0, 0, :, g0 + j, :] = y[:N1, j * C:(j + 1) * C]
            o_ref[0, 1, :, g0 + j, :] = y[N1:, j * C:(j + 1) * C]


def _hy_dft_a(m, x):
    B, R, N2, C = x.shape
    N1 = m.shape[0] // 2
    W = HY_WIDTH
    G = _hy_n2_rows(N1)
    return pl.pallas_call(
        _hy_dft_a_kernel,
        grid=(B, N2 // G, C // W),
        in_specs=[pl.BlockSpec(m.shape, lambda b, g, c: (0, 0)),
                  pl.BlockSpec((1, R, G, W), lambda b, g, c: (b, 0, g, c))],
        out_specs=pl.BlockSpec((1, 2, N1, G, W), lambda b, g, c: (b, 0, 0, g, c)),
        out_shape=jax.ShapeDtypeStruct((B, 2, N1, N2, C), F32),
        compiler_params=pltpu.CompilerParams(dimension_semantics=("parallel", "parallel", "parallel")),
        name="hyena_dft_a",
    )(m, x)


def _hy_gate_kernel(m_ref, y_ref, z_ref, g_ref, s_ref, o_ref):
    C = o_ref.shape[3]
    for g0 in range(0, o_ref.shape[2], SUB):
        y = jnp.concatenate(
            [jnp.concatenate([y_ref[0, 0, :, g0 + j, :], y_ref[0, 1, :, g0 + j, :]], axis=0) for j in range(SUB)],
            axis=1)
        zc = _dot(m_ref[...], y.astype(BF16))
        for j in range(SUB):
            o_ref[0, :, g0 + j, :] = g_ref[0, :, g0 + j, :] * (
                zc[:, j * C:(j + 1) * C] + s_ref[...] * z_ref[0, :, g0 + j, :])


def _hy_inverse_gate(m, y, z, gate, skip_row):
    B, _, N1, N2, C = y.shape
    G = _hy_n2_rows(N1)
    tok = pl.BlockSpec((1, N1 // 2, G, C), lambda b, g: (b, 0, g, 0))
    return pl.pallas_call(
        _hy_gate_kernel,
        grid=(B, N2 // G),
        in_specs=[pl.BlockSpec(m.shape, lambda b, g: (0, 0)),
                  pl.BlockSpec((1, 2, N1, G, C), lambda b, g: (b, 0, 0, g, 0)),
                  tok, tok,
                  pl.BlockSpec((1, C), lambda b, g: (0, 0))],
        out_specs=tok,
        out_shape=jax.ShapeDtypeStruct((B, N1 // 2, N2, C), F32),
        compiler_params=pltpu.CompilerParams(dimension_semantics=("parallel", "parallel")),
        name="hyena_idft_a_gate",
    )(m, y, z, gate, skip_row)


def _cmul(ar, ai, br, bi):
    return ar * br - ai * bi, ar * bi + ai * br


def _hy_stage_b_one(ar, ai, tcos, tsin, fb, fbt, kr, ki):
    N2 = HY_N2
    tc = jnp.concatenate([tcos, tcos], axis=1)
    ts = jnp.concatenate([tsin, tsin], axis=1)
    br, bi = _cmul(ar, ai, tc, -ts)
    x = _dot(fb, jnp.concatenate([br, bi], axis=0).astype(BF16))
    yr, yi = _cmul(x[:N2], x[N2:], kr, ki)
    y = _dot(fbt, jnp.concatenate([yr, yi], axis=0).astype(BF16))
    return _cmul(y[:N2], y[N2:], tc, ts)


def _hy_fused_kernel(ma_ref, mi_ref, tw_ref, fb_ref, fbt_ref, k_ref, z_ref, g_ref, s_ref, o_ref, w_sc):
    R, N2, C = z_ref.shape[1:]
    N1 = k_ref.shape[1]
    for g0 in range(0, N2, SUB):
        x = z_ref[0, :, g0:g0 + SUB, :].reshape(R * SUB, C).astype(BF16)
        w_sc[:, :, g0:g0 + SUB, :] = _dot(ma_ref[...], x).reshape(2, N1, SUB, C)

    def k1_step(j, carry):
        yr, yi = _hy_stage_b_one(w_sc[0, j], w_sc[1, j], tw_ref[0, j], tw_ref[1, j], fb_ref[...], fbt_ref[...],
                                 k_ref[0, j], k_ref[1, j])
        w_sc[0, j] = yr
        w_sc[1, j] = yi
        return carry

    lax.fori_loop(0, N1, k1_step, 0, unroll=8)
    for g0 in range(0, N2, SUB):
        y = w_sc[:, :, g0:g0 + SUB, :].reshape(2 * N1 * SUB, C).astype(BF16)
        zc = _dot(mi_ref[...], y).reshape(R, SUB, C)
        rows = slice(g0, g0 + SUB)
        o_ref[0, :, rows, :] = g_ref[0, :, rows, :] * (zc + s_ref[...] * z_ref[0, :, rows, :])


def _hy_fused_conv(tabs, kspec, order, z, gate, skip_row):
    B, R, N2, C = z.shape
    N1 = tabs['N1']
    once = lambda a: pl.BlockSpec(a.shape, lambda b: (0,) * a.ndim, pipeline_mode=pl.Buffered(1))
    tok = pl.BlockSpec((1, R, N2, C), lambda b: (b, 0, 0, 0))
    ma, mi = tabs['fa_half_kron'], tabs['fa_inv_kron']
    return pl.pallas_call(
        _hy_fused_kernel,
        grid=(B,),
        in_specs=[once(ma), once(mi), once(tabs['tw']), once(tabs['fb']), once(tabs['fb_t']),
                  pl.BlockSpec((2, N1, N2, C), lambda b: (0, 0, 0, order), pipeline_mode=pl.Buffered(1)),
                  tok, tok, pl.BlockSpec((1, C), lambda b: (0, 0))],
        out_specs=tok,
        out_shape=jax.ShapeDtypeStruct(z.shape, F32),
        scratch_shapes=[pltpu.VMEM((2, N1, N2, C), F32)],
        compiler_params=pltpu.CompilerParams(dimension_semantics=("parallel",)),
        name="hyena_conv_fused",
    )(ma, mi, tabs['tw'], tabs['fb'], tabs['fb_t'], kspec, z, gate, skip_row)


def _hy_stage_b_kernel(a_ref, tw_ref, fb_ref, fbt_ref, k_ref, o_ref, *, conv):
    N2 = HY_N2
    for j in range(a_ref.shape[2]):
        if conv:
            xr, xi = _hy_stage_b_one(a_ref[0, 0, j], a_ref[0, 1, j], tw_ref[0, j], tw_ref[1, j], fb_ref[...],
                                     fbt_ref[...], k_ref[0, j], k_ref[1, j])
        else:
            tc = jnp.concatenate([tw_ref[0, j], tw_ref[0, j]], axis=1)
            ts = jnp.concatenate([tw_ref[1, j], tw_ref[1, j]], axis=1)
            br, bi = _cmul(a_ref[0, 0, j], a_ref[0, 1, j], tc, -ts)
            x = _dot(fb_ref[...], jnp.concatenate([br, bi], axis=0).astype(BF16))
            xr, xi = x[:N2] * k_ref[...], x[N2:] * k_ref[...]
        o_ref[0, 0, j] = xr
        o_ref[0, 1, j] = xi


def _hy_stage_b(a, tabs, kspec=None, order=0, scale=None):
    B, _, N1, N2, C = a.shape
    t1 = HY_K1_TILE
    conv = kspec is not None
    blk = pl.BlockSpec((1, 2, t1, N2, HY_WIDTH), lambda i, b, c: (b, 0, i, 0, c))
    if conv:
        k_arr = kspec
        k_spec = pl.BlockSpec((2, t1, N2, HY_WIDTH), lambda i, b, c: (0, i, 0, order))
    else:
        k_arr = scale
        k_spec = pl.BlockSpec((1, HY_WIDTH), lambda i, b, c: (0, c))
    return pl.pallas_call(
        functools.partial(_hy_stage_b_kernel, conv=conv),
        grid=(N1 // t1, B, C // HY_WIDTH),
        in_specs=[blk,
                  pl.BlockSpec((2, t1, N2, 128), lambda i, b, c: (0, i, 0, 0)),
                  pl.BlockSpec(tabs['fb'].shape, lambda i, b, c: (0, 0)),
                  pl.BlockSpec(tabs['fb_t'].shape, lambda i, b, c: (0, 0)),
                  k_spec],
        out_specs=blk,
        out_shape=jax.ShapeDtypeStruct(a.shape, F32),
        compiler_params=pltpu.CompilerParams(dimension_semantics=("parallel", "parallel", "parallel")),
        name="hyena_stage_b_conv" if conv else "hyena_stage_b_spectrum",
    )(a, tabs['tw'], tabs['fb'], tabs['fb_t'], k_arr)


def _hy_filter_spectrum(L, tabs, w1, b1, freq, w2, b2, w3):
    N1, N2 = tabs['N1'], HY_N2
    W2 = HY_ORDER * HY_WIDTH
    k, total = _hy_filter(L, w1, b1, freq, w2, b2, w3)
    a = _hy_dft_a(tabs['fa_full'], k.reshape(1, N1, N2, W2))
    return _hy_stage_b(a, tabs, scale=1.0 / total)[0]


def _hyena(v, x1, x2, skip, tabs, kspec):
    B, L, _ = v.shape
    N1, N2, C = tabs['N1'], HY_N2, HY_WIDTH
    grid = lambda t: t.reshape(B, N1 // 2, N2, C)
    z = grid(v)
    for n, gate in enumerate((x1, x2)):
        if 'fa_half_kron' in tabs:
            z = _hy_fused_conv(tabs, kspec, n, z, grid(gate), skip[n].reshape(1, C))
        else:
            a = _hy_dft_a(tabs['fa_half'], z)
            y = _hy_stage_b(a, tabs, kspec, order=n)
            z = _hy_inverse_gate(tabs['fa_inv'], y, z, grid(gate), skip[n].reshape(1, C))
    return z.reshape(B, L, C)


def _prep_layer_weights(l, w_in, w_out, na_rpb, mla_w_uq, mla_w_ukv, ffn_w_up, ffn_conv, ffn_w_down):
    wi = w_in[l]
    s0, s1, s2 = NA_COLS, NA_COLS + HG_COLS, NA_COLS + HG_COLS + HY_COLS
    w_ml = jnp.pad(wi[:, s2:], ((0, 0), (0, MLA_COLS_PAD - MLA_COLS)))
    return dict(
        w_na=wi[:, :s0].astype(BF16), w_hg=wi[:, s0:s1].astype(BF16), w_hy=wi[:, s1:s2].astype(BF16),
        w_ml=w_ml.astype(BF16), w_out=w_out[l].astype(BF16), na_bias=_na_bias_table(na_rpb[l]),
        mla=_mla_weights(mla_w_uq[l], mla_w_ukv[l]),
        ffn=_ffn_weight_tiles(ffn_w_up[l], ffn_conv[l], ffn_w_down[l]),
    )


def _trunk_layer(x, mod, lb, g, lw, p, cos, sin, hy_tabs, hy_spec):
    L = x.shape[1]
    pa, pb, hv, hx1, hx2, q, k, v = _in_proj(x, mod, g, lw['w_na'], lw['w_hg'], lw['w_hy'], lw['w_ml'], p['hy_short'],
                                             cos[:L], sin[:L], p['mla_q_norm'], p['mla_kv_norm'], lw['mla'])
    o_na = _neighbourhood_attention(pa, lw['na_bias'])
    hg_dirs = _hgrn2_scan(pb, lb)
    o_hy = _hyena(hv, hx1, hx2, p['hy_skip'], hy_tabs, hy_spec)
    o_ml = _mla_flash(q, k, v)
    x1, h2 = _out_proj(x, mod, g, o_na, hg_dirs, pb, p['hg_norm'], o_hy, o_ml, lw['w_out'])
    return _conv_ffn(h2, x1, mod, g, *lw['ffn'])


def kernel(x_prompt, x_sample, c_prompt, c_sample, ada_w, ada_b, norm_g, w_in, w_out, na_rpb, hg_lb,
           hg_norm, hy_short, hy_w1, hy_b1, hy_freq, hy_w2, hy_b2, hy_w3, hy_skip, mla_q_norm,
           mla_kv_norm, mla_w_uq, mla_w_ukv, ffn_w_up, ffn_conv, ffn_w_down):
    Bp, Bs = x_prompt.shape[0], x_sample.shape[0]
    Lp, Ls = x_prompt.shape[1], x_sample.shape[1]
    lb_soft = jax.nn.softmax(hg_lb.astype(F32), axis=0)
    lower_bounds = jnp.cumsum(lb_soft, axis=0) - lb_soft[:1]

    R = -(-(Bp + Bs) // 8) * 8
    cond = jnp.zeros((R, D_MODEL), F32).at[:Bp].set(c_prompt).at[Bp:Bp + Bs].set(c_sample)
    mod = _ada_modulation(cond, ada_w, ada_b).reshape(DEPTH, R, 6, D_MODEL)
    mod = jnp.pad(mod, ((0, 0), (0, 0), (0, 2), (0, 0)))

    cos, sin = _rope_tables(max(Lp, Ls))
    tabs_p, tabs_s = _hy_dft_tables(Lp), _hy_dft_tables(Ls)
    y_prompt, y_sample = x_prompt, x_sample
    for l in range(DEPTH):
        lw = _prep_layer_weights(l, w_in, w_out, na_rpb, mla_w_uq, mla_w_ukv, ffn_w_up, ffn_conv, ffn_w_down)
        p = dict(hg_norm=hg_norm[l], hy_short=hy_short[l], hy_skip=hy_skip[l],
                 mla_q_norm=mla_q_norm[l].reshape(1, -1), mla_kv_norm=mla_kv_norm[l].reshape(1, -1))
        filt = (hy_w1[l], hy_b1[l], hy_freq[l], hy_w2[l], hy_b2[l], hy_w3[l])
        spec_p = _hy_filter_spectrum(Lp, tabs_p, *filt)
        spec_s = _hy_filter_spectrum(Ls, tabs_s, *filt)
        y_prompt = _trunk_layer(y_prompt, mod[l, :Bp], lower_bounds[l], norm_g[l], lw, p, cos, sin,
                                tabs_p, spec_p)
        y_sample = _trunk_layer(y_sample, mod[l, Bp:Bp + Bs], lower_bounds[l], norm_g[l], lw, p, cos, sin,
                                tabs_s, spec_s)
    return (y_prompt, y_sample)
```
